```python
import math
import jax
import jax.numpy as jnp
from jax import lax
import numpy as np

D_MODEL = 2048
BATCH = 4
SEQ = 4096
DEPTH = 4
DEC_BATCH = 8
DEC_SEQ = 64
PAST_LEN = 1024

CHUNK = 64
HEAD_DIM = 64
N_BRANCH = 4
BRANCH_WIDTH = D_MODEL // 2
D_FF = 4 * D_MODEL
NORM_EPS = 1e-6

RW_HEADS = BRANCH_WIDTH // HEAD_DIM
RW_DECAY_RANK = 64
RW_ICLR_RANK = 64
RW_GATE_RANK = 128
RW_IN = 3 * BRANCH_WIDTH + RW_DECAY_RANK + RW_ICLR_RANK + RW_GATE_RANK
RW_GN_EPS = HEAD_DIM * 1e-5

POOL_WINDOWS = (2, 4, 8, 16)
POOL_GROUPS = 4
POOL_GROUP_WIDTH = BRANCH_WIDTH // POOL_GROUPS
POOL_HIST = 15

SWA_Q_HEADS = BRANCH_WIDTH // HEAD_DIM
SWA_KV_HEADS = 4
SWA_GROUP = SWA_Q_HEADS // SWA_KV_HEADS
WINDOW = 128
WINDOW_CHUNKS = WINDOW // CHUNK
SWA_IN = (SWA_Q_HEADS + 2 * SWA_KV_HEADS) * HEAD_DIM

SSM_HEADS = BRANCH_WIDTH // HEAD_DIM
SSM_GROUPS = 2
SSM_STATE = 128
SSM_CONV = 4
SSM_CONV_DIM = BRANCH_WIDTH + 2 * SSM_GROUPS * SSM_STATE
SSM_IN = BRANCH_WIDTH + SSM_CONV_DIM + SSM_HEADS

IN_COLS = RW_IN + BRANCH_WIDTH + SWA_IN + SSM_IN

kernel_name = 'hybrid_stream_rwkv7_pool_swa_ssd'


def _rms(x, g):
    xf = x.astype(jnp.float32)
    y = xf * lax.rsqrt(jnp.mean(xf * xf, axis=-1, keepdims=True) + NORM_EPS)
    return (y * g.astype(jnp.float32)).astype(x.dtype)


def _alibi_slopes():
    h = jnp.arange(1, SWA_Q_HEADS + 1, dtype=jnp.float32)
    return jnp.exp2(-8.0 * h / SWA_Q_HEADS)


def _rwkv_scan(r, w, k, v, kk, a, s0):
    def step(s, inp):
        r_t, w_t, k_t, v_t, kk_t, a_t = inp
        sa = jnp.einsum('bhij,bhj->bhi', s, -kk_t)
        s = (s * w_t[:, :, None, :] + sa[..., None] * (kk_t * a_t)[:, :, None, :]
             + v_t[..., None] * k_t[:, :, None, :])
        return s, jnp.einsum('bhij,bhj->bhi', s, r_t)
    xs = tuple(jnp.swapaxes(z, 0, 1) for z in (r, w, k, v, kk, a))
    s_fin, ys = lax.scan(step, s0, xs)
    return jnp.swapaxes(ys, 0, 1), s_fin


def _rwkv7(u, shift_hist, s0, mu, w0, w2, a0, a2, g2, k_k, k_a, r_k, ln_w, ln_b):
    f32 = jnp.float32
    bsz, t, _ = u.shape
    bw = BRANCH_WIDTH
    prev = jnp.concatenate([shift_hist.astype(u.dtype), u[:, :-1]], axis=1)
    xs = u + (prev - u) * mu
    r = xs[..., :bw]
    k = xs[..., bw:2 * bw]
    v = xs[..., 2 * bw:3 * bw]
    o = 3 * bw
    wd = xs[..., o:o + RW_DECAY_RANK]
    o += RW_DECAY_RANK
    ad = xs[..., o:o + RW_ICLR_RANK]
    o += RW_ICLR_RANK
    gd = xs[..., o:o + RW_GATE_RANK]
    w_log = -jax.nn.softplus(-(w0 + jnp.tanh(wd) @ w2).astype(f32)) - 0.5
    decay = jnp.exp(-jnp.exp(w_log))
    a = jax.nn.sigmoid((a0 + ad @ a2).astype(f32))
    g = jax.nn.sigmoid(gd) @ g2

    def heads(z):
        return z.astype(f32).reshape(bsz, t, RW_HEADS, HEAD_DIM)

    kk = heads(k * k_k)
    kk = kk / jnp.maximum(jnp.sqrt(jnp.sum(kk * kk, axis=-1, keepdims=True)), 1e-12)
    a_h = heads(a)
    k_h = heads(k) * (1.0 + (a_h - 1.0) * k_a.astype(f32).reshape(RW_HEADS, HEAD_DIM))
    r_h = heads(r)
    v_h = heads(v)
    y, s_fin = _rwkv_scan(r_h, heads(decay), k_h, v_h, kk, a_h, s0.astype(f32))
    mean = jnp.mean(y, axis=-1, keepdims=True)
    var = jnp.mean(jnp.square(y - mean), axis=-1, keepdims=True)
    y = ((y - mean) * lax.rsqrt(var + RW_GN_EPS)).reshape(bsz, t, bw) * ln_w + ln_b
    bonus = jnp.sum(r_h * k_h * r_k.astype(f32), axis=-1, keepdims=True) * v_h
    y = (y + bonus.reshape(bsz, t, bw)) * g
    return y.astype(u.dtype), u[:, -1:], s_fin


def _pool(u, hist, pos0, w_pool, scale):
    f32 = jnp.float32
    bsz, t, _ = u.shape
    uf = u.astype(f32)
    full = jnp.concatenate([hist.astype(f32), uf], axis=1)
    cs = jnp.cumsum(full, axis=1)
    cs = jnp.concatenate([jnp.zeros_like(cs[:, :1]), cs], axis=1)
    pos = pos0 + jnp.arange(t)
    end = POOL_HIST + 1
    groups = []
    for gi, win in enumerate(POOL_WINDOWS):
        sl = slice(gi * POOL_GROUP_WIDTH, (gi + 1) * POOL_GROUP_WIDTH)
        wsum = cs[:, end:end + t, sl] - cs[:, end - win:end - win + t, sl]
        cnt = jnp.minimum(pos + 1, win).astype(f32)[None, :, None]
        groups.append(wsum / cnt - uf[..., sl])
    p = jnp.stack(groups, axis=2)
    y = jnp.einsum('btgc,gcd->btgd', p, w_pool.astype(f32)).reshape(bsz, t, BRANCH_WIDTH)
    y = y * scale
    return y.astype(u.dtype), full[:, -POOL_HIST:].astype(u.dtype)


def _swa(u, hist_k, hist_v, pos0, sinks):
    f32 = jnp.float32
    bsz, t, _ = u.shape
    nq = SWA_Q_HEADS * HEAD_DIM
    nk = SWA_KV_HEADS * HEAD_DIM
    q = u[..., :nq].reshape(bsz, t, SWA_KV_HEADS, SWA_GROUP, HEAD_DIM)
    k = u[..., nq:nq + nk].reshape(bsz, t, SWA_KV_HEADS, HEAD_DIM)
    v = u[..., nq + nk:].reshape(bsz, t, SWA_KV_HEADS, HEAD_DIM)
    nh = hist_k.shape[1]
    k_all = jnp.concatenate([hist_k.astype(k.dtype), k], axis=1)
    v_all = jnp.concatenate([hist_v.astype(v.dtype), v], axis=1)
    kpos = pos0 - nh + jnp.arange(nh + t)
    qpos = pos0 + jnp.arange(t)
    if t <= CHUNK:
        qb, kb, vb = q[:, None], k_all[:, None], v_all[:, None]
        qp, kp = qpos[None], kpos[None]
    else:
        nb = t // CHUNK
        hb = nh // CHUNK
        qb = q.reshape(bsz, nb, CHUNK, SWA_KV_HEADS, SWA_GROUP, HEAD_DIM)
        kc = k_all.reshape(bsz, nb + hb, CHUNK, SWA_KV_HEADS, HEAD_DIM)
        vc = v_all.reshape(bsz, nb + hb, CHUNK, SWA_KV_HEADS, HEAD_DIM)
        pc = kpos.reshape(nb + hb, CHUNK)
        kb = jnp.concatenate([kc[:, i:i + nb] for i in range(hb + 1)], axis=2)
        vb = jnp.concatenate([vc[:, i:i + nb] for i in range(hb + 1)], axis=2)
        kp = jnp.concatenate([pc[i:i + nb] for i in range(hb + 1)], axis=1)
        qp = qpos.reshape(nb, CHUNK)
    s = jnp.einsum('bnqkgd,bnskd->bnkgqs', qb.astype(f32), kb.astype(f32)) * (HEAD_DIM ** -0.5)
    dist = jnp.abs(qp[:, :, None] - kp[:, None, :]).astype(f32)
    slopes = _alibi_slopes().reshape(SWA_KV_HEADS, SWA_GROUP)
    s = s - slopes[None, None, :, :, None, None] * dist[None, :, None, None]
    qc = jnp.floor_divide(qp, CHUNK)[:, :, None]
    kch = jnp.floor_divide(kp, CHUNK)[:, None, :]
    allowed = (kp[:, None, :] >= 0) & (kch >= qc - WINDOW_CHUNKS) & (kch <= qc)
    s = jnp.where(allowed[None, :, None, None], s, -jnp.inf)
    sink = jnp.broadcast_to(
        sinks.astype(f32).reshape(SWA_KV_HEADS, SWA_GROUP)[None, None, :, :, None, None],
        s.shape[:-1] + (1,))
    p = jax.nn.softmax(jnp.concatenate([s, sink], axis=-1), axis=-1)[..., :-1]
    o = jnp.einsum('bnkgqs,bnskd->bnqkgd', p, vb.astype(f32)).reshape(bsz, t, nq)
    return o.astype(u.dtype), k_all[:, -nh:], v_all[:, -nh:]


def _segsum(a):
    t = a.shape[-1]
    rep = jnp.broadcast_to(a[..., :, None], a.shape + (t,))
    rep = jnp.where(jnp.tril(jnp.ones((t, t), dtype=bool), -1), rep, 0.0)
    ss = jnp.cumsum(rep, axis=-2)
    return jnp.where(jnp.tril(jnp.ones((t, t), dtype=bool)), ss, -jnp.inf)


def _ssd_scan(x, a, bm, cm, s0):
    bsz, t, h, p = x.shape
    cl = min(CHUNK, t)
    nc = t // cl
    rep = h // bm.shape[2]
    bh = jnp.repeat(bm, rep, axis=2).reshape(bsz, nc, cl, h, SSM_STATE)
    ch = jnp.repeat(cm, rep, axis=2).reshape(bsz, nc, cl, h, SSM_STATE)
    xc = x.reshape(bsz, nc, cl, h, p)
    ac = a.reshape(bsz, nc, cl, h).transpose(0, 3, 1, 2)
    a_cs = jnp.cumsum(ac, axis=-1)
    lmat = jnp.exp(_segsum(ac))
    scores = jnp.einsum('bclhn,bcshn->bhcls', ch, bh) * lmat
    y_diag = jnp.einsum('bhcls,bcshp->bclhp', scores, xc)
    decay_states = jnp.exp(a_cs[..., -1:] - a_cs)
    states = jnp.einsum('bclhn,bhcl,bclhp->bchpn', bh, decay_states, xc)
    states = jnp.concatenate([s0[:, None], states], axis=1)
    chunk_decay = jnp.exp(_segsum(jnp.pad(a_cs[..., -1], ((0, 0), (0, 0), (1, 0)))))
    new_states = jnp.einsum('bhzc,bchpn->bzhpn', chunk_decay, states)
    states, final = new_states[:, :-1], new_states[:, -1]
    y_off = jnp.einsum('bclhn,bchpn,bhcl->bclhp', ch, states, jnp.exp(a_cs))
    return (y_diag + y_off).reshape(bsz, t, h, p), final


def _ssd(u, conv_hist, s0, conv_w, conv_b, dt_bias, a_log, d_skip, norm_w):
    f32 = jnp.float32
    bsz, t, _ = u.shape
    bw = BRANCH_WIDTH
    gn = SSM_GROUPS * SSM_STATE
    z = u[..., :bw]
    xbc = u[..., bw:bw + SSM_CONV_DIM]
    dt_raw = u[..., bw + SSM_CONV_DIM:]
    full = jnp.concatenate([conv_hist.astype(u.dtype), xbc], axis=1)
    conv = conv_b + full[:, 0:t] * conv_w[0]
    for i in range(1, SSM_CONV):
        conv = conv + full[:, i:i + t] * conv_w[i]
    xbc_act = jax.nn.silu(conv.astype(f32))
    x = xbc_act[..., :bw].reshape(bsz, t, SSM_HEADS, HEAD_DIM)
    bm = xbc_act[..., bw:bw + gn].reshape(bsz, t, SSM_GROUPS, SSM_STATE)
    cm = xbc_act[..., bw + gn:].reshape(bsz, t, SSM_GROUPS, SSM_STATE)
    dt = jax.nn.softplus(dt_raw.astype(f32) + dt_bias.astype(f32))
    a = -jnp.exp(a_log.astype(f32))
    y, s_fin = _ssd_scan(x * dt[..., None], a * dt, bm, cm, s0.astype(f32))
    y = y + x * d_skip.astype(f32)[:, None]
    y = y.reshape(bsz, t, bw) * jax.nn.silu(z.astype(f32))
    yg = y.reshape(bsz, t, SSM_GROUPS, bw // SSM_GROUPS)
    yg = yg * lax.rsqrt(jnp.mean(yg * yg, axis=-1, keepdims=True) + NORM_EPS)
    y = yg.reshape(bsz, t, bw) * norm_w.astype(f32)
    return y.astype(u.dtype), full[:, -(SSM_CONV - 1):], s_fin


def setup_inputs(seed: int = 0) -> dict:
    key = jax.random.key(seed)
    ks = iter(jax.random.split(key, 64))
    f32 = jnp.float32

    def nrm(shape, scale=1.0):
        return jax.random.normal(next(ks), shape, f32) * scale

    def gain(shape):
        return 1.0 + 0.05 * jax.random.normal(next(ks), shape, f32)

    swa_rows = min(WINDOW, PAST_LEN)
    inp = {}
    inp['x_prompt'] = nrm((BATCH, SEQ, D_MODEL))
    inp['x_sample'] = nrm((DEC_BATCH, DEC_SEQ, D_MODEL))
    inp['state_rwkv'] = nrm((DEPTH, DEC_BATCH, RW_HEADS, HEAD_DIM, HEAD_DIM), 0.1)
    inp['state_rwkv_shift'] = nrm((DEPTH, DEC_BATCH, 1, RW_IN))
    inp['state_pool'] = nrm((DEPTH, DEC_BATCH, POOL_HIST, BRANCH_WIDTH))
    inp['cache_swa_k'] = nrm((DEPTH, DEC_BATCH, swa_rows, SWA_KV_HEADS, HEAD_DIM))
    inp['cache_swa_v'] = nrm((DEPTH, DEC_BATCH, swa_rows, SWA_KV_HEADS, HEAD_DIM))
    inp['state_ssm'] = nrm((DEPTH, DEC_BATCH, SSM_HEADS, HEAD_DIM, SSM_STATE), 0.1)
    inp['state_ssm_conv'] = nrm((DEPTH, DEC_BATCH, SSM_CONV - 1, SSM_CONV_DIM))
    inp['c_prompt'] = nrm((BATCH, D_MODEL))
    inp['c_sample'] = nrm((DEC_BATCH, D_MODEL))
    inp['ada_w'] = nrm((DEPTH, D_MODEL, 6 * D_MODEL), 0.5 * D_MODEL ** -0.5)
    inp['ada_b'] = nrm((DEPTH, 6 * D_MODEL), 0.01)
    inp['norm_mix'] = gain((DEPTH, D_MODEL))
    inp['norm_ffn'] = gain((DEPTH, D_MODEL))
    inp['w_in'] = nrm((DEPTH, D_MODEL, IN_COLS), D_MODEL ** -0.5)
    inp['rw_mu'] = jax.random.uniform(next(ks), (DEPTH, RW_IN), f32)
    inp['rw_w0'] = jax.random.uniform(next(ks), (DEPTH, BRANCH_WIDTH), f32, -6.0, 1.0)
    inp['rw_w2'] = nrm((DEPTH, RW_DECAY_RANK, BRANCH_WIDTH), 0.5 * RW_DECAY_RANK ** -0.5)
    inp['rw_a0'] = nrm((DEPTH, BRANCH_WIDTH), 0.5)
    inp['rw_a2'] = nrm((DEPTH, RW_ICLR_RANK, BRANCH_WIDTH), 0.5 * RW_ICLR_RANK ** -0.5)
    inp['rw_g2'] = nrm((DEPTH, RW_GATE_RANK, BRANCH_WIDTH), RW_GATE_RANK ** -0.5)
    inp['rw_kk'] = 0.85 + nrm((DEPTH, BRANCH_WIDTH), 0.05)
    inp['rw_ka'] = gain((DEPTH, BRANCH_WIDTH))
    inp['rw_rk'] = nrm((DEPTH, RW_HEADS, HEAD_DIM), 0.1)
    inp['rw_ln_w'] = gain((DEPTH, BRANCH_WIDTH))
    inp['rw_ln_b'] = nrm((DEPTH, BRANCH_WIDTH), 0.01)
    inp['pool_w'] = nrm((DEPTH, POOL_GROUPS, POOL_GROUP_WIDTH, POOL_GROUP_WIDTH), POOL_GROUP_WIDTH ** -0.5)
    inp['pool_scale'] = gain((DEPTH, BRANCH_WIDTH))
    inp['swa_sinks'] = nrm((DEPTH, SWA_Q_HEADS), 0.5)
    inp['ssm_conv_w'] = nrm((DEPTH, SSM_CONV, SSM_CONV_DIM), SSM_CONV ** -0.5)
    inp['ssm_conv_b'] = nrm((DEPTH, SSM_CONV_DIM), 0.01)
    dt0 = jnp.exp(jax.random.uniform(next(ks), (DEPTH, SSM_HEADS), f32,
                                     math.log(1e-3), math.log(1e-1)))
    inp['ssm_dt_bias'] = dt0 + jnp.log(-jnp.expm1(-dt0))
    inp['ssm_a_log'] = jnp.log(jax.random.uniform(next(ks), (DEPTH, SSM_HEADS), f32, 1.0, 16.0))
    inp['ssm_d'] = gain((DEPTH, SSM_HEADS))
    inp['ssm_norm'] = gain((DEPTH, BRANCH_WIDTH))
    inp['w_gate'] = nrm((DEPTH, N_BRANCH, D_MODEL, D_MODEL), D_MODEL ** -0.5)
    inp['w_branch'] = nrm((DEPTH, N_BRANCH, BRANCH_WIDTH, D_MODEL), BRANCH_WIDTH ** -0.5)
    inp['w_out'] = nrm((DEPTH, D_MODEL, D_MODEL), D_MODEL ** -0.5)
    inp['w_up'] = nrm((DEPTH, D_MODEL, D_FF), D_MODEL ** -0.5)
    inp['w_down'] = nrm((DEPTH, D_FF, D_MODEL), D_FF ** -0.5)
    inp['final_norm'] = gain((D_MODEL,))
    return inp


def reference(x_prompt, x_sample, state_rwkv, state_rwkv_shift, state_pool, cache_swa_k,
              cache_swa_v, state_ssm, state_ssm_conv, c_prompt, c_sample, ada_w, ada_b,
              norm_mix, norm_ffn, w_in, rw_mu, rw_w0, rw_w2, rw_a0, rw_a2, rw_g2, rw_kk,
              rw_ka, rw_rk, rw_ln_w, rw_ln_b, pool_w, pool_scale, swa_sinks, ssm_conv_w,
              ssm_conv_b, ssm_dt_bias, ssm_a_log, ssm_d, ssm_norm, w_gate, w_branch, w_out,
              w_up, w_down, final_norm):
    o_pool = RW_IN
    o_swa = o_pool + BRANCH_WIDTH
    o_ssm = o_swa + SWA_IN

    def run_group(x, c, st_rwkv, st_shift, st_pool, ck, cv, st_ssm, st_conv, pos0):
        new_rwkv, new_shift, new_pool, new_k, new_v, new_ssm, new_conv = [], [], [], [], [], [], []
        for l in range(DEPTH):
            mod = jax.nn.silu(c) @ ada_w[l] + ada_b[l]
            sh1, sc1, g1, sh2, sc2, g2 = jnp.split(mod, 6, axis=-1)
            h = _rms(x, norm_mix[l]) * (1.0 + sc1[:, None]) + sh1[:, None]
            u = h @ w_in[l]
            y_a, n_shift, n_rwkv = _rwkv7(u[..., :o_pool], st_shift[l], st_rwkv[l], rw_mu[l],
                                          rw_w0[l], rw_w2[l], rw_a0[l], rw_a2[l], rw_g2[l],
                                          rw_kk[l], rw_ka[l], rw_rk[l], rw_ln_w[l], rw_ln_b[l])
            y_b, n_pool = _pool(u[..., o_pool:o_swa], st_pool[l], pos0, pool_w[l], pool_scale[l])
            y_c, n_k, n_v = _swa(u[..., o_swa:o_ssm], ck[l], cv[l], pos0, swa_sinks[l])
            y_d, n_conv, n_ssm = _ssd(u[..., o_ssm:], st_conv[l], st_ssm[l], ssm_conv_w[l],
                                      ssm_conv_b[l], ssm_dt_bias[l], ssm_a_log[l], ssm_d[l],
                                      ssm_norm[l])
            merged = jax.nn.sigmoid(h @ w_gate[l, 0]) * (y_a @ w_branch[l, 0])
            merged = merged + jax.nn.sigmoid(h @ w_gate[l, 1]) * (y_b @ w_branch[l, 1])
            merged = merged + jax.nn.sigmoid(h @ w_gate[l, 2]) * (y_c @ w_branch[l, 2])
            merged = merged + jax.nn.sigmoid(h @ w_gate[l, 3]) * (y_d @ w_branch[l, 3])
            x = x + g1[:, None] * (merged @ w_out[l])
            h2 = _rms(x, norm_ffn[l]) * (1.0 + sc2[:, None]) + sh2[:, None]
            x = x + g2[:, None] * (jnp.square(jax.nn.relu(h2 @ w_up[l])) @ w_down[l])
            new_rwkv.append(n_rwkv.astype(x.dtype))
            new_shift.append(n_shift.astype(x.dtype))
            new_pool.append(n_pool.astype(x.dtype))
            new_k.append(n_k.astype(x.dtype))
            new_v.append(n_v.astype(x.dtype))
            new_ssm.append(n_ssm.astype(x.dtype))
            new_conv.append(n_conv.astype(x.dtype))
        y = _rms(x, final_norm)
        return y, (jnp.stack(new_rwkv), jnp.stack(new_shift), jnp.stack(new_pool),
                   jnp.stack(new_k), jnp.stack(new_v), jnp.stack(new_ssm), jnp.stack(new_conv))

    bp = x_prompt.shape[0]
    dtp = x_prompt.dtype
    z_rwkv = jnp.zeros((DEPTH, bp, RW_HEADS, HEAD_DIM, HEAD_DIM), dtp)
    z_shift = jnp.zeros((DEPTH, bp, 1, RW_IN), dtp)
    z_pool = jnp.zeros((DEPTH, bp, POOL_HIST, BRANCH_WIDTH), dtp)
    z_k = jnp.zeros((DEPTH, bp, WINDOW, SWA_KV_HEADS, HEAD_DIM), dtp)
    z_v = jnp.zeros((DEPTH, bp, WINDOW, SWA_KV_HEADS, HEAD_DIM), dtp)
    z_ssm = jnp.zeros((DEPTH, bp, SSM_HEADS, HEAD_DIM, SSM_STATE), dtp)
    z_conv = jnp.zeros((DEPTH, bp, SSM_CONV - 1, SSM_CONV_DIM), dtp)
    y_prompt, (p_rwkv, p_shift, p_pool, p_k, p_v, p_ssm, p_conv) = run_group(
        x_prompt, c_prompt, z_rwkv, z_shift, z_pool, z_k, z_v, z_ssm, z_conv, 0)

    y_sample, (s_rwkv, s_shift, s_pool, s_k, s_v, s_ssm, s_conv) = run_group(
        x_sample, c_sample, state_rwkv, state_rwkv_shift, state_pool, cache_swa_k,
        cache_swa_v, state_ssm, state_ssm_conv, PAST_LEN)

    return (y_prompt, y_sample, p_rwkv, p_shift, p_pool, p_k, p_v, p_ssm, p_conv,
            s_rwkv, s_shift, s_pool, s_k, s_v, s_ssm, s_conv)
```

```python
import functools
import math

import jax
import jax.numpy as jnp
from jax import lax
from jax.experimental import pallas as pl
from jax.experimental.pallas import tpu as pltpu

F32 = jnp.float32
BF16 = jnp.bfloat16

D_MODEL = 2048
DEPTH = 4
PAST_LEN = 1024
CHUNK = 64
HEAD_DIM = 64
BW = D_MODEL // 2
D_FF = 4 * D_MODEL
NORM_EPS = 1e-6
N_HEADS = BW // HEAD_DIM
RW_LORA = 256
RW_IN = 3 * BW + RW_LORA
RW_GN_EPS = HEAD_DIM * 1e-5
POOL_WINDOWS = (2, 4, 8, 16)
POOL_GW = BW // 4
POOL_HIST = 15
SWA_KV = 4
SWA_GROUP = N_HEADS // SWA_KV
WINDOW = 128
SSM_GROUPS = 2
SSM_STATE = 128
SSM_CONV = 4
SSM_BC = SSM_GROUPS * SSM_STATE
SSM_CONV_DIM = BW + 2 * SSM_BC

C_RKV = 0
C_Z = 3072
C_POOL = 4096
C_Q = 5120
C_X = 6144
C_LORA = 7168
C_KS = 7424
C_VS = 7680
C_B = 7936
C_C = 8192
C_DT = 8448
U_COLS = 8704

_N_POOL = RW_IN
_N_SWA = _N_POOL + BW
_N_SSM = _N_SWA + (N_HEADS + 2 * SWA_KV) * HEAD_DIM
IN_COLS = _N_SSM + BW + SSM_CONV_DIM + N_HEADS

VMEM_LIMIT = 56 * 1024 * 1024


def _cparams(sem):
    return pltpu.CompilerParams(dimension_semantics=sem, vmem_limit_bytes=VMEM_LIMIT)


def _bf(x):
    return x.astype(BF16)


def _mm(a, b):
    return jnp.dot(_bf(a), _bf(b), preferred_element_type=F32)


def _mm_nt(a, b):
    return lax.dot_general(_bf(a), _bf(b), (((1,), (1,)), ((), ())), preferred_element_type=F32)


def _mm_tn(a, b):
    return lax.dot_general(_bf(a), _bf(b), (((0,), (0,)), ((), ())), preferred_element_type=F32)


def _split3(x):
    hi = _bf(x)
    r1 = x - hi.astype(F32)
    mid = _bf(r1)
    lo = _bf(r1 - mid.astype(F32))
    return hi, mid, lo


def _mm_sel(c, x):
    hi, mid, lo = _split3(x)
    d = lambda p: jnp.dot(c, p, preferred_element_type=F32)
    return d(hi) + d(mid) + d(lo)


def _mm_sel_nt(c, x):
    hi, mid, lo = _split3(x)
    d = lambda p: lax.dot_general(c, p, (((1,), (1,)), ((), ())), preferred_element_type=F32)
    return d(hi) + d(mid) + d(lo)


def _softplus(x):
    return jnp.maximum(x, 0.0) + jnp.log1p(jnp.exp(-jnp.abs(x)))


def _silu(x):
    return x * jax.nn.sigmoid(x)


def _tri(n, strict=False):
    r = lax.broadcasted_iota(jnp.int32, (n, n), 0)
    c = lax.broadcasted_iota(jnp.int32, (n, n), 1)
    return (r > c) if strict else (r >= c)


def _adaln_kernel(c_ref, w_ref, b_ref, o_ref):
    s = _silu(c_ref[...])
    o_ref[0] = _mm(s, w_ref[0]) + b_ref[0]


def _adaln(c_all, ada_w, ada_b):
    nb = c_all.shape[0]
    tn = 1024
    n_out = ada_w.shape[2]
    return pl.pallas_call(
        _adaln_kernel,
        grid=(DEPTH, n_out // tn),
        in_specs=[
            pl.BlockSpec((nb, D_MODEL), lambda l, n: (0, 0)),
            pl.BlockSpec((1, D_MODEL, tn), lambda l, n: (l, 0, n)),
            pl.BlockSpec((1, 1, tn), lambda l, n: (l, 0, n)),
        ],
        out_specs=pl.BlockSpec((1, nb, tn), lambda l, n: (l, 0, n)),
        out_shape=jax.ShapeDtypeStruct((DEPTH, nb, n_out), F32),
        compiler_params=_cparams(("parallel", "parallel")),
        name="adaln",
    )(c_all, ada_w, ada_b.reshape(DEPTH, 1, n_out))


def _norm_mod(x, g, sc, sh):
    y = x * lax.rsqrt(jnp.mean(x * x, axis=-1, keepdims=True) + NORM_EPS)
    return (y * g) * (1.0 + sc) + sh


def _inproj_kernel(x_ref, g_ref, sc_ref, sh_ref, w_ref, u_ref, h_ref):
    bb, tt, _ = x_ref.shape

    @pl.when(pl.program_id(2) == 0)
    def _():
        h = _norm_mod(x_ref[...], g_ref[...], sc_ref[:, 0], sh_ref[:, 0])
        h_ref[...] = _bf(h)

    h = h_ref[...].reshape(bb * tt, D_MODEL)
    u_ref[...] = jnp.dot(h, w_ref[...], preferred_element_type=F32).reshape(u_ref.shape)


def _inproj(x, g, mod, w, bb, tt):
    b, t, _ = x.shape
    tn = 512
    return pl.pallas_call(
        _inproj_kernel,
        grid=(b // bb, t // tt, U_COLS // tn),
        in_specs=[
            pl.BlockSpec((bb, tt, D_MODEL), lambda i, j, n: (i, j, 0)),
            pl.BlockSpec((1, 1, D_MODEL), lambda i, j, n: (0, 0, 0)),
            pl.BlockSpec((bb, 1, 1, D_MODEL), lambda i, j, n: (i, 1, 0, 0)),
            pl.BlockSpec((bb, 1, 1, D_MODEL), lambda i, j, n: (i, 0, 0, 0)),
            pl.BlockSpec((D_MODEL, tn), lambda i, j, n: (0, n)),
        ],
        out_specs=[
            pl.BlockSpec((bb, tt, tn), lambda i, j, n: (i, j, n)),
            pl.BlockSpec((bb, tt, D_MODEL), lambda i, j, n: (i, j, 0)),
        ],
        out_shape=[
            jax.ShapeDtypeStruct((b, t, U_COLS), F32),
            jax.ShapeDtypeStruct((b, t, D_MODEL), BF16),
        ],
        compiler_params=_cparams(("parallel", "parallel", "arbitrary")),
        name="inproj",
    )(x, g.reshape(1, 1, D_MODEL), mod, mod, w)


def _merge_kernel(h_ref, ya_ref, yb_ref, yc_ref, yd_ref, wg_ref, wb_ref, o_ref):
    bb, tt, _ = h_ref.shape
    m = bb * tt
    h = h_ref[...].reshape(m, D_MODEL)
    acc = None
    for i, y_ref in enumerate((ya_ref, yb_ref, yc_ref, yd_ref)):
        gate = jax.nn.sigmoid(jnp.dot(h, wg_ref[i], preferred_element_type=F32))
        br = jnp.dot(y_ref[...].reshape(m, BW), wb_ref[i], preferred_element_type=F32)
        acc = gate * br if acc is None else acc + gate * br
    o_ref[...] = _bf(acc).reshape(o_ref.shape)


def _merge(h, ys, wg, wb, bb, tt):
    b, t, _ = h.shape
    tn = 256
    yspec = pl.BlockSpec((bb, tt, BW), lambda i, j, n: (i, j, 0))
    return pl.pallas_call(
        _merge_kernel,
        grid=(b // bb, t // tt, D_MODEL // tn),
        in_specs=[
            pl.BlockSpec((bb, tt, D_MODEL), lambda i, j, n: (i, j, 0)),
            yspec, yspec, yspec, yspec,
            pl.BlockSpec((4, D_MODEL, tn), lambda i, j, n: (0, 0, n)),
            pl.BlockSpec((4, BW, tn), lambda i, j, n: (0, 0, n)),
        ],
        out_specs=pl.BlockSpec((bb, tt, tn), lambda i, j, n: (i, j, n)),
        out_shape=jax.ShapeDtypeStruct((b, t, D_MODEL), BF16),
        compiler_params=_cparams(("parallel", "parallel", "arbitrary")),
        name="merge",
    )(h, *ys, wg, wb)


def _outproj_kernel(m_ref, w_ref, x_ref, g_ref, o_ref):
    bb, tt, _ = m_ref.shape
    y = jnp.dot(m_ref[...].reshape(bb * tt, D_MODEL), w_ref[...], preferred_element_type=F32)
    o_ref[...] = x_ref[...] + g_ref[:, 0] * y.reshape(o_ref.shape)


def _outproj(merged, w, x, mod, bb, tt):
    b, t, _ = x.shape
    tn = 1024
    return pl.pallas_call(
        _outproj_kernel,
        grid=(b // bb, t // tt, D_MODEL // tn),
        in_specs=[
            pl.BlockSpec((bb, tt, D_MODEL), lambda i, j, n: (i, j, 0)),
            pl.BlockSpec((D_MODEL, tn), lambda i, j, n: (0, n)),
            pl.BlockSpec((bb, tt, tn), lambda i, j, n: (i, j, n)),
            pl.BlockSpec((bb, 1, 1, tn), lambda i, j, n: (i, 2, 0, n)),
        ],
        out_specs=pl.BlockSpec((bb, tt, tn), lambda i, j, n: (i, j, n)),
        out_shape=jax.ShapeDtypeStruct((b, t, D_MODEL), F32),
        compiler_params=_cparams(("parallel", "parallel", "arbitrary")),
        name="outproj",
    )(merged, w, x, mod)


def _ffn_kernel(x_ref, g_ref, sc_ref, sh_ref, gate_ref, wu_ref, wd_ref, o_ref, h_sc):
    bb, tt, _ = x_ref.shape
    m = bb * tt
    f = pl.program_id(2)

    @pl.when(f == 0)
    def _():
        h = _norm_mod(x_ref[...], g_ref[...], sc_ref[:, 0], sh_ref[:, 0])
        h_sc[...] = _bf(h).reshape(m, D_MODEL)

    a = jnp.dot(h_sc[...], wu_ref[...], preferred_element_type=F32)
    a = jnp.square(jnp.maximum(a, 0.0))
    part = jnp.dot(_bf(a), wd_ref[...], preferred_element_type=F32).reshape(o_ref.shape)

    @pl.when(f == 0)
    def _():
        o_ref[...] = part

    @pl.when(f > 0)
    def _():
        o_ref[...] += part

    @pl.when(f == pl.num_programs(2) - 1)
    def _():
        o_ref[...] = x_ref[...] + gate_ref[:, 0] * o_ref[...]


def _ffn(x, g, mod, wu, wd, bb, tt):
    b, t, _ = x.shape
    tf = 1024
    if bb * tt > 512:
        tt = 512 // bb
    return pl.pallas_call(
        _ffn_kernel,
        grid=(b // bb, t // tt, D_FF // tf),
        in_specs=[
            pl.BlockSpec((bb, tt, D_MODEL), lambda i, j, f: (i, j, 0)),
            pl.BlockSpec((1, 1, D_MODEL), lambda i, j, f: (0, 0, 0)),
            pl.BlockSpec((bb, 1, 1, D_MODEL), lambda i, j, f: (i, 4, 0, 0)),
            pl.BlockSpec((bb, 1, 1, D_MODEL), lambda i, j, f: (i, 3, 0, 0)),
            pl.BlockSpec((bb, 1, 1, D_MODEL), lambda i, j, f: (i, 5, 0, 0)),
            pl.BlockSpec((D_MODEL, tf), lambda i, j, f: (0, f)),
            pl.BlockSpec((tf, D_MODEL), lambda i, j, f: (f, 0)),
        ],
        out_specs=pl.BlockSpec((bb, tt, D_MODEL), lambda i, j, f: (i, j, 0)),
        out_shape=jax.ShapeDtypeStruct((b, t, D_MODEL), F32),
        scratch_shapes=[pltpu.VMEM((bb * tt, D_MODEL), BF16)],
        compiler_params=_cparams(("parallel", "parallel", "arbitrary")),
        name="ffn",
    )(x, g.reshape(1, 1, D_MODEL), mod, mod, mod, wu, wd)


def _final_norm_kernel(x_ref, g_ref, o_ref):
    x = x_ref[...]
    o_ref[...] = (x * lax.rsqrt(jnp.mean(x * x, axis=-1, keepdims=True) + NORM_EPS)) * g_ref[...]


def _final_norm(x, g, bb, tt):
    b, t, _ = x.shape
    spec = pl.BlockSpec((bb, tt, D_MODEL), lambda i, j: (i, j, 0))
    return pl.pallas_call(
        _final_norm_kernel,
        grid=(b // bb, t // tt),
        in_specs=[spec, pl.BlockSpec((1, 1, D_MODEL), lambda i, j: (0, 0, 0))],
        out_specs=spec,
        out_shape=jax.ShapeDtypeStruct((b, t, D_MODEL), F32),
        compiler_params=_cparams(("parallel", "parallel")),
        name="final_norm",
    )(x, g.reshape(1, 1, D_MODEL))


NPAIR = N_HEADS // 2


def _rwkv_kernel(rkv_ref, lora_ref, sh_rkv_ref, sh_lora_ref, s0_ref, mu_rkv_ref, mu_lora_ref,
                 w0_ref, w2_ref, a0_ref, a2_ref, g2_ref, kkp_ref, ka_ref, rk_ref, lnw_ref, lnb_ref,
                 y_ref, sfin_ref,
                 prev_rkv, prev_lora, s_sc, r_sc, lw_sc, kh_sc, v_sc, kk_sc, a_sc, g_sc, yo_sc):
    tt = rkv_ref.shape[1]
    nc = tt // CHUNK
    i = pl.program_id(1)

    @pl.when(i == 0)
    def _():
        prev_rkv[...] = sh_rkv_ref[0]
        prev_lora[...] = sh_lora_ref[0]
        s_sc[...] = s0_ref[0]

    rid = lax.broadcasted_iota(jnp.int32, (tt, 1), 0)

    def tshift(x, prev_row, mu):
        prev = jnp.where(rid == 0, prev_row, pltpu.roll(x, 1, 0))
        return x + (prev - x) * mu

    u = rkv_ref[0]
    ul = lora_ref[0]
    xs = tshift(u, prev_rkv[...], mu_rkv_ref[...])
    xl = tshift(ul, prev_lora[...], mu_lora_ref[...])
    prev_rkv[...] = u[tt - 1:tt, :]
    prev_lora[...] = ul[tt - 1:tt, :]

    r = xs[:, 0:BW]
    k = xs[:, BW:2 * BW]
    v = xs[:, 2 * BW:3 * BW]
    wd = xl[:, 0:64]
    ad = xl[:, 64:128]
    gd = xl[:, 128:256]
    w_log = -_softplus(-(w0_ref[...] + _mm(jnp.tanh(wd), w2_ref[...]))) - 0.5
    lw = -jnp.exp(w_log)
    a = jax.nn.sigmoid(a0_ref[...] + _mm(ad, a2_ref[...]))
    g = _mm(jax.nn.sigmoid(gd), g2_ref[...])
    kkr = k * kkp_ref[...]
    kh = k * (1.0 + (a - 1.0) * ka_ref[...])
    for hp in range(NPAIR):
        sl = slice(hp * 128, (hp + 1) * 128)
        r_sc[hp] = r[:, sl]
        lw_sc[hp] = lw[:, sl]
        kh_sc[hp] = kh[:, sl]
        v_sc[hp] = v[:, sl]
        kk_sc[hp] = kkr[:, sl]
        a_sc[hp] = a[:, sl]
        g_sc[hp] = g[:, sl]

    tri_incl = _tri(CHUNK)
    tri_strict = _tri(CHUNK, strict=True)
    tri_incl_bf = tri_incl.astype(BF16)
    eye = (lax.broadcasted_iota(jnp.int32, (CHUNK, CHUNK), 0)
           == lax.broadcasted_iota(jnp.int32, (CHUNK, CHUNK), 1)).astype(F32)
    tri2 = (lax.broadcasted_iota(jnp.int32, (CHUNK, 2 * CHUNK), 0)
            >= (lax.broadcasted_iota(jnp.int32, (CHUNK, 2 * CHUNK), 1) & (CHUNK - 1)))

    def pair_body(n, carry):
        hp = n // nc
        c = n % nc
        rows = pl.ds(pl.multiple_of(c * CHUNK, CHUNK), CHUNK)
        lwc = lw_sc[hp, rows, :]
        rc = r_sc[hp, rows, :]
        khc = kh_sc[hp, rows, :]
        vc = v_sc[hp, rows, :]
        kkc = kk_sc[hp, rows, :]
        ac = a_sc[hp, rows, :]
        gc = g_sc[hp, rows, :]
        rkp = rk_ref[hp]
        lnw = lnw_ref[hp]
        lnb = lnb_ref[hp]
        cl = _mm_sel(tri_incl_bf, lwc)
        cl_last = cl[CHUNK - 1:CHUNK, :]
        e_in = jnp.exp(cl)
        e_ex = jnp.exp(cl - lwc)
        e_inv = jnp.exp(-cl)
        e_end = jnp.exp(cl_last - cl)
        g_end = jnp.exp(cl_last)
        outs = []
        for h2 in range(2):
            sl = slice(h2 * HEAD_DIM, (h2 + 1) * HEAD_DIM)
            kk = kkc[:, sl]
            kk = kk / jnp.maximum(jnp.sqrt(jnp.sum(kk * kk, axis=-1, keepdims=True)), 1e-12)
            q = kk * ac[:, sl]
            rt = rc[:, sl] * e_in[:, sl]
            pt = -kk * e_ex[:, sl]
            qt = q * e_inv[:, sl]
            kt = khc[:, sl] * e_inv[:, sl]
            qe = q * e_end[:, sl]
            ke = khc[:, sl] * e_end[:, sl]
            vh = vc[:, sl]
            s0 = s_sc[2 * hp + h2]
            lhs = jnp.concatenate([pt, rt], axis=0)
            gm = _mm_nt(lhs, jnp.concatenate([qt, kt], axis=0))
            l_pq = jnp.where(tri_strict, gm[0:CHUNK, 0:CHUNK], 0.0)
            l_pk = jnp.where(tri_strict, gm[0:CHUNK, CHUNK:2 * CHUNK], 0.0)
            a_rqk = jnp.where(tri2, gm[CHUNK:2 * CHUNK, :], 0.0)
            ps_rs = _mm_nt(lhs, s0)
            rhs = ps_rs[0:CHUNK] + _mm(l_pk, vh)
            tinv = eye + l_pq
            xp = l_pq
            for _ in range(5):
                xp = _mm(xp, xp)
                tinv = tinv + _mm(tinv, xp)
            uu = _mm(tinv, rhs)
            uv = jnp.concatenate([uu, vh], axis=0)
            yh = ps_rs[CHUNK:2 * CHUNK] + _mm(a_rqk, uv)
            s_new = s0 * g_end[:, sl] + _mm_tn(uv, jnp.concatenate([qe, ke], axis=0))
            s_sc[2 * hp + h2] = s_new
            mean = jnp.mean(yh, axis=-1, keepdims=True)
            d = yh - mean
            var = jnp.mean(d * d, axis=-1, keepdims=True)
            yn = d * lax.rsqrt(var + RW_GN_EPS) * lnw[:, sl] + lnb[:, sl]
            bonus = jnp.sum(rc[:, sl] * khc[:, sl] * rkp[:, sl], axis=-1, keepdims=True) * vh
            outs.append((yn + bonus) * gc[:, sl])
        yo_sc[hp, rows, :] = jnp.concatenate(outs, axis=1)
        return carry

    lax.fori_loop(0, NPAIR * nc, pair_body, 0)

    for hp in range(NPAIR):
        y_ref[0, :, hp * 128:(hp + 1) * 128] = _bf(yo_sc[hp])

    @pl.when(i == pl.num_programs(1) - 1)
    def _():
        sfin_ref[0] = s_sc[...]


def _rwkv(u, shift_hist, s0, p, tt):
    b, t, _ = u.shape
    row = lambda a: a.reshape(1, -1)
    pairs = lambda a: a.reshape(NPAIR, 1, 128)
    sh = shift_hist.reshape(b, 1, RW_IN)
    sh_rkv = sh[:, :, :3 * BW]
    sh_lora = sh[:, :, 3 * BW:]
    mu = p["rw_mu"]
    full = lambda shape: pl.BlockSpec(shape, lambda i, j: (0,) * len(shape))
    scr = lambda: pltpu.VMEM((NPAIR, tt, 128), F32)
    return pl.pallas_call(
        _rwkv_kernel,
        grid=(b, t // tt),
        in_specs=[
            pl.BlockSpec((1, tt, 3 * BW), lambda i, j: (i, j, C_RKV // (3 * BW))),
            pl.BlockSpec((1, tt, RW_LORA), lambda i, j: (i, j, C_LORA // RW_LORA)),
            pl.BlockSpec((1, 1, 3 * BW), lambda i, j: (i, 0, 0)),
            pl.BlockSpec((1, 1, RW_LORA), lambda i, j: (i, 0, 0)),
            pl.BlockSpec((1, N_HEADS, HEAD_DIM, HEAD_DIM), lambda i, j: (i, 0, 0, 0)),
            full((1, 3 * BW)), full((1, RW_LORA)),
            full((1, BW)), full((64, BW)), full((1, BW)), full((64, BW)), full((128, BW)),
            full((1, BW)), full((1, BW)),
            full((NPAIR, 1, 128)), full((NPAIR, 1, 128)), full((NPAIR, 1, 128)),
        ],
        out_specs=[
            pl.BlockSpec((1, tt, BW), lambda i, j: (i, j, 0)),
            pl.BlockSpec((1, N_HEADS, HEAD_DIM, HEAD_DIM), lambda i, j: (i, 0, 0, 0)),
        ],
        out_shape=[
            jax.ShapeDtypeStruct((b, t, BW), BF16),
            jax.ShapeDtypeStruct((b, N_HEADS, HEAD_DIM, HEAD_DIM), F32),
        ],
        scratch_shapes=[
            pltpu.VMEM((1, 3 * BW), F32), pltpu.VMEM((1, RW_LORA), F32),
            pltpu.VMEM((N_HEADS, HEAD_DIM, HEAD_DIM), F32),
            scr(), scr(), scr(), scr(), scr(), scr(), scr(), scr(),
        ],
        compiler_params=_cparams(("parallel", "arbitrary")),
        name="rwkv7",
    )(u, u, sh_rkv, sh_lora, s0, row(mu[:3 * BW]), row(mu[3 * BW:]),
      row(p["rw_w0"]), p["rw_w2"], row(p["rw_a0"]), p["rw_a2"], p["rw_g2"],
      row(p["rw_kk"]), row(p["rw_ka"]), pairs(p["rw_rk"]), pairs(p["rw_ln_w"]), pairs(p["rw_ln_b"]))


def _pool_kernel(pos0, u_ref, prev_ref, hist_ref, w_ref, scale_ref, y_ref):
    tt = u_ref.shape[1]
    i = pl.program_id(1)
    x = u_ref[0]
    prev = jnp.where(i == 0, hist_ref[0], prev_ref[0])
    s = jnp.concatenate([prev, x], axis=0)
    pos = pos0 + i * tt + lax.broadcasted_iota(jnp.int32, (tt, 1), 0)
    outs = []
    for gi, win in enumerate(POOL_WINDOWS):
        s = s + pltpu.roll(s, win // 2, 0)
        sl = slice(gi * POOL_GW, (gi + 1) * POOL_GW)
        cnt = jnp.minimum(pos + 1, win).astype(F32)
        pg = s[16:, sl] / cnt - x[:, sl]
        outs.append(_mm(pg, w_ref[gi]))
    y_ref[0] = _bf(jnp.concatenate(outs, axis=1) * scale_ref[...])


def _pool(u, hist16, pos0, w_pool, scale, tt):
    b, t, _ = u.shape
    nprev = tt // 16
    return pl.pallas_call(
        functools.partial(_pool_kernel, pos0),
        grid=(b, t // tt),
        in_specs=[
            pl.BlockSpec((1, tt, BW), lambda i, j: (i, j, C_POOL // BW)),
            pl.BlockSpec((1, 16, BW), lambda i, j: (i, jnp.maximum(j * nprev - 1, 0), C_POOL // BW)),
            pl.BlockSpec((1, 16, BW), lambda i, j: (i, 0, 0)),
            pl.BlockSpec((4, POOL_GW, POOL_GW), lambda i, j: (0, 0, 0)),
            pl.BlockSpec((1, BW), lambda i, j: (0, 0)),
        ],
        out_specs=pl.BlockSpec((1, tt, BW), lambda i, j: (i, j, 0)),
        out_shape=jax.ShapeDtypeStruct((b, t, BW), BF16),
        compiler_params=_cparams(("parallel", "parallel")),
        name="pool",
    )(u, u, hist16, w_pool, scale.reshape(1, BW))


def _swa_kernel(mask_history, q_ref, kp2_ref, kp1_ref, kc_ref, vp2_ref, vp1_ref, vc_ref, sink_ref, y_ref):
    tq = q_ref.shape[1]
    nc = tq // CHUNK
    i = pl.program_id(1)
    q = q_ref[0]
    k_all = jnp.concatenate([kp2_ref[0], kp1_ref[0], kc_ref[0]], axis=0)
    v_all = jnp.concatenate([vp2_ref[0], vp1_ref[0], vc_ref[0]], axis=0)
    nk = 3 * CHUNK
    qi = lax.broadcasted_iota(jnp.int32, (CHUNK, nk), 0)
    si = lax.broadcasted_iota(jnp.int32, (CHUNK, nk), 1)
    dist1 = jnp.abs(qi + 2 * CHUNK - si).astype(F32)
    dist = jnp.concatenate([dist1] * SWA_GROUP, axis=0)
    scol = lax.broadcasted_iota(jnp.int32, (SWA_GROUP * CHUNK, nk), 1)
    for c in range(nc):
        if mask_history:
            gchunk = i * nc + c
            valid = scol >= (2 - gchunk) * CHUNK
        for g in range(SWA_KV):
            ks = k_all[c * CHUNK:c * CHUNK + nk, g * HEAD_DIM:(g + 1) * HEAD_DIM]
            vs = v_all[c * CHUNK:c * CHUNK + nk, g * HEAD_DIM:(g + 1) * HEAD_DIM]
            heads = [g * SWA_GROUP + hh for hh in range(SWA_GROUP)]
            qs = jnp.concatenate(
                [q[c * CHUNK:(c + 1) * CHUNK, h * HEAD_DIM:(h + 1) * HEAD_DIM] for h in heads], axis=0)
            slope = jnp.concatenate(
                [jnp.full((CHUNK, 1), 2.0 ** (-8.0 * (h + 1) / N_HEADS), F32) for h in heads], axis=0)
            sink = jnp.concatenate(
                [jnp.full((CHUNK, 1), sink_ref[h], F32) for h in heads], axis=0)
            s = _mm_nt(qs, ks) * (HEAD_DIM ** -0.5) - slope * dist
            if mask_history:
                s = jnp.where(valid, s, -1e30)
            m = jnp.maximum(jnp.max(s, axis=-1, keepdims=True), sink)
            p = jnp.exp(s - m)
            denom = jnp.sum(p, axis=-1, keepdims=True) + jnp.exp(sink - m)
            o = _mm(p / denom, vs)
            for hh, h in enumerate(heads):
                y_ref[0, c * CHUNK:(c + 1) * CHUNK, h * HEAD_DIM:(h + 1) * HEAD_DIM] = _bf(
                    o[hh * CHUNK:(hh + 1) * CHUNK])


def _swa(u, hist_k, hist_v, sinks, tq):
    b, t, _ = u.shape
    per = tq // CHUNK
    kvw = SWA_KV * HEAD_DIM
    mask_history = hist_k is None
    cur_k = pl.BlockSpec((1, tq, kvw), lambda i, j: (i, j, C_KS // kvw))
    cur_v = pl.BlockSpec((1, tq, kvw), lambda i, j: (i, j, C_VS // kvw))
    if mask_history:
        prev = lambda d, col: pl.BlockSpec(
            (1, CHUNK, kvw), lambda i, j: (i, jnp.maximum(j * per - d, 0), col))
        specs = [prev(2, C_KS // kvw), prev(1, C_KS // kvw), cur_k,
                 prev(2, C_VS // kvw), prev(1, C_VS // kvw), cur_v]
        args = (u, u, u, u, u, u)
    else:
        assert t == tq
        hk = hist_k.reshape(b, WINDOW, kvw)
        hv = hist_v.reshape(b, WINDOW, kvw)
        hist = lambda blk: pl.BlockSpec((1, CHUNK, kvw), lambda i, j: (i, blk, 0))
        specs = [hist(0), hist(1), cur_k, hist(0), hist(1), cur_v]
        args = (hk, hk, u, hv, hv, u)
    return pl.pallas_call(
        functools.partial(_swa_kernel, mask_history),
        grid=(b, t // tq),
        in_specs=[pl.BlockSpec((1, tq, BW), lambda i, j: (i, j, C_Q // BW))] + specs
        + [pl.BlockSpec(memory_space=pltpu.SMEM)],
        out_specs=pl.BlockSpec((1, tq, BW), lambda i, j: (i, j, 0)),
        out_shape=jax.ShapeDtypeStruct((b, t, BW), BF16),
        compiler_params=_cparams(("parallel", "parallel")),
        name="swa",
    )(u, *args, sinks)


def _ssd_kernel(z_ref, x_ref, b_ref, c_ref, dt_ref, hx_ref, hb_ref, hc_ref, s0_ref,
                cwx_ref, cwb_ref, cwc_ref, cbx_ref, cbb_ref, cbc_ref,
                dtb_ref, alog_ref, dsk_ref, nw_ref,
                y_ref, sfin_ref,
                px_sc, pb_sc, pc_sc, s_sc, xa_sc, ba_sc, ca_sc, dt_sc, ad_sc, y_sc):
    tt = x_ref.shape[1]
    nc = tt // CHUNK
    i = pl.program_id(1)

    @pl.when(i == 0)
    def _():
        px_sc[...] = hx_ref[0]
        pb_sc[...] = hb_ref[0]
        pc_sc[...] = hc_ref[0]
        s_sc[...] = s0_ref[0]

    rid8 = lax.broadcasted_iota(jnp.int32, (8, 1), 0)

    def conv_silu(x, prev8, w_ref, b_ref_):
        acc = None
        for wi in range(SSM_CONV):
            sh = SSM_CONV - 1 - wi
            if sh == 0:
                xs = x
            else:
                rolled = pltpu.roll(x, sh, 0)
                top = jnp.where(rid8 < sh, pltpu.roll(prev8, sh, 0), rolled[0:8])
                xs = jnp.concatenate([top, rolled[8:]], axis=0)
            term = xs * w_ref[wi:wi + 1, :]
            acc = (b_ref_[...] + term) if acc is None else acc + term
        return _silu(acc)

    xr = x_ref[0]
    br = b_ref[0]
    cr = c_ref[0]
    xa_sc[...] = conv_silu(xr, px_sc[...], cwx_ref, cbx_ref)
    ba_sc[...] = conv_silu(br, pb_sc[...], cwb_ref, cbb_ref)
    ca_sc[...] = conv_silu(cr, pc_sc[...], cwc_ref, cbc_ref)
    px_sc[...] = xr[tt - 8:tt, :]
    pb_sc[...] = br[tt - 8:tt, :]
    pc_sc[...] = cr[tt - 8:tt, :]
    dt = _softplus(dt_ref[0] + dtb_ref[...])
    dt_sc[...] = dt
    ad_sc[...] = -jnp.exp(alog_ref[...]) * dt

    tri_incl = _tri(CHUNK)
    tri_incl_bf = tri_incl.astype(BF16)
    sel16 = (lax.broadcasted_iota(jnp.int32, (N_HEADS, SSM_BC), 0)
             == lax.broadcasted_iota(jnp.int32, (N_HEADS, SSM_BC), 1)).astype(BF16)
    dsk = dsk_ref[...]
    hpg = N_HEADS // SSM_GROUPS

    def chunk_body(c, carry):
        rows = pl.ds(pl.multiple_of(c * CHUNK, CHUNK), CHUNK)
        xc = xa_sc[rows, :]
        bc = ba_sc[rows, :]
        cc = ca_sc[rows, :]
        dtc = dt_sc[rows, :]
        acs = _mm_sel(tri_incl_bf, ad_sc[rows, :])
        acs_t = _mm_sel_nt(sel16, acs)
        a_last = acs[CHUNK - 1:CHUNK, :]
        e_out = jnp.exp(acs)
        e_end = jnp.exp(a_last - acs)
        g_end = jnp.exp(a_last)
        scores = []
        for gi in range(SSM_GROUPS):
            gs = slice(gi * SSM_STATE, (gi + 1) * SSM_STATE)
            scores.append(_mm_nt(cc[:, gs], bc[:, gs]))
        outs = []
        for h in range(N_HEADS):
            gi = h // hpg
            gs = slice(gi * SSM_STATE, (gi + 1) * SSM_STATE)
            hs = slice(h * HEAD_DIM, (h + 1) * HEAD_DIM)
            xh = xc[:, hs]
            xdt = xh * dtc[:, h:h + 1]
            seg = acs[:, h:h + 1] - acs_t[h:h + 1, :]
            lmat = jnp.exp(jnp.where(tri_incl, seg, -1e30))
            sh_ = s_sc[h]
            y_diag = _mm(scores[gi] * lmat, xdt)
            y_off = _mm_nt(cc[:, gs], sh_) * e_out[:, h:h + 1]
            s_new = sh_ * g_end[:, h:h + 1] + _mm_tn(xdt * e_end[:, h:h + 1], bc[:, gs])
            s_sc[h] = s_new
            outs.append(y_diag + y_off + xh * dsk[:, h:h + 1])
        y_sc[rows, :] = jnp.concatenate(outs, axis=1)
        return carry

    lax.fori_loop(0, nc, chunk_body, 0)

    y = y_sc[...] * _silu(z_ref[0])
    gw = BW // SSM_GROUPS
    parts = []
    for gi in range(SSM_GROUPS):
        yg = y[:, gi * gw:(gi + 1) * gw]
        parts.append(yg * lax.rsqrt(jnp.mean(yg * yg, axis=-1, keepdims=True) + NORM_EPS))
    y_ref[0] = _bf(jnp.concatenate(parts, axis=1) * nw_ref[...])

    @pl.when(i == pl.num_programs(1) - 1)
    def _():
        sfin_ref[0] = s_sc[...]


def _pad_rows8(a):
    return jnp.pad(a, ((0, 0), (8 - a.shape[1], 0), (0, 0)))


def _pad_lanes(a, n):
    return jnp.pad(a.reshape(1, -1), ((0, 0), (0, n - a.shape[-1])))


def _ssd(u, conv_hist, s0, p, tt):
    b, t, _ = u.shape
    h8 = _pad_rows8(conv_hist)
    hx, hb, hc = h8[:, :, :BW], h8[:, :, BW:BW + SSM_BC], h8[:, :, BW + SSM_BC:]
    cw, cb = p["ssm_conv_w"], p["ssm_conv_b"].reshape(1, -1)
    full = lambda shape: pl.BlockSpec(shape, lambda i, j: (0,) * len(shape))
    col = lambda w, c0: pl.BlockSpec((1, tt, w), lambda i, j: (i, j, c0 // w))
    hist = lambda w: pl.BlockSpec((1, 8, w), lambda i, j: (i, 0, 0))
    return pl.pallas_call(
        _ssd_kernel,
        grid=(b, t // tt),
        in_specs=[
            col(BW, C_Z), col(BW, C_X), col(SSM_BC, C_B), col(SSM_BC, C_C), col(SSM_BC, C_DT),
            hist(BW), hist(SSM_BC), hist(SSM_BC),
            pl.BlockSpec((1, N_HEADS, HEAD_DIM, SSM_STATE), lambda i, j: (i, 0, 0, 0)),
            full((SSM_CONV, BW)), full((SSM_CONV, SSM_BC)), full((SSM_CONV, SSM_BC)),
            full((1, BW)), full((1, SSM_BC)), full((1, SSM_BC)),
            full((1, SSM_BC)), full((1, SSM_BC)), full((1, SSM_BC)), full((1, BW)),
        ],
        out_specs=[
            pl.BlockSpec((1, tt, BW), lambda i, j: (i, j, 0)),
            pl.BlockSpec((1, N_HEADS, HEAD_DIM, SSM_STATE), lambda i, j: (i, 0, 0, 0)),
        ],
        out_shape=[
            jax.ShapeDtypeStruct((b, t, BW), BF16),
            jax.ShapeDtypeStruct((b, N_HEADS, HEAD_DIM, SSM_STATE), F32),
        ],
        scratch_shapes=[
            pltpu.VMEM((8, BW), F32), pltpu.VMEM((8, SSM_BC), F32), pltpu.VMEM((8, SSM_BC), F32),
            pltpu.VMEM((N_HEADS, HEAD_DIM, SSM_STATE), F32),
            pltpu.VMEM((tt, BW), F32), pltpu.VMEM((tt, SSM_BC), F32), pltpu.VMEM((tt, SSM_BC), F32),
            pltpu.VMEM((tt, SSM_BC), F32), pltpu.VMEM((tt, SSM_BC), F32), pltpu.VMEM((tt, BW), F32),
        ],
        compiler_params=_cparams(("parallel", "arbitrary")),
        name="ssd",
    )(u, u, u, u, u, hx, hb, hc, s0,
      cw[:, :BW], cw[:, BW:BW + SSM_BC], cw[:, BW + SSM_BC:],
      cb[:, :BW], cb[:, BW:BW + SSM_BC], cb[:, BW + SSM_BC:],
      _pad_lanes(p["ssm_dt_bias"], SSM_BC), _pad_lanes(p["ssm_a_log"], SSM_BC),
      _pad_lanes(p["ssm_d"], SSM_BC), p["ssm_norm"].reshape(1, BW))


def _prep_w_in(w_in):
    c = lambda a, n: w_in[:, :, a:a + n]
    ssm = _N_SSM
    parts = [
        c(0, 3 * BW),
        c(ssm, BW),
        c(_N_POOL, BW),
        c(_N_SWA, BW),
        c(ssm + BW, BW),
        c(3 * BW, RW_LORA),
        c(_N_SWA + BW, 2 * SWA_KV * HEAD_DIM),
        c(ssm + 2 * BW, 2 * SSM_BC + N_HEADS),
    ]
    w = jnp.concatenate(parts, axis=-1).astype(BF16)
    return jnp.pad(w, ((0, 0), (0, 0), (0, U_COLS - w.shape[-1])))


def _run_group(x, mods, st, p, final_norm, wts, pos0, bb, tt, tt_branch):
    b, t, _ = x.shape
    kvw = SWA_KV * HEAD_DIM
    outs = {k: [] for k in ("rwkv", "shift", "pool", "k", "v", "ssm", "conv")}
    for l in range(DEPTH):
        pl_ = {k: v[l] for k, v in p.items()}
        mod = mods[l]
        u, h = _inproj(x, pl_["norm_mix"], mod, wts["w_in"][l], bb, tt)
        if st is None:
            shift_hist = jnp.zeros((b, 1, RW_IN), F32)
            s_rwkv = jnp.zeros((b, N_HEADS, HEAD_DIM, HEAD_DIM), F32)
            pool_hist = jnp.zeros((b, POOL_HIST, BW), F32)
            hk = hv = None
            s_ssm = jnp.zeros((b, N_HEADS, HEAD_DIM, SSM_STATE), F32)
            conv_hist = jnp.zeros((b, SSM_CONV - 1, SSM_CONV_DIM), F32)
        else:
            shift_hist, s_rwkv, pool_hist = st["shift"][l], st["rwkv"][l], st["pool"][l]
            hk, hv, s_ssm, conv_hist = st["k"][l], st["v"][l], st["ssm"][l], st["conv"][l]
        y_a, n_rwkv = _rwkv(u, shift_hist, s_rwkv, pl_, tt_branch)
        hist16 = jnp.pad(pool_hist, ((0, 0), (1, 0), (0, 0)))
        y_b = _pool(u, hist16, pos0, wts["pool_w"][l], pl_["pool_scale"], tt_branch)
        y_c = _swa(u, hk, hv, pl_["swa_sinks"], min(tt_branch, 256))
        y_d, n_ssm = _ssd(u, conv_hist, s_ssm, pl_, tt_branch)
        merged = _merge(h, (y_a, y_b, y_c, y_d), wts["w_gate"][l], wts["w_branch"][l], bb, tt)
        x = _outproj(merged, wts["w_out"][l], x, mod, bb, tt)
        x = _ffn(x, pl_["norm_ffn"], mod, wts["w_up"][l], wts["w_down"][l], bb, tt)

        outs["rwkv"].append(n_rwkv)
        outs["shift"].append(jnp.concatenate(
            [u[:, t - 1:, C_RKV:C_RKV + 3 * BW], u[:, t - 1:, C_LORA:C_LORA + RW_LORA]], axis=-1))
        outs["pool"].append(u[:, t - POOL_HIST:, C_POOL:C_POOL + BW])
        k_new = u[:, :, C_KS:C_KS + kvw]
        v_new = u[:, :, C_VS:C_VS + kvw]
        if hk is not None:
            k_new = jnp.concatenate([hk.reshape(b, WINDOW, kvw), k_new], axis=1)
            v_new = jnp.concatenate([hv.reshape(b, WINDOW, kvw), v_new], axis=1)
        outs["k"].append(k_new[:, -WINDOW:].reshape(b, WINDOW, SWA_KV, HEAD_DIM))
        outs["v"].append(v_new[:, -WINDOW:].reshape(b, WINDOW, SWA_KV, HEAD_DIM))
        outs["ssm"].append(n_ssm)
        outs["conv"].append(jnp.concatenate(
            [u[:, t - 3:, C_X:C_X + BW], u[:, t - 3:, C_B:C_B + 2 * SSM_BC]], axis=-1))
    y = _final_norm(x, final_norm, bb, tt)
    order = ("rwkv", "shift", "pool", "k", "v", "ssm", "conv")
    return y, tuple(jnp.stack(outs[k]) for k in order)


def kernel(x_prompt, x_sample, state_rwkv, state_rwkv_shift, state_pool, cache_swa_k, cache_swa_v,
           state_ssm, state_ssm_conv, c_prompt, c_sample, ada_w, ada_b, norm_mix, norm_ffn, w_in,
           rw_mu, rw_w0, rw_w2, rw_a0, rw_a2, rw_g2, rw_kk, rw_ka, rw_rk, rw_ln_w, rw_ln_b, pool_w,
           pool_scale, swa_sinks, ssm_conv_w, ssm_conv_b, ssm_dt_bias, ssm_a_log, ssm_d, ssm_norm,
           w_gate, w_branch, w_out, w_up, w_down, final_norm):
    bp, tp, _ = x_prompt.shape
    bs, ts, _ = x_sample.shape
    p = dict(norm_mix=norm_mix, norm_ffn=norm_ffn, rw_mu=rw_mu, rw_w0=rw_w0, rw_w2=rw_w2, rw_a0=rw_a0,
             rw_a2=rw_a2, rw_g2=rw_g2, rw_kk=rw_kk, rw_ka=rw_ka, rw_rk=rw_rk, rw_ln_w=rw_ln_w,
             rw_ln_b=rw_ln_b, pool_scale=pool_scale, swa_sinks=swa_sinks, ssm_conv_w=ssm_conv_w,
             ssm_conv_b=ssm_conv_b, ssm_dt_bias=ssm_dt_bias, ssm_a_log=ssm_a_log, ssm_d=ssm_d,
             ssm_norm=ssm_norm)
    wts = dict(w_in=_prep_w_in(w_in), pool_w=pool_w.astype(BF16), w_gate=w_gate.astype(BF16),
               w_branch=w_branch.astype(BF16), w_out=w_out.astype(BF16), w_up=w_up.astype(BF16),
               w_down=w_down.astype(BF16))

    nb = bp + bs
    nb_pad = -(-nb // 8) * 8
    c_all = jnp.pad(jnp.concatenate([c_prompt, c_sample], axis=0), ((0, nb_pad - nb), (0, 0)))
    mod_all = _adaln(c_all, ada_w, ada_b)
    mods_p = mod_all[:, :bp].reshape(DEPTH, bp, 6, 1, D_MODEL)
    mods_s = mod_all[:, bp:nb].reshape(DEPTH, bs, 6, 1, D_MODEL)

    y_prompt, st_p = _run_group(x_prompt, mods_p, None, p, final_norm, wts, 0,
                                1, min(tp, 1024), min(tp, 512))
    st_s = dict(rwkv=state_rwkv, shift=state_rwkv_shift, pool=state_pool, k=cache_swa_k,
                v=cache_swa_v, ssm=state_ssm, conv=state_ssm_conv)
    y_sample, st_o = _run_group(x_sample, mods_s, st_s, p, final_norm, wts, PAST_LEN, bs, ts, ts)
    return (y_prompt, y_sample) + st_p + st_o
```

```python
import functools
import math

import jax
import jax.numpy as jnp
from jax import lax
from jax.experimental import pallas as pl
from jax.experimental.pallas import tpu as pltpu

F32 = jnp.float32
BF16 = jnp.bfloat16

D_MODEL = 2048
DEPTH = 4
PAST_LEN = 1024
CHUNK = 64
HEAD_DIM = 64
BW = D_MODEL // 2
D_FF = 4 * D_MODEL
NORM_EPS = 1e-6
N_HEADS = BW // HEAD_DIM
RW_LORA = 256
RW_IN = 3 * BW + RW_LORA
RW_GN_EPS = HEAD_DIM * 1e-5
POOL_WINDOWS = (2, 4, 8, 16)
POOL_GW = BW // 4
POOL_HIST = 15
SWA_KV = 4
SWA_GROUP = N_HEADS // SWA_KV
WINDOW = 128
SSM_GROUPS = 2
SSM_STATE = 128
SSM_CONV = 4
SSM_BC = SSM_GROUPS * SSM_STATE
SSM_CONV_DIM = BW + 2 * SSM_BC

C_RKV = 0
C_Z = 3072
C_POOL = 4096
C_Q = 5120
C_X = 6144
C_LORA = 7168
C_KS = 7424
C_VS = 7680
C_B = 7936
C_C = 8192
C_DT = 8448
U_COLS = 8704

_N_POOL = RW_IN
_N_SWA = _N_POOL + BW
_N_SSM = _N_SWA + (N_HEADS + 2 * SWA_KV) * HEAD_DIM
IN_COLS = _N_SSM + BW + SSM_CONV_DIM + N_HEADS

VMEM_LIMIT = 56 * 1024 * 1024


def _cparams(sem):
    return pltpu.CompilerParams(dimension_semantics=sem, vmem_limit_bytes=VMEM_LIMIT)


def _bf(x):
    return x.astype(BF16)


def _mm(a, b):
    return jnp.dot(_bf(a), _bf(b), preferred_element_type=F32)


def _mm_nt(a, b):
    return lax.dot_general(_bf(a), _bf(b), (((1,), (1,)), ((), ())), preferred_element_type=F32)


def _mm_tn(a, b):
    return lax.dot_general(_bf(a), _bf(b), (((0,), (0,)), ((), ())), preferred_element_type=F32)


def _split3(x):
    hi = _bf(x)
    r1 = x - hi.astype(F32)
    mid = _bf(r1)
    lo = _bf(r1 - mid.astype(F32))
    return hi, mid, lo


def _split(x, terms):
    parts = []
    for _ in range(terms - 1):
        hi = _bf(x)
        parts.append(hi)
        x = x - hi.astype(F32)
    parts.append(_bf(x))
    return parts


def _mm_sel(c, x, terms=3):
    acc = None
    for part in _split(x, terms):
        d = jnp.dot(c, part, preferred_element_type=F32)
        acc = d if acc is None else acc + d
    return acc


def _mm_sel_nt(c, x):
    hi, mid, lo = _split3(x)
    d = lambda p: lax.dot_general(c, p, (((1,), (1,)), ((), ())), preferred_element_type=F32)
    return d(hi) + d(mid) + d(lo)


def _softplus(x):
    return jnp.maximum(x, 0.0) + jnp.log1p(jnp.exp(-jnp.abs(x)))


def _silu(x):
    return x * jax.nn.sigmoid(x)


def _tri(n, strict=False):
    r = lax.broadcasted_iota(jnp.int32, (n, n), 0)
    c = lax.broadcasted_iota(jnp.int32, (n, n), 1)
    return (r > c) if strict else (r >= c)


def _adaln_kernel(c_ref, w_ref, b_ref, o_ref):
    s = _silu(c_ref[...])
    o_ref[0] = _mm(s, w_ref[0]) + b_ref[0]


def _adaln(c_all, ada_w, ada_b):
    nb = c_all.shape[0]
    tn = 1024
    n_out = ada_w.shape[2]
    return pl.pallas_call(
        _adaln_kernel,
        grid=(DEPTH, n_out // tn),
        in_specs=[
            pl.BlockSpec((nb, D_MODEL), lambda l, n: (0, 0)),
            pl.BlockSpec((1, D_MODEL, tn), lambda l, n: (l, 0, n)),
            pl.BlockSpec((1, 1, tn), lambda l, n: (l, 0, n)),
        ],
        out_specs=pl.BlockSpec((1, nb, tn), lambda l, n: (l, 0, n)),
        out_shape=jax.ShapeDtypeStruct((DEPTH, nb, n_out), F32),
        compiler_params=_cparams(("parallel", "parallel")),
        name="adaln",
    )(c_all, ada_w, ada_b.reshape(DEPTH, 1, n_out))


def _norm_mod(x, g, sc, sh):
    y = x * lax.rsqrt(jnp.mean(x * x, axis=-1, keepdims=True) + NORM_EPS)
    return (y * g) * (1.0 + sc) + sh


def _inproj_kernel(x_ref, g_ref, sc_ref, sh_ref, w_ref, u_ref, h_ref):
    bb, tt, _ = x_ref.shape

    @pl.when(pl.program_id(2) == 0)
    def _():
        h = _norm_mod(x_ref[...], g_ref[...], sc_ref[:, 0], sh_ref[:, 0])
        h_ref[...] = _bf(h)

    h = h_ref[...].reshape(bb * tt, D_MODEL)
    u_ref[...] = jnp.dot(h, w_ref[...], preferred_element_type=F32).reshape(u_ref.shape)


def _inproj(x, g, mod, w, bb, tt):
    b, t, _ = x.shape
    tn = 512
    return pl.pallas_call(
        _inproj_kernel,
        grid=(b // bb, t // tt, U_COLS // tn),
        in_specs=[
            pl.BlockSpec((bb, tt, D_MODEL), lambda i, j, n: (i, j, 0)),
            pl.BlockSpec((1, 1, D_MODEL), lambda i, j, n: (0, 0, 0)),
            pl.BlockSpec((bb, 1, 1, D_MODEL), lambda i, j, n: (i, 1, 0, 0)),
            pl.BlockSpec((bb, 1, 1, D_MODEL), lambda i, j, n: (i, 0, 0, 0)),
            pl.BlockSpec((D_MODEL, tn), lambda i, j, n: (0, n)),
        ],
        out_specs=[
            pl.BlockSpec((bb, tt, tn), lambda i, j, n: (i, j, n)),
            pl.BlockSpec((bb, tt, D_MODEL), lambda i, j, n: (i, j, 0)),
        ],
        out_shape=[
            jax.ShapeDtypeStruct((b, t, U_COLS), F32),
            jax.ShapeDtypeStruct((b, t, D_MODEL), BF16),
        ],
        compiler_params=_cparams(("parallel", "parallel", "arbitrary")),
        name="inproj",
    )(x, g.reshape(1, 1, D_MODEL), mod, mod, w)


def _merge_kernel(h_ref, ya_ref, yb_ref, yc_ref, yd_ref, wg_ref, wb_ref, o_ref):
    bb, tt, _ = h_ref.shape
    m = bb * tt
    h = h_ref[...].reshape(m, D_MODEL)
    acc = None
    for i, y_ref in enumerate((ya_ref, yb_ref, yc_ref, yd_ref)):
        gate = jax.nn.sigmoid(jnp.dot(h, wg_ref[i], preferred_element_type=F32))
        br = jnp.dot(y_ref[...].reshape(m, BW), wb_ref[i], preferred_element_type=F32)
        acc = gate * br if acc is None else acc + gate * br
    o_ref[...] = _bf(acc).reshape(o_ref.shape)


def _merge(h, ys, wg, wb, bb, tt):
    b, t, _ = h.shape
    tn = 256
    yspec = pl.BlockSpec((bb, tt, BW), lambda i, j, n: (i, j, 0))
    return pl.pallas_call(
        _merge_kernel,
        grid=(b // bb, t // tt, D_MODEL // tn),
        in_specs=[
            pl.BlockSpec((bb, tt, D_MODEL), lambda i, j, n: (i, j, 0)),
            yspec, yspec, yspec, yspec,
            pl.BlockSpec((4, D_MODEL, tn), lambda i, j, n: (0, 0, n)),
            pl.BlockSpec((4, BW, tn), lambda i, j, n: (0, 0, n)),
        ],
        out_specs=pl.BlockSpec((bb, tt, tn), lambda i, j, n: (i, j, n)),
        out_shape=jax.ShapeDtypeStruct((b, t, D_MODEL), BF16),
        compiler_params=_cparams(("parallel", "parallel", "arbitrary")),
        name="merge",
    )(h, *ys, wg, wb)


def _outproj_kernel(m_ref, w_ref, x_ref, g_ref, o_ref):
    bb, tt, _ = m_ref.shape
    y = jnp.dot(m_ref[...].reshape(bb * tt, D_MODEL), w_ref[...], preferred_element_type=F32)
    o_ref[...] = x_ref[...] + g_ref[:, 0] * y.reshape(o_ref.shape)


def _outproj(merged, w, x, mod, bb, tt):
    b, t, _ = x.shape
    tn = 1024
    return pl.pallas_call(
        _outproj_kernel,
        grid=(b // bb, t // tt, D_MODEL // tn),
        in_specs=[
            pl.BlockSpec((bb, tt, D_MODEL), lambda i, j, n: (i, j, 0)),
            pl.BlockSpec((D_MODEL, tn), lambda i, j, n: (0, n)),
            pl.BlockSpec((bb, tt, tn), lambda i, j, n: (i, j, n)),
            pl.BlockSpec((bb, 1, 1, tn), lambda i, j, n: (i, 2, 0, n)),
        ],
        out_specs=pl.BlockSpec((bb, tt, tn), lambda i, j, n: (i, j, n)),
        out_shape=jax.ShapeDtypeStruct((b, t, D_MODEL), F32),
        compiler_params=_cparams(("parallel", "parallel", "arbitrary")),
        name="outproj",
    )(merged, w, x, mod)


def _ffn_kernel(x_ref, g_ref, sc_ref, sh_ref, gate_ref, wu_ref, wd_ref, o_ref, h_sc):
    bb, tt, _ = x_ref.shape
    m = bb * tt
    f = pl.program_id(2)

    @pl.when(f == 0)
    def _():
        h = _norm_mod(x_ref[...], g_ref[...], sc_ref[:, 0], sh_ref[:, 0])
        h_sc[...] = _bf(h).reshape(m, D_MODEL)

    a = jnp.dot(h_sc[...], wu_ref[...], preferred_element_type=F32)
    a = jnp.square(jnp.maximum(a, 0.0))
    part = jnp.dot(_bf(a), wd_ref[...], preferred_element_type=F32).reshape(o_ref.shape)

    @pl.when(f == 0)
    def _():
        o_ref[...] = part

    @pl.when(f > 0)
    def _():
        o_ref[...] += part

    @pl.when(f == pl.num_programs(2) - 1)
    def _():
        o_ref[...] = x_ref[...] + gate_ref[:, 0] * o_ref[...]


def _ffn(x, g, mod, wu, wd, bb, tt):
    b, t, _ = x.shape
    tf = 1024
    if bb * tt > 512:
        tt = 512 // bb
    return pl.pallas_call(
        _ffn_kernel,
        grid=(b // bb, t // tt, D_FF // tf),
        in_specs=[
            pl.BlockSpec((bb, tt, D_MODEL), lambda i, j, f: (i, j, 0)),
            pl.BlockSpec((1, 1, D_MODEL), lambda i, j, f: (0, 0, 0)),
            pl.BlockSpec((bb, 1, 1, D_MODEL), lambda i, j, f: (i, 4, 0, 0)),
            pl.BlockSpec((bb, 1, 1, D_MODEL), lambda i, j, f: (i, 3, 0, 0)),
            pl.BlockSpec((bb, 1, 1, D_MODEL), lambda i, j, f: (i, 5, 0, 0)),
            pl.BlockSpec((D_MODEL, tf), lambda i, j, f: (0, f)),
            pl.BlockSpec((tf, D_MODEL), lambda i, j, f: (f, 0)),
        ],
        out_specs=pl.BlockSpec((bb, tt, D_MODEL), lambda i, j, f: (i, j, 0)),
        out_shape=jax.ShapeDtypeStruct((b, t, D_MODEL), F32),
        scratch_shapes=[pltpu.VMEM((bb * tt, D_MODEL), BF16)],
        compiler_params=_cparams(("parallel", "parallel", "arbitrary")),
        name="ffn",
    )(x, g.reshape(1, 1, D_MODEL), mod, mod, mod, wu, wd)


def _final_norm_kernel(x_ref, g_ref, o_ref):
    x = x_ref[...]
    o_ref[...] = (x * lax.rsqrt(jnp.mean(x * x, axis=-1, keepdims=True) + NORM_EPS)) * g_ref[...]


def _final_norm(x, g, bb, tt):
    b, t, _ = x.shape
    spec = pl.BlockSpec((bb, tt, D_MODEL), lambda i, j: (i, j, 0))
    return pl.pallas_call(
        _final_norm_kernel,
        grid=(b // bb, t // tt),
        in_specs=[spec, pl.BlockSpec((1, 1, D_MODEL), lambda i, j: (0, 0, 0))],
        out_specs=spec,
        out_shape=jax.ShapeDtypeStruct((b, t, D_MODEL), F32),
        compiler_params=_cparams(("parallel", "parallel")),
        name="final_norm",
    )(x, g.reshape(1, 1, D_MODEL))


NPAIR = N_HEADS // 2


def _rwkv_kernel(rkv_ref, lora_ref, sh_rkv_ref, sh_lora_ref, s0_ref, mu_rkv_ref, mu_lora_ref,
                 w0_ref, w2_ref, a0_ref, a2_ref, g2_ref, kkp_ref, ka_ref, rk_ref, lnw_ref, lnb_ref,
                 y_ref, sfin_ref,
                 prev_rkv, prev_lora, s_sc, g_sc, gend_sc, bonus_sc, yo_sc,
                 rt_sc, pt_sc, qt_sc, kt_sc, qe_sc, ke_sc, v_sc):
    tt = rkv_ref.shape[1]
    nc = tt // CHUNK
    i = pl.program_id(1)

    @pl.when(i == 0)
    def _():
        prev_rkv[...] = sh_rkv_ref[0]
        prev_lora[...] = sh_lora_ref[0]
        for hp in range(NPAIR):
            s_sc[hp] = jnp.concatenate([s0_ref[0, 2 * hp], s0_ref[0, 2 * hp + 1]], axis=1)

    rid = lax.broadcasted_iota(jnp.int32, (tt, 1), 0)

    def tshift(x, prev_row, mu):
        prev = jnp.where(rid == 0, prev_row, pltpu.roll(x, 1, 0))
        return x + (prev - x) * mu

    u = rkv_ref[0]
    ul = lora_ref[0]
    xs = tshift(u, prev_rkv[...], mu_rkv_ref[...])
    xl = tshift(ul, prev_lora[...], mu_lora_ref[...])
    prev_rkv[...] = u[tt - 1:tt, :]
    prev_lora[...] = ul[tt - 1:tt, :]

    r = xs[:, 0:BW]
    k = xs[:, BW:2 * BW]
    v = xs[:, 2 * BW:3 * BW]
    wd = xl[:, 0:64]
    ad = xl[:, 64:128]
    gd = xl[:, 128:256]
    lw = -math.exp(-0.5) * jax.nn.sigmoid(w0_ref[...] + _mm(jnp.tanh(wd), w2_ref[...]))
    a = jax.nn.sigmoid(a0_ref[...] + _mm(ad, a2_ref[...]))
    g_sc[...] = _mm(jax.nn.sigmoid(gd), g2_ref[...])
    kkr = k * kkp_ref[...]
    kh = k * (1.0 + (a - 1.0) * ka_ref[...])

    lane = lax.broadcasted_iota(jnp.int32, (CHUNK, 128), 1)
    row = lax.broadcasted_iota(jnp.int32, (CHUNK, 128), 0)
    col_in_head = lane & (HEAD_DIM - 1)
    tri_strict2 = row > col_in_head
    tri_incl2 = row >= col_in_head
    eye2 = (row == col_in_head).astype(F32)
    first_head = lane < HEAD_DIM
    r128 = lax.broadcasted_iota(jnp.int32, (128, 128), 0)
    c128 = lax.broadcasted_iota(jnp.int32, (128, 128), 1)
    same_head = (r128 < HEAD_DIM) == (c128 < HEAD_DIM)
    ones_bd = same_head.astype(BF16)
    tri_incl_bf = _tri(CHUNK).astype(BF16)

    def head_sum(x, terms):
        acc = None
        for part in _split(x, terms):
            d = jnp.dot(part, ones_bd, preferred_element_type=F32)
            acc = d if acc is None else acc + d
        return acc

    cl = jnp.concatenate(
        [_mm_sel(tri_incl_bf, lw[c * CHUNK:(c + 1) * CHUNK, :], terms=2) for c in range(nc)], axis=0)
    cl_last = jnp.concatenate(
        [jnp.broadcast_to(cl[(c + 1) * CHUNK - 1:(c + 1) * CHUNK, :], (CHUNK, BW)) for c in range(nc)], axis=0)
    e_in = jnp.exp(cl)
    e_ex = jnp.exp(cl - lw)
    e_inv = jnp.exp(-cl)
    e_end = jnp.exp(cl_last - cl)
    gend_sc[...] = jnp.exp(jnp.concatenate(
        [jnp.broadcast_to(cl[(c + 1) * CHUNK - 1:(c + 1) * CHUNK, :], (8, BW)) for c in range(nc)], axis=0))
    for hp in range(NPAIR):
        sl = slice(hp * 128, (hp + 1) * 128)
        kk = kkr[:, sl]
        kk = kk * lax.rsqrt(jnp.maximum(head_sum(kk * kk, 2), 1e-24))
        q = kk * a[:, sl]
        rt_sc[hp] = r[:, sl] * e_in[:, sl]
        pt_sc[hp] = -kk * e_ex[:, sl]
        qt_sc[hp] = q * e_inv[:, sl]
        kt_sc[hp] = kh[:, sl] * e_inv[:, sl]
        qe_sc[hp] = q * e_end[:, sl]
        ke_sc[hp] = kh[:, sl] * e_end[:, sl]
        v_sc[hp] = v[:, sl]
        bonus_sc[:, sl] = head_sum(r[:, sl] * kh[:, sl] * rk_ref[:, sl], 1) * v[:, sl]

    def bd(x):
        return jnp.where(same_head, jnp.concatenate([x, x], axis=0), jnp.zeros((), x.dtype))

    def parts(x):
        hi = _bf(x)
        return hi, _bf(x - hi.astype(F32))

    def bd2(p):
        return bd(p[0]), bd(p[1])

    def dot3(a, b, dims):
        d = lambda x, y: lax.dot_general(x, y, (dims, ((), ())), preferred_element_type=F32)
        return d(a[0], b[0]) + (d(a[0], b[1]) + d(a[1], b[0]))

    nn, nt, tn = ((1,), (0,)), ((1,), (1,)), ((0,), (0,))

    def chunk_body(c, carry):
        rows = pl.ds(pl.multiple_of(c * CHUNK, CHUNK), CHUNK)
        grow = pl.ds(pl.multiple_of(c * 8, 8), 8)
        pairs = range(NPAIR)
        vv = [parts(v_sc[hp, rows, :]) for hp in pairs]
        bdv = [bd2(vv[hp]) for hp in pairs]
        s0 = [s_sc[hp] for hp in pairs]
        lhs = [parts(jnp.concatenate([pt_sc[hp, rows, :], rt_sc[hp, rows, :]], axis=0)) for hp in pairs]
        gq = [dot3(lhs[hp], bd2(parts(qt_sc[hp, rows, :])), nt) for hp in pairs]
        gk = [dot3(lhs[hp], bd2(parts(kt_sc[hp, rows, :])), nt) for hp in pairs]
        ps = [dot3(lhs[hp], bd2(parts(s0[hp])), nt) for hp in pairs]
        l_pq = [jnp.where(tri_strict2, gq[hp][0:CHUNK], 0.0) for hp in pairs]
        rhs = [ps[hp][0:CHUNK] + dot3(parts(jnp.where(tri_strict2, gk[hp][0:CHUNK], 0.0)), bdv[hp], nn)
               for hp in pairs]
        tinv = [eye2 + l_pq[hp] for hp in pairs]
        xpp = [parts(l_pq[hp]) for hp in pairs]
        for _ in range(5):
            xpp = [parts(dot3(xpp[hp], bd2(xpp[hp]), nn)) for hp in pairs]
            tinv = [tinv[hp] + dot3(parts(tinv[hp]), bd2(xpp[hp]), nn) for hp in pairs]
        uu = [parts(dot3(parts(tinv[hp]), bd2(parts(rhs[hp])), nn)) for hp in pairs]
        for hp in pairs:
            sl = slice(hp * 128, (hp + 1) * 128)
            a_rq = jnp.where(tri_incl2, gq[hp][CHUNK:2 * CHUNK], 0.0)
            a_rk = jnp.where(tri_incl2, gk[hp][CHUNK:2 * CHUNK], 0.0)
            yo_sc[rows, sl] = ps[hp][CHUNK:2 * CHUNK] + _mm(
                jnp.concatenate([_bf(a_rq), _bf(a_rk)], axis=1),
                jnp.concatenate([bd(uu[hp][0]), bdv[hp][0]], axis=0))
        for hp in pairs:
            sl = slice(hp * 128, (hp + 1) * 128)
            uv = tuple(jnp.concatenate([uu[hp][t], vv[hp][t]], axis=0) for t in range(2))
            qk = parts(jnp.concatenate([qe_sc[hp, rows, :], ke_sc[hp, rows, :]], axis=0))
            full = dot3(uv, qk, tn)
            g_end = gend_sc[grow, sl][0:1, :]
            s_sc[hp] = s0[hp] * g_end + jnp.where(
                first_head, full[0:HEAD_DIM], full[HEAD_DIM:2 * HEAD_DIM])
        return carry

    lax.fori_loop(0, nc, chunk_body, 0)

    for hp in range(NPAIR):
        sl = slice(hp * 128, (hp + 1) * 128)
        yh = yo_sc[:, sl]
        mean = head_sum(yh, 1) * (1.0 / HEAD_DIM)
        d = yh - mean
        var = head_sum(d * d, 1) * (1.0 / HEAD_DIM)
        yn = d * lax.rsqrt(var + RW_GN_EPS) * lnw_ref[:, sl] + lnb_ref[:, sl]
        y_ref[0, :, sl] = _bf((yn + bonus_sc[:, sl]) * g_sc[:, sl])

    @pl.when(i == pl.num_programs(1) - 1)
    def _():
        for hp in range(NPAIR):
            sp = s_sc[hp]
            sfin_ref[0, 2 * hp] = sp[:, 0:HEAD_DIM]
            sfin_ref[0, 2 * hp + 1] = sp[:, HEAD_DIM:2 * HEAD_DIM]


def _rwkv(u, shift_hist, s0, p, tt):
    b, t, _ = u.shape
    row = lambda a: a.reshape(1, -1)
    sh = shift_hist.reshape(b, 1, RW_IN)
    sh_rkv = sh[:, :, :3 * BW]
    sh_lora = sh[:, :, 3 * BW:]
    mu = p["rw_mu"]
    full = lambda shape: pl.BlockSpec(shape, lambda i, j: (0,) * len(shape))
    scr = lambda: pltpu.VMEM((NPAIR, tt, 128), F32)
    wide = lambda rows: pltpu.VMEM((rows, BW), F32)
    return pl.pallas_call(
        _rwkv_kernel,
        grid=(b, t // tt),
        in_specs=[
            pl.BlockSpec((1, tt, 3 * BW), lambda i, j: (i, j, C_RKV // (3 * BW))),
            pl.BlockSpec((1, tt, RW_LORA), lambda i, j: (i, j, C_LORA // RW_LORA)),
            pl.BlockSpec((1, 1, 3 * BW), lambda i, j: (i, 0, 0)),
            pl.BlockSpec((1, 1, RW_LORA), lambda i, j: (i, 0, 0)),
            pl.BlockSpec((1, N_HEADS, HEAD_DIM, HEAD_DIM), lambda i, j: (i, 0, 0, 0)),
            full((1, 3 * BW)), full((1, RW_LORA)),
            full((1, BW)), full((64, BW)), full((1, BW)), full((64, BW)), full((128, BW)),
            full((1, BW)), full((1, BW)),
            full((1, BW)), full((1, BW)), full((1, BW)),
        ],
        out_specs=[
            pl.BlockSpec((1, tt, BW), lambda i, j: (i, j, 0)),
            pl.BlockSpec((1, N_HEADS, HEAD_DIM, HEAD_DIM), lambda i, j: (i, 0, 0, 0)),
        ],
        out_shape=[
            jax.ShapeDtypeStruct((b, t, BW), BF16),
            jax.ShapeDtypeStruct((b, N_HEADS, HEAD_DIM, HEAD_DIM), F32),
        ],
        scratch_shapes=[
            pltpu.VMEM((1, 3 * BW), F32), pltpu.VMEM((1, RW_LORA), F32),
            pltpu.VMEM((NPAIR, HEAD_DIM, 128), F32),
            wide(tt), wide(8 * (tt // CHUNK)), wide(tt), wide(tt),
            scr(), scr(), scr(), scr(), scr(), scr(), scr(),
        ],
        compiler_params=_cparams(("parallel", "arbitrary")),
        name="rwkv7",
    )(u, u, sh_rkv, sh_lora, s0, row(mu[:3 * BW]), row(mu[3 * BW:]),
      row(p["rw_w0"]), p["rw_w2"], row(p["rw_a0"]), p["rw_a2"], p["rw_g2"],
      row(p["rw_kk"]), row(p["rw_ka"]), row(p["rw_rk"]), row(p["rw_ln_w"]), row(p["rw_ln_b"]))


def _pool_kernel(pos0, u_ref, prev_ref, hist_ref, w_ref, scale_ref, y_ref):
    tt = u_ref.shape[1]
    i = pl.program_id(1)
    x = u_ref[0]
    prev = jnp.where(i == 0, hist_ref[0], prev_ref[0])
    s = jnp.concatenate([prev, x], axis=0)
    pos = pos0 + i * tt + lax.broadcasted_iota(jnp.int32, (tt, 1), 0)
    outs = []
    for gi, win in enumerate(POOL_WINDOWS):
        s = s + pltpu.roll(s, win // 2, 0)
        sl = slice(gi * POOL_GW, (gi + 1) * POOL_GW)
        cnt = jnp.minimum(pos + 1, win).astype(F32)
        pg = s[16:, sl] / cnt - x[:, sl]
        outs.append(_mm(pg, w_ref[gi]))
    y_ref[0] = _bf(jnp.concatenate(outs, axis=1) * scale_ref[...])


def _pool(u, hist16, pos0, w_pool, scale, tt):
    b, t, _ = u.shape
    nprev = tt // 16
    return pl.pallas_call(
        functools.partial(_pool_kernel, pos0),
        grid=(b, t // tt),
        in_specs=[
            pl.BlockSpec((1, tt, BW), lambda i, j: (i, j, C_POOL // BW)),
            pl.BlockSpec((1, 16, BW), lambda i, j: (i, jnp.maximum(j * nprev - 1, 0), C_POOL // BW)),
            pl.BlockSpec((1, 16, BW), lambda i, j: (i, 0, 0)),
            pl.BlockSpec((4, POOL_GW, POOL_GW), lambda i, j: (0, 0, 0)),
            pl.BlockSpec((1, BW), lambda i, j: (0, 0)),
        ],
        out_specs=pl.BlockSpec((1, tt, BW), lambda i, j: (i, j, 0)),
        out_shape=jax.ShapeDtypeStruct((b, t, BW), BF16),
        compiler_params=_cparams(("parallel", "parallel")),
        name="pool",
    )(u, u, hist16, w_pool, scale.reshape(1, BW))


def _swa_kernel(mask_history, q_ref, kp2_ref, kp1_ref, kc_ref, vp2_ref, vp1_ref, vc_ref, sink_ref, y_ref):
    tq = q_ref.shape[1]
    nc = tq // CHUNK
    i = pl.program_id(1)
    q = q_ref[0]
    k_all = jnp.concatenate([kp2_ref[0], kp1_ref[0], kc_ref[0]], axis=0)
    v_all = jnp.concatenate([vp2_ref[0], vp1_ref[0], vc_ref[0]], axis=0)
    nk = 3 * CHUNK
    qi = lax.broadcasted_iota(jnp.int32, (CHUNK, nk), 0)
    si = lax.broadcasted_iota(jnp.int32, (CHUNK, nk), 1)
    dist1 = jnp.abs(qi + 2 * CHUNK - si).astype(F32)
    dist = jnp.concatenate([dist1] * SWA_GROUP, axis=0)
    scol = lax.broadcasted_iota(jnp.int32, (SWA_GROUP * CHUNK, nk), 1)
    units = [(c, g) for c in range(nc) for g in range(SWA_KV)]
    bias = []
    sinks = []
    for g in range(SWA_KV):
        heads = [g * SWA_GROUP + hh for hh in range(SWA_GROUP)]
        slope = jnp.concatenate(
            [jnp.full((CHUNK, 1), 2.0 ** (-8.0 * (h + 1) / N_HEADS), F32) for h in heads], axis=0)
        bias.append(slope * dist)
        sinks.append(jnp.concatenate([jnp.full((CHUNK, 1), sink_ref[h], F32) for h in heads], axis=0))

    def scores(c, g):
        ks = k_all[c * CHUNK:c * CHUNK + nk, g * HEAD_DIM:(g + 1) * HEAD_DIM]
        qs = jnp.concatenate(
            [q[c * CHUNK:(c + 1) * CHUNK, (g * SWA_GROUP + hh) * HEAD_DIM:(g * SWA_GROUP + hh + 1) * HEAD_DIM]
             for hh in range(SWA_GROUP)], axis=0)
        return _mm_nt(qs, ks)

    def probs(s, c, g):
        s = s * (HEAD_DIM ** -0.5) - bias[g]
        if mask_history:
            s = jnp.where(scol >= (2 - (i * nc + c)) * CHUNK, s, -1e30)
        m = jnp.maximum(jnp.max(s, axis=-1, keepdims=True), sinks[g])
        p = jnp.exp(s - m)
        denom = jnp.sum(p, axis=-1, keepdims=True) + jnp.exp(sinks[g] - m)
        return _bf(p * (1.0 / denom))

    s_all = [scores(c, g) for c, g in units]
    p_all = [probs(s, c, g) for s, (c, g) in zip(s_all, units)]
    o_all = [_mm(p, v_all[c * CHUNK:c * CHUNK + nk, g * HEAD_DIM:(g + 1) * HEAD_DIM])
             for p, (c, g) in zip(p_all, units)]
    for o, (c, g) in zip(o_all, units):
        for hh in range(SWA_GROUP):
            h = g * SWA_GROUP + hh
            y_ref[0, c * CHUNK:(c + 1) * CHUNK, h * HEAD_DIM:(h + 1) * HEAD_DIM] = _bf(
                o[hh * CHUNK:(hh + 1) * CHUNK])


def _swa(u, hist_k, hist_v, sinks, tq):
    b, t, _ = u.shape
    per = tq // CHUNK
    kvw = SWA_KV * HEAD_DIM
    mask_history = hist_k is None
    cur_k = pl.BlockSpec((1, tq, kvw), lambda i, j: (i, j, C_KS // kvw))
    cur_v = pl.BlockSpec((1, tq, kvw), lambda i, j: (i, j, C_VS // kvw))
    if mask_history:
        prev = lambda d, col: pl.BlockSpec(
            (1, CHUNK, kvw), lambda i, j: (i, jnp.maximum(j * per - d, 0), col))
        specs = [prev(2, C_KS // kvw), prev(1, C_KS // kvw), cur_k,
                 prev(2, C_VS // kvw), prev(1, C_VS // kvw), cur_v]
        args = (u, u, u, u, u, u)
    else:
        assert t == tq
        hk = hist_k.reshape(b, WINDOW, kvw)
        hv = hist_v.reshape(b, WINDOW, kvw)
        hist = lambda blk: pl.BlockSpec((1, CHUNK, kvw), lambda i, j: (i, blk, 0))
        specs = [hist(0), hist(1), cur_k, hist(0), hist(1), cur_v]
        args = (hk, hk, u, hv, hv, u)
    return pl.pallas_call(
        functools.partial(_swa_kernel, mask_history),
        grid=(b, t // tq),
        in_specs=[pl.BlockSpec((1, tq, BW), lambda i, j: (i, j, C_Q // BW))] + specs
        + [pl.BlockSpec(memory_space=pltpu.SMEM)],
        out_specs=pl.BlockSpec((1, tq, BW), lambda i, j: (i, j, 0)),
        out_shape=jax.ShapeDtypeStruct((b, t, BW), BF16),
        compiler_params=_cparams(("parallel", "parallel")),
        name="swa",
    )(u, *args, sinks)


def _ssd_kernel(z_ref, x_ref, b_ref, c_ref, dt_ref, hx_ref, hb_ref, hc_ref, s0_ref,
                cwx_ref, cwb_ref, cwc_ref, cbx_ref, cbb_ref, cbc_ref,
                dtb_ref, alog_ref, dsk_ref, nw_ref,
                y_ref, sfin_ref,
                px_sc, pb_sc, pc_sc, s_sc, xa_sc, ba_sc, ca_sc, acs_sc, gend_sc,
                acsx_sc, eout_sc, y_sc, xdt_sc, xde_sc):
    tt = x_ref.shape[1]
    nc = tt // CHUNK
    i = pl.program_id(1)

    @pl.when(i == 0)
    def _():
        px_sc[...] = hx_ref[0]
        pb_sc[...] = hb_ref[0]
        pc_sc[...] = hc_ref[0]
        s_sc[...] = s0_ref[0].reshape(N_HEADS * HEAD_DIM, SSM_STATE)

    rid8 = lax.broadcasted_iota(jnp.int32, (8, 1), 0)

    def conv_silu(x, prev8, w_ref, b_ref_):
        acc = None
        for wi in range(SSM_CONV):
            sh = SSM_CONV - 1 - wi
            if sh == 0:
                xs = x
            else:
                rolled = pltpu.roll(x, sh, 0)
                top = jnp.where(rid8 < sh, pltpu.roll(prev8, sh, 0), rolled[0:8])
                xs = jnp.concatenate([top, rolled[8:]], axis=0)
            term = xs * w_ref[wi:wi + 1, :]
            acc = (b_ref_[...] + term) if acc is None else acc + term
        return _silu(acc)

    xr = x_ref[0]
    br = b_ref[0]
    cr = c_ref[0]
    xa_sc[...] = conv_silu(xr, px_sc[...], cwx_ref, cbx_ref)
    ba_sc[...] = conv_silu(br, pb_sc[...], cwb_ref, cbb_ref)
    ca_sc[...] = conv_silu(cr, pc_sc[...], cwc_ref, cbc_ref)
    px_sc[...] = xr[tt - 8:tt, :]
    pb_sc[...] = br[tt - 8:tt, :]
    pc_sc[...] = cr[tt - 8:tt, :]
    hl = 128
    dt = _softplus(dt_ref[0][:, 0:hl] + dtb_ref[...])
    ad = -jnp.exp(alog_ref[...]) * dt

    tri_incl_bf = _tri(CHUNK).astype(BF16)
    sel16 = (lax.broadcasted_iota(jnp.int32, (N_HEADS, hl), 0)
             == lax.broadcasted_iota(jnp.int32, (N_HEADS, hl), 1)).astype(BF16)
    spread = (lax.broadcasted_iota(jnp.int32, (hl, BW), 0)
              == lax.broadcasted_iota(jnp.int32, (hl, BW), 1) // HEAD_DIM).astype(BF16)

    def per_channel(x, terms):
        acc = None
        for part in _split(x, terms):
            d = jnp.dot(part, spread, preferred_element_type=F32)
            acc = d if acc is None else acc + d
        return acc

    acs = jnp.concatenate(
        [_mm_sel(tri_incl_bf, ad[c * CHUNK:(c + 1) * CHUNK, :]) for c in range(nc)], axis=0)
    a_last = jnp.concatenate(
        [jnp.broadcast_to(acs[(c + 1) * CHUNK - 1:(c + 1) * CHUNK, :], (CHUNK, hl)) for c in range(nc)], axis=0)
    acs_sc[...] = acs
    acsx_sc[...] = per_channel(acs, 3)
    eout_sc[...] = per_channel(jnp.exp(acs), 2)
    gend_sc[...] = jnp.exp(jnp.concatenate(
        [jnp.broadcast_to(acs[(c + 1) * CHUNK - 1:(c + 1) * CHUNK, :], (8, hl)) for c in range(nc)], axis=0))
    xdt = xa_sc[...] * per_channel(dt, 2)
    xdt_sc[...] = _bf(xdt)
    xde_sc[...] = _bf(xdt * per_channel(jnp.exp(a_last - acs), 2))

    lane = lax.broadcasted_iota(jnp.int32, (CHUNK, 128), 1)
    row = lax.broadcasted_iota(jnp.int32, (CHUNK, 128), 0)
    tri_incl2 = row >= (lane & (HEAD_DIM - 1))
    r128 = lax.broadcasted_iota(jnp.int32, (128, 128), 0)
    c128 = lax.broadcasted_iota(jnp.int32, (128, 128), 1)
    same_head = (r128 < HEAD_DIM) == (c128 < HEAD_DIM)
    hpg = N_HEADS // SSM_GROUPS
    rpg = hpg * HEAD_DIM

    def chunk_body(c, carry):
        rows = pl.ds(pl.multiple_of(c * CHUNK, CHUNK), CHUNK)
        grow = pl.ds(pl.multiple_of(c * 8, 8), 8)
        bc = _bf(ba_sc[rows, :])
        cc = _bf(ca_sc[rows, :])
        acs_t = _mm_sel_nt(sel16, acs_sc[rows, :])
        g_end = gend_sc[grow, :][0:1, :]
        gsl = [slice(gi * SSM_STATE, (gi + 1) * SSM_STATE) for gi in range(SSM_GROUPS)]
        scores = [_mm_nt(cc[:, gs], bc[:, gs]) for gs in gsl]
        scores2 = [jnp.concatenate([s, s], axis=1) for s in scores]
        y_off = [_mm_nt(cc[:, gsl[gi]], s_sc[gi * rpg:(gi + 1) * rpg, :]) for gi in range(SSM_GROUPS)]
        upd = [_mm_tn(xde_sc[rows, gi * rpg:(gi + 1) * rpg], bc[:, gsl[gi]]) for gi in range(SSM_GROUPS)]
        y_diag = []
        for hp in range(N_HEADS // 2):
            sl = slice(hp * 128, (hp + 1) * 128)
            seg = acsx_sc[rows, sl] - jnp.concatenate([acs_t[2 * hp:2 * hp + 1, :],
                                                       acs_t[2 * hp + 1:2 * hp + 2, :]], axis=1)
            m = scores2[(2 * hp) // hpg] * jnp.exp(jnp.where(tri_incl2, seg, -1e30))
            xd = xdt_sc[rows, sl]
            y_diag.append(_mm(m, jnp.where(same_head, jnp.concatenate([xd, xd], axis=0),
                                           jnp.zeros((), BF16))))
        for h in range(N_HEADS):
            gi, hl_ = divmod(h, hpg)
            hrows = slice(h * HEAD_DIM, (h + 1) * HEAD_DIM)
            s_sc[hrows, :] = (s_sc[hrows, :] * g_end[:, h:h + 1]
                              + upd[gi][hl_ * HEAD_DIM:(hl_ + 1) * HEAD_DIM, :])
        y_sc[rows, :] = (jnp.concatenate(y_diag, axis=1)
                         + jnp.concatenate(y_off, axis=1) * eout_sc[rows, :])
        return carry

    lax.fori_loop(0, nc, chunk_body, 0)

    y = (y_sc[...] + xa_sc[...] * dsk_ref[...]) * _silu(z_ref[0])
    gw = BW // SSM_GROUPS
    parts = []
    for gi in range(SSM_GROUPS):
        yg = y[:, gi * gw:(gi + 1) * gw]
        parts.append(yg * lax.rsqrt(jnp.mean(yg * yg, axis=-1, keepdims=True) + NORM_EPS))
    y_ref[0] = _bf(jnp.concatenate(parts, axis=1) * nw_ref[...])

    @pl.when(i == pl.num_programs(1) - 1)
    def _():
        sfin_ref[0] = s_sc[...].reshape(N_HEADS, HEAD_DIM, SSM_STATE)


def _pad_rows8(a):
    return jnp.pad(a, ((0, 0), (8 - a.shape[1], 0), (0, 0)))


def _pad_lanes(a, n):
    return jnp.pad(a.reshape(1, -1), ((0, 0), (0, n - a.shape[-1])))


def _ssd(u, conv_hist, s0, p, tt):
    b, t, _ = u.shape
    h8 = _pad_rows8(conv_hist)
    hx, hb, hc = h8[:, :, :BW], h8[:, :, BW:BW + SSM_BC], h8[:, :, BW + SSM_BC:]
    cw, cb = p["ssm_conv_w"], p["ssm_conv_b"].reshape(1, -1)
    full = lambda shape: pl.BlockSpec(shape, lambda i, j: (0,) * len(shape))
    col = lambda w, c0: pl.BlockSpec((1, tt, w), lambda i, j: (i, j, c0 // w))
    hist = lambda w: pl.BlockSpec((1, 8, w), lambda i, j: (i, 0, 0))
    return pl.pallas_call(
        _ssd_kernel,
        grid=(b, t // tt),
        in_specs=[
            col(BW, C_Z), col(BW, C_X), col(SSM_BC, C_B), col(SSM_BC, C_C), col(SSM_BC, C_DT),
            hist(BW), hist(SSM_BC), hist(SSM_BC),
            pl.BlockSpec((1, N_HEADS, HEAD_DIM, SSM_STATE), lambda i, j: (i, 0, 0, 0)),
            full((SSM_CONV, BW)), full((SSM_CONV, SSM_BC)), full((SSM_CONV, SSM_BC)),
            full((1, BW)), full((1, SSM_BC)), full((1, SSM_BC)),
            full((1, 128)), full((1, 128)), full((1, BW)), full((1, BW)),
        ],
        out_specs=[
            pl.BlockSpec((1, tt, BW), lambda i, j: (i, j, 0)),
            pl.BlockSpec((1, N_HEADS, HEAD_DIM, SSM_STATE), lambda i, j: (i, 0, 0, 0)),
        ],
        out_shape=[
            jax.ShapeDtypeStruct((b, t, BW), BF16),
            jax.ShapeDtypeStruct((b, N_HEADS, HEAD_DIM, SSM_STATE), F32),
        ],
        scratch_shapes=[
            pltpu.VMEM((8, BW), F32), pltpu.VMEM((8, SSM_BC), F32), pltpu.VMEM((8, SSM_BC), F32),
            pltpu.VMEM((N_HEADS * HEAD_DIM, SSM_STATE), F32),
            pltpu.VMEM((tt, BW), F32), pltpu.VMEM((tt, SSM_BC), F32), pltpu.VMEM((tt, SSM_BC), F32),
            pltpu.VMEM((tt, 128), F32), pltpu.VMEM((8 * (tt // CHUNK), 128), F32),
            pltpu.VMEM((tt, BW), F32), pltpu.VMEM((tt, BW), F32), pltpu.VMEM((tt, BW), F32),
            pltpu.VMEM((tt, BW), BF16), pltpu.VMEM((tt, BW), BF16),
        ],
        compiler_params=_cparams(("parallel", "arbitrary")),
        name="ssd",
    )(u, u, u, u, u, hx, hb, hc, s0,
      cw[:, :BW], cw[:, BW:BW + SSM_BC], cw[:, BW + SSM_BC:],
      cb[:, :BW], cb[:, BW:BW + SSM_BC], cb[:, BW + SSM_BC:],
      _pad_lanes(p["ssm_dt_bias"], 128), _pad_lanes(p["ssm_a_log"], 128),
      jnp.repeat(p["ssm_d"], HEAD_DIM).reshape(1, BW), p["ssm_norm"].reshape(1, BW))


def _prep_w_in(w_in):
    c = lambda a, n: w_in[:, :, a:a + n]
    ssm = _N_SSM
    parts = [
        c(0, 3 * BW),
        c(ssm, BW),
        c(_N_POOL, BW),
        c(_N_SWA, BW),
        c(ssm + BW, BW),
        c(3 * BW, RW_LORA),
        c(_N_SWA + BW, 2 * SWA_KV * HEAD_DIM),
        c(ssm + 2 * BW, 2 * SSM_BC + N_HEADS),
    ]
    w = jnp.concatenate(parts, axis=-1).astype(BF16)
    return jnp.pad(w, ((0, 0), (0, 0), (0, U_COLS - w.shape[-1])))


def _run_group(x, mods, st, p, final_norm, wts, pos0, bb, tt, tt_branch):
    b, t, _ = x.shape
    kvw = SWA_KV * HEAD_DIM
    outs = {k: [] for k in ("rwkv", "shift", "pool", "k", "v", "ssm", "conv")}
    for l in range(DEPTH):
        pl_ = {k: v[l] for k, v in p.items()}
        mod = mods[l]
        u, h = _inproj(x, pl_["norm_mix"], mod, wts["w_in"][l], bb, tt)
        if st is None:
            shift_hist = jnp.zeros((b, 1, RW_IN), F32)
            s_rwkv = jnp.zeros((b, N_HEADS, HEAD_DIM, HEAD_DIM), F32)
            pool_hist = jnp.zeros((b, POOL_HIST, BW), F32)
            hk = hv = None
            s_ssm = jnp.zeros((b, N_HEADS, HEAD_DIM, SSM_STATE), F32)
            conv_hist = jnp.zeros((b, SSM_CONV - 1, SSM_CONV_DIM), F32)
        else:
            shift_hist, s_rwkv, pool_hist = st["shift"][l], st["rwkv"][l], st["pool"][l]
            hk, hv, s_ssm, conv_hist = st["k"][l], st["v"][l], st["ssm"][l], st["conv"][l]
        y_a, n_rwkv = _rwkv(u, shift_hist, s_rwkv, pl_, tt_branch)
        hist16 = jnp.pad(pool_hist, ((0, 0), (1, 0), (0, 0)))
        y_b = _pool(u, hist16, pos0, wts["pool_w"][l], pl_["pool_scale"], tt_branch)
        y_c = _swa(u, hk, hv, pl_["swa_sinks"], min(tt_branch, 256))
        y_d, n_ssm = _ssd(u, conv_hist, s_ssm, pl_, tt_branch)
        merged = _merge(h, (y_a, y_b, y_c, y_d), wts["w_gate"][l], wts["w_branch"][l], bb, tt)
        x = _outproj(merged, wts["w_out"][l], x, mod, bb, tt)
        x = _ffn(x, pl_["norm_ffn"], mod, wts["w_up"][l], wts["w_down"][l], bb, tt)

        outs["rwkv"].append(n_rwkv)
        outs["shift"].append(jnp.concatenate(
            [u[:, t - 1:, C_RKV:C_RKV + 3 * BW], u[:, t - 1:, C_LORA:C_LORA + RW_LORA]], axis=-1))
        outs["pool"].append(u[:, t - POOL_HIST:, C_POOL:C_POOL + BW])
        k_new = u[:, :, C_KS:C_KS + kvw]
        v_new = u[:, :, C_VS:C_VS + kvw]
        if hk is not None:
            k_new = jnp.concatenate([hk.reshape(b, WINDOW, kvw), k_new], axis=1)
            v_new = jnp.concatenate([hv.reshape(b, WINDOW, kvw), v_new], axis=1)
        outs["k"].append(k_new[:, -WINDOW:].reshape(b, WINDOW, SWA_KV, HEAD_DIM))
        outs["v"].append(v_new[:, -WINDOW:].reshape(b, WINDOW, SWA_KV, HEAD_DIM))
        outs["ssm"].append(n_ssm)
        outs["conv"].append(jnp.concatenate(
            [u[:, t - 3:, C_X:C_X + BW], u[:, t - 3:, C_B:C_B + 2 * SSM_BC]], axis=-1))
    y = _final_norm(x, final_norm, bb, tt)
    order = ("rwkv", "shift", "pool", "k", "v", "ssm", "conv")
    return y, tuple(jnp.stack(outs[k]) for k in order)


def kernel(x_prompt, x_sample, state_rwkv, state_rwkv_shift, state_pool, cache_swa_k, cache_swa_v,
           state_ssm, state_ssm_conv, c_prompt, c_sample, ada_w, ada_b, norm_mix, norm_ffn, w_in,
           rw_mu, rw_w0, rw_w2, rw_a0, rw_a2, rw_g2, rw_kk, rw_ka, rw_rk, rw_ln_w, rw_ln_b, pool_w,
           pool_scale, swa_sinks, ssm_conv_w, ssm_conv_b, ssm_dt_bias, ssm_a_log, ssm_d, ssm_norm,
           w_gate, w_branch, w_out, w_up, w_down, final_norm):
    bp, tp, _ = x_prompt.shape
    bs, ts, _ = x_sample.shape
    p = dict(norm_mix=norm_mix, norm_ffn=norm_ffn, rw_mu=rw_mu, rw_w0=rw_w0, rw_w2=rw_w2, rw_a0=rw_a0,
             rw_a2=rw_a2, rw_g2=rw_g2, rw_kk=rw_kk, rw_ka=rw_ka, rw_rk=rw_rk, rw_ln_w=rw_ln_w,
             rw_ln_b=rw_ln_b, pool_scale=pool_scale, swa_sinks=swa_sinks, ssm_conv_w=ssm_conv_w,
             ssm_conv_b=ssm_conv_b, ssm_dt_bias=ssm_dt_bias, ssm_a_log=ssm_a_log, ssm_d=ssm_d,
             ssm_norm=ssm_norm)
    wts = dict(w_in=_prep_w_in(w_in), pool_w=pool_w.astype(BF16), w_gate=w_gate.astype(BF16),
               w_branch=w_branch.astype(BF16), w_out=w_out.astype(BF16), w_up=w_up.astype(BF16),
               w_down=w_down.astype(BF16))

    nb = bp + bs
    nb_pad = -(-nb // 8) * 8
    c_all = jnp.pad(jnp.concatenate([c_prompt, c_sample], axis=0), ((0, nb_pad - nb), (0, 0)))
    mod_all = _adaln(c_all, ada_w, ada_b)
    mods_p = mod_all[:, :bp].reshape(DEPTH, bp, 6, 1, D_MODEL)
    mods_s = mod_all[:, bp:nb].reshape(DEPTH, bs, 6, 1, D_MODEL)

    y_prompt, st_p = _run_group(x_prompt, mods_p, None, p, final_norm, wts, 0,
                                1, min(tp, 1024), min(tp, 512))
    st_s = dict(rwkv=state_rwkv, shift=state_rwkv_shift, pool=state_pool, k=cache_swa_k,
                v=cache_swa_v, ssm=state_ssm, conv=state_ssm_conv)
    y_sample, st_o = _run_group(x_sample, mods_s, st_s, p, final_norm, wts, PAST_LEN, bs, ts, ts)
    return (y_prompt, y_sample) + st_p + st_o
```

```python
import functools
import math

import jax
import jax.numpy as jnp
from jax import lax
from jax.experimental import pallas as pl
from jax.experimental.pallas import tpu as pltpu

F32 = jnp.float32
BF16 = jnp.bfloat16

D_MODEL = 2048
DEPTH = 4
PAST_LEN = 1024
CHUNK = 64
HEAD_DIM = 64
BW = D_MODEL // 2
D_FF = 4 * D_MODEL
NORM_EPS = 1e-6
N_HEADS = BW // HEAD_DIM
RW_LORA = 256
RW_IN = 3 * BW + RW_LORA
RW_GN_EPS = HEAD_DIM * 1e-5
POOL_WINDOWS = (2, 4, 8, 16)
POOL_GW = BW // 4
POOL_HIST = 15
SWA_KV = 4
SWA_GROUP = N_HEADS // SWA_KV
WINDOW = 128
SSM_GROUPS = 2
SSM_STATE = 128
SSM_CONV = 4
SSM_BC = SSM_GROUPS * SSM_STATE
SSM_CONV_DIM = BW + 2 * SSM_BC

C_RKV = 0
C_Z = 3072
C_POOL = 4096
C_Q = 5120
C_X = 6144
C_LORA = 7168
C_KS = 7424
C_VS = 7680
C_B = 7936
C_C = 8192
C_DT = 8448
U_COLS = 8704

_N_POOL = RW_IN
_N_SWA = _N_POOL + BW
_N_SSM = _N_SWA + (N_HEADS + 2 * SWA_KV) * HEAD_DIM
IN_COLS = _N_SSM + BW + SSM_CONV_DIM + N_HEADS

VMEM_LIMIT = 56 * 1024 * 1024


def _cparams(sem):
    return pltpu.CompilerParams(dimension_semantics=sem, vmem_limit_bytes=VMEM_LIMIT)


def _bf(x):
    return x.astype(BF16)


def _mm(a, b):
    return jnp.dot(_bf(a), _bf(b), preferred_element_type=F32)


def _mm_nt(a, b):
    return lax.dot_general(_bf(a), _bf(b), (((1,), (1,)), ((), ())), preferred_element_type=F32)


def _mm_tn(a, b):
    return lax.dot_general(_bf(a), _bf(b), (((0,), (0,)), ((), ())), preferred_element_type=F32)


def _split3(x):
    hi = _bf(x)
    r1 = x - hi.astype(F32)
    mid = _bf(r1)
    lo = _bf(r1 - mid.astype(F32))
    return hi, mid, lo


def _split(x, terms):
    parts = []
    for _ in range(terms - 1):
        hi = _bf(x)
        parts.append(hi)
        x = x - hi.astype(F32)
    parts.append(_bf(x))
    return parts


def _mm_sel(c, x, terms=3):
    acc = None
    for part in _split(x, terms):
        d = jnp.dot(c, part, preferred_element_type=F32)
        acc = d if acc is None else acc + d
    return acc


def _mm_sel_nt(c, x):
    hi, mid, lo = _split3(x)
    d = lambda p: lax.dot_general(c, p, (((1,), (1,)), ((), ())), preferred_element_type=F32)
    return d(hi) + d(mid) + d(lo)


def _softplus(x):
    return jnp.maximum(x, 0.0) + jnp.log1p(jnp.exp(-jnp.abs(x)))


def _silu(x):
    return x * jax.nn.sigmoid(x)


def _tri(n, strict=False):
    r = lax.broadcasted_iota(jnp.int32, (n, n), 0)
    c = lax.broadcasted_iota(jnp.int32, (n, n), 1)
    return (r > c) if strict else (r >= c)


def _adaln_kernel(c_ref, w_ref, b_ref, o_ref):
    s = _silu(c_ref[...])
    o_ref[0] = _mm(s, w_ref[0]) + b_ref[0]


def _adaln(c_all, ada_w, ada_b):
    nb = c_all.shape[0]
    tn = 1024
    n_out = ada_w.shape[2]
    return pl.pallas_call(
        _adaln_kernel,
        grid=(DEPTH, n_out // tn),
        in_specs=[
            pl.BlockSpec((nb, D_MODEL), lambda l, n: (0, 0)),
            pl.BlockSpec((1, D_MODEL, tn), lambda l, n: (l, 0, n)),
            pl.BlockSpec((1, 1, tn), lambda l, n: (l, 0, n)),
        ],
        out_specs=pl.BlockSpec((1, nb, tn), lambda l, n: (l, 0, n)),
        out_shape=jax.ShapeDtypeStruct((DEPTH, nb, n_out), F32),
        compiler_params=_cparams(("parallel", "parallel")),
        name="adaln",
    )(c_all, ada_w, ada_b.reshape(DEPTH, 1, n_out))


def _norm_mod(x, g, sc, sh):
    y = x * lax.rsqrt(jnp.mean(x * x, axis=-1, keepdims=True) + NORM_EPS)
    return (y * g) * (1.0 + sc) + sh


def _inproj_kernel(x_ref, g_ref, sc_ref, sh_ref, w_ref, u_ref, h_ref):
    bb, tt, _ = x_ref.shape

    @pl.when(pl.program_id(2) == 0)
    def _():
        h = _norm_mod(x_ref[...], g_ref[...], sc_ref[:, 0], sh_ref[:, 0])
        h_ref[...] = _bf(h)

    h = h_ref[...].reshape(bb * tt, D_MODEL)
    u_ref[...] = jnp.dot(h, w_ref[...], preferred_element_type=F32).reshape(u_ref.shape)


def _inproj(x, g, mod, w, bb, tt):
    b, t, _ = x.shape
    tn = 512
    return pl.pallas_call(
        _inproj_kernel,
        grid=(b // bb, t // tt, U_COLS // tn),
        in_specs=[
            pl.BlockSpec((bb, tt, D_MODEL), lambda i, j, n: (i, j, 0)),
            pl.BlockSpec((1, 1, D_MODEL), lambda i, j, n: (0, 0, 0)),
            pl.BlockSpec((bb, 1, 1, D_MODEL), lambda i, j, n: (i, 1, 0, 0)),
            pl.BlockSpec((bb, 1, 1, D_MODEL), lambda i, j, n: (i, 0, 0, 0)),
            pl.BlockSpec((D_MODEL, tn), lambda i, j, n: (0, n)),
        ],
        out_specs=[
            pl.BlockSpec((bb, tt, tn), lambda i, j, n: (i, j, n)),
            pl.BlockSpec((bb, tt, D_MODEL), lambda i, j, n: (i, j, 0)),
        ],
        out_shape=[
            jax.ShapeDtypeStruct((b, t, U_COLS), F32),
            jax.ShapeDtypeStruct((b, t, D_MODEL), BF16),
        ],
        compiler_params=_cparams(("parallel", "parallel", "arbitrary")),
        name="inproj",
    )(x, g.reshape(1, 1, D_MODEL), mod, mod, w)


def _merge_kernel(h_ref, ya_ref, yb_ref, yc_ref, yd_ref, wg_ref, wb_ref, o_ref):
    bb, tt, _ = h_ref.shape
    m = bb * tt
    h = h_ref[...].reshape(m, D_MODEL)
    acc = None
    for i, y_ref in enumerate((ya_ref, yb_ref, yc_ref, yd_ref)):
        gate = jax.nn.sigmoid(jnp.dot(h, wg_ref[i], preferred_element_type=F32))
        br = jnp.dot(y_ref[...].reshape(m, BW), wb_ref[i], preferred_element_type=F32)
        acc = gate * br if acc is None else acc + gate * br
    o_ref[...] = _bf(acc).reshape(o_ref.shape)


def _merge(h, ys, wg, wb, bb, tt):
    b, t, _ = h.shape
    tn = 256
    yspec = pl.BlockSpec((bb, tt, BW), lambda i, j, n: (i, j, 0))
    return pl.pallas_call(
        _merge_kernel,
        grid=(b // bb, t // tt, D_MODEL // tn),
        in_specs=[
            pl.BlockSpec((bb, tt, D_MODEL), lambda i, j, n: (i, j, 0)),
            yspec, yspec, yspec, yspec,
            pl.BlockSpec((4, D_MODEL, tn), lambda i, j, n: (0, 0, n)),
            pl.BlockSpec((4, BW, tn), lambda i, j, n: (0, 0, n)),
        ],
        out_specs=pl.BlockSpec((bb, tt, tn), lambda i, j, n: (i, j, n)),
        out_shape=jax.ShapeDtypeStruct((b, t, D_MODEL), BF16),
        compiler_params=_cparams(("parallel", "parallel", "arbitrary")),
        name="merge",
    )(h, *ys, wg, wb)


def _outproj_kernel(m_ref, w_ref, x_ref, g_ref, o_ref):
    bb, tt, _ = m_ref.shape
    y = jnp.dot(m_ref[...].reshape(bb * tt, D_MODEL), w_ref[...], preferred_element_type=F32)
    o_ref[...] = x_ref[...] + g_ref[:, 0] * y.reshape(o_ref.shape)


def _outproj(merged, w, x, mod, bb, tt):
    b, t, _ = x.shape
    tn = 1024
    return pl.pallas_call(
        _outproj_kernel,
        grid=(b // bb, t // tt, D_MODEL // tn),
        in_specs=[
            pl.BlockSpec((bb, tt, D_MODEL), lambda i, j, n: (i, j, 0)),
            pl.BlockSpec((D_MODEL, tn), lambda i, j, n: (0, n)),
            pl.BlockSpec((bb, tt, tn), lambda i, j, n: (i, j, n)),
            pl.BlockSpec((bb, 1, 1, tn), lambda i, j, n: (i, 2, 0, n)),
        ],
        out_specs=pl.BlockSpec((bb, tt, tn), lambda i, j, n: (i, j, n)),
        out_shape=jax.ShapeDtypeStruct((b, t, D_MODEL), F32),
        compiler_params=_cparams(("parallel", "parallel", "arbitrary")),
        name="outproj",
    )(merged, w, x, mod)


def _ffn_kernel(x_ref, g_ref, sc_ref, sh_ref, gate_ref, wu_ref, wd_ref, o_ref, h_sc):
    bb, tt, _ = x_ref.shape
    m = bb * tt
    f = pl.program_id(2)

    @pl.when(f == 0)
    def _():
        h = _norm_mod(x_ref[...], g_ref[...], sc_ref[:, 0], sh_ref[:, 0])
        h_sc[...] = _bf(h).reshape(m, D_MODEL)

    a = jnp.dot(h_sc[...], wu_ref[...], preferred_element_type=F32)
    a = jnp.square(jnp.maximum(a, 0.0))
    part = jnp.dot(_bf(a), wd_ref[...], preferred_element_type=F32).reshape(o_ref.shape)

    @pl.when(f == 0)
    def _():
        o_ref[...] = part

    @pl.when(f > 0)
    def _():
        o_ref[...] += part

    @pl.when(f == pl.num_programs(2) - 1)
    def _():
        o_ref[...] = x_ref[...] + gate_ref[:, 0] * o_ref[...]


def _ffn(x, g, mod, wu, wd, bb, tt):
    b, t, _ = x.shape
    tf = 1024
    if bb * tt > 512:
        tt = 512 // bb
    return pl.pallas_call(
        _ffn_kernel,
        grid=(b // bb, t // tt, D_FF // tf),
        in_specs=[
            pl.BlockSpec((bb, tt, D_MODEL), lambda i, j, f: (i, j, 0)),
            pl.BlockSpec((1, 1, D_MODEL), lambda i, j, f: (0, 0, 0)),
            pl.BlockSpec((bb, 1, 1, D_MODEL), lambda i, j, f: (i, 4, 0, 0)),
            pl.BlockSpec((bb, 1, 1, D_MODEL), lambda i, j, f: (i, 3, 0, 0)),
            pl.BlockSpec((bb, 1, 1, D_MODEL), lambda i, j, f: (i, 5, 0, 0)),
            pl.BlockSpec((D_MODEL, tf), lambda i, j, f: (0, f)),
            pl.BlockSpec((tf, D_MODEL), lambda i, j, f: (f, 0)),
        ],
        out_specs=pl.BlockSpec((bb, tt, D_MODEL), lambda i, j, f: (i, j, 0)),
        out_shape=jax.ShapeDtypeStruct((b, t, D_MODEL), F32),
        scratch_shapes=[pltpu.VMEM((bb * tt, D_MODEL), BF16)],
        compiler_params=_cparams(("parallel", "parallel", "arbitrary")),
        name="ffn",
    )(x, g.reshape(1, 1, D_MODEL), mod, mod, mod, wu, wd)


def _final_norm_kernel(x_ref, g_ref, o_ref):
    x = x_ref[...]
    o_ref[...] = (x * lax.rsqrt(jnp.mean(x * x, axis=-1, keepdims=True) + NORM_EPS)) * g_ref[...]


def _final_norm(x, g, bb, tt):
    b, t, _ = x.shape
    spec = pl.BlockSpec((bb, tt, D_MODEL), lambda i, j: (i, j, 0))
    return pl.pallas_call(
        _final_norm_kernel,
        grid=(b // bb, t // tt),
        in_specs=[spec, pl.BlockSpec((1, 1, D_MODEL), lambda i, j: (0, 0, 0))],
        out_specs=spec,
        out_shape=jax.ShapeDtypeStruct((b, t, D_MODEL), F32),
        compiler_params=_cparams(("parallel", "parallel")),
        name="final_norm",
    )(x, g.reshape(1, 1, D_MODEL))


NPAIR = N_HEADS // 2


def _rwkv_kernel(rkv_ref, lora_ref, sh_rkv_ref, sh_lora_ref, s0_ref, mu_rkv_ref, mu_lora_ref,
                 w0_ref, w2_ref, a0_ref, a2_ref, g2_ref, kkp_ref, ka_ref, rk_ref, lnw_ref, lnb_ref,
                 y_ref, sfin_ref,
                 prev_rkv, prev_lora, s_sc, g_sc, gend_sc, bonus_sc, yo_sc,
                 rt_sc, pt_sc, qt_sc, kt_sc, qe_sc, ke_sc, v_sc, lhs_sc, z_sc, arqk_sc):
    tt = rkv_ref.shape[1]
    nc = tt // CHUNK
    i = pl.program_id(1)

    @pl.when(i == 0)
    def _():
        prev_rkv[...] = sh_rkv_ref[0]
        prev_lora[...] = sh_lora_ref[0]
        for hp in range(NPAIR):
            s_sc[hp] = jnp.concatenate([s0_ref[0, 2 * hp], s0_ref[0, 2 * hp + 1]], axis=1)

    rid = lax.broadcasted_iota(jnp.int32, (tt, 1), 0)

    def tshift(x, prev_row, mu):
        prev = jnp.where(rid == 0, prev_row, pltpu.roll(x, 1, 0))
        return x + (prev - x) * mu

    u = rkv_ref[0]
    ul = lora_ref[0]
    xs = tshift(u, prev_rkv[...], mu_rkv_ref[...])
    xl = tshift(ul, prev_lora[...], mu_lora_ref[...])
    prev_rkv[...] = u[tt - 1:tt, :]
    prev_lora[...] = ul[tt - 1:tt, :]

    r = xs[:, 0:BW]
    k = xs[:, BW:2 * BW]
    v = xs[:, 2 * BW:3 * BW]
    wd = xl[:, 0:64]
    ad = xl[:, 64:128]
    gd = xl[:, 128:256]
    lw = -math.exp(-0.5) * jax.nn.sigmoid(w0_ref[...] + _mm(jnp.tanh(wd), w2_ref[...]))
    a = jax.nn.sigmoid(a0_ref[...] + _mm(ad, a2_ref[...]))
    g_sc[...] = _mm(jax.nn.sigmoid(gd), g2_ref[...])
    kkr = k * kkp_ref[...]
    kh = k * (1.0 + (a - 1.0) * ka_ref[...])

    lane = lax.broadcasted_iota(jnp.int32, (CHUNK, 128), 1)
    row = lax.broadcasted_iota(jnp.int32, (CHUNK, 128), 0)
    col_in_head = lane & (HEAD_DIM - 1)
    tri_strict2 = row > col_in_head
    tri_incl2 = row >= col_in_head
    eye2 = (row == col_in_head).astype(F32)
    first_head = lane < HEAD_DIM
    r128 = lax.broadcasted_iota(jnp.int32, (128, 128), 0)
    c128 = lax.broadcasted_iota(jnp.int32, (128, 128), 1)
    same_head = (r128 < HEAD_DIM) == (c128 < HEAD_DIM)
    ones_bd = same_head.astype(BF16)
    tri_incl_bf = _tri(CHUNK).astype(BF16)

    def head_sum(x, terms):
        acc = None
        for part in _split(x, terms):
            d = jnp.dot(part, ones_bd, preferred_element_type=F32)
            acc = d if acc is None else acc + d
        return acc

    cl = jnp.concatenate(
        [_mm_sel(tri_incl_bf, lw[c * CHUNK:(c + 1) * CHUNK, :], terms=2) for c in range(nc)], axis=0)
    cl_last = jnp.concatenate(
        [jnp.broadcast_to(cl[(c + 1) * CHUNK - 1:(c + 1) * CHUNK, :], (CHUNK, BW)) for c in range(nc)], axis=0)
    e_in = jnp.exp(cl)
    e_ex = jnp.exp(cl - lw)
    e_inv = jnp.exp(-cl)
    e_end = jnp.exp(cl_last - cl)
    gend_sc[...] = jnp.exp(jnp.concatenate(
        [jnp.broadcast_to(cl[(c + 1) * CHUNK - 1:(c + 1) * CHUNK, :], (8, BW)) for c in range(nc)], axis=0))
    for hp in range(NPAIR):
        sl = slice(hp * 128, (hp + 1) * 128)
        kk = kkr[:, sl]
        kk = kk * lax.rsqrt(jnp.maximum(head_sum(kk * kk, 2), 1e-24))
        q = kk * a[:, sl]
        rt_sc[hp] = r[:, sl] * e_in[:, sl]
        pt_sc[hp] = -kk * e_ex[:, sl]
        qt_sc[hp] = q * e_inv[:, sl]
        kt_sc[hp] = kh[:, sl] * e_inv[:, sl]
        qe_sc[hp] = q * e_end[:, sl]
        ke_sc[hp] = kh[:, sl] * e_end[:, sl]
        v_sc[hp] = v[:, sl]
        bonus_sc[:, sl] = head_sum(r[:, sl] * kh[:, sl] * rk_ref[:, sl], 1) * v[:, sl]

    def bd(x):
        return jnp.where(same_head, jnp.concatenate([x, x], axis=0), jnp.zeros((), x.dtype))

    def parts(x):
        hi = _bf(x)
        return hi, _bf(x - hi.astype(F32))

    def bd2(p):
        return bd(p[0]), bd(p[1])

    nn, nt, tn = ((1,), (0,)), ((1,), (1,)), ((0,), (0,))

    def dot3_shared(a_list, b, dims):
        d = lambda x, y: lax.dot_general(x, y, (dims, ((), ())), preferred_element_type=F32)
        ax = 1 if dims == tn else 0
        m = a_list[0][0].shape[ax]
        big = d(jnp.concatenate([t for a in a_list for t in a], axis=ax), b[0])
        small = d(jnp.concatenate([a[0] for a in a_list], axis=ax), b[1])
        return [big[2 * i * m:(2 * i + 1) * m] + big[(2 * i + 1) * m:(2 * i + 2) * m] + small[i * m:(i + 1) * m]
                for i in range(len(a_list))]

    def dot3(a, b, dims):
        return dot3_shared([a], b, dims)[0]

    pairs = range(NPAIR)

    cpi = 2 if nc % 2 == 0 else 1

    def intra_body(n, carry):
        units = [(n * cpi + j, hp) for j in range(cpi) for hp in pairs]
        rows = [pl.ds(pl.multiple_of(c * CHUNK, CHUNK), CHUNK) for c, _ in units]
        un = range(len(units))
        pt = [pt_sc[units[k][1], rows[k], :] for k in un]
        rt = [rt_sc[units[k][1], rows[k], :] for k in un]
        lhs = [parts(jnp.concatenate([pt[k], rt[k]], axis=0)) for k in un]
        gq = [dot3(lhs[k], bd2(parts(qt_sc[units[k][1], rows[k], :])), nt) for k in un]
        gk = [dot3(lhs[k], bd2(parts(kt_sc[units[k][1], rows[k], :])), nt) for k in un]
        l_pq = [jnp.where(tri_strict2, gq[k][0:CHUNK], 0.0) for k in un]
        lv = [dot3(parts(jnp.where(tri_strict2, gk[k][0:CHUNK], 0.0)),
                   bd2(parts(v_sc[units[k][1], rows[k], :])), nn) for k in un]
        tinv = [eye2 + l_pq[k] for k in un]
        xpp = [parts(l_pq[k]) for k in un]
        xpp = [parts(dot3(xpp[k], bd2(xpp[k]), nn)) for k in un]
        for it in range(5):
            tp = [parts(tinv[k]) for k in un]
            if it < 4:
                res = [dot3_shared([xpp[k], tp[k]], bd2(xpp[k]), nn) for k in un]
                xpp = [parts(res[k][0]) for k in un]
                tinv = [tinv[k] + res[k][1] for k in un]
            else:
                tinv = [tinv[k] + dot3(tp[k], bd2(xpp[k]), nn) for k in un]
        tp = [parts(tinv[k]) for k in un]
        wmat = [dot3(tp[k], bd2((lhs[k][0][0:CHUNK], lhs[k][1][0:CHUNK])), nn) for k in un]
        for k, (c, hp) in enumerate(units):
            z_sc[c, hp] = dot3(tp[k], bd2(parts(lv[k])), nn)
            for t, part in enumerate(parts(jnp.concatenate([wmat[k], rt[k]], axis=0))):
                lhs_sc[t, c, hp] = part
            arqk_sc[c, hp] = jnp.concatenate(
                [_bf(jnp.where(tri_incl2, gq[k][CHUNK:2 * CHUNK], 0.0)),
                 _bf(jnp.where(tri_incl2, gk[k][CHUNK:2 * CHUNK], 0.0))], axis=1)
        return carry

    def state_body(c, carry):
        rows = pl.ds(pl.multiple_of(c * CHUNK, CHUNK), CHUNK)
        grow = pl.ds(pl.multiple_of(c * 8, 8), 8)
        vv = [parts(v_sc[hp, rows, :]) for hp in pairs]
        s0 = [s_sc[hp] for hp in pairs]
        ps = [dot3((lhs_sc[0, c, hp], lhs_sc[1, c, hp]), bd2(parts(s0[hp])), nt)
              for hp in pairs]
        uu = [parts(ps[hp][0:CHUNK] + z_sc[c, hp]) for hp in pairs]
        for hp in pairs:
            sl = slice(hp * 128, (hp + 1) * 128)
            yo_sc[rows, sl] = ps[hp][CHUNK:2 * CHUNK] + jnp.dot(
                arqk_sc[c, hp], jnp.concatenate([bd(uu[hp][0]), bd(vv[hp][0])], axis=0),
                preferred_element_type=F32)
        for hp in pairs:
            sl = slice(hp * 128, (hp + 1) * 128)
            uv = tuple(jnp.concatenate([uu[hp][t], vv[hp][t]], axis=0) for t in range(2))
            qk = parts(jnp.concatenate([qe_sc[hp, rows, :], ke_sc[hp, rows, :]], axis=0))
            full = dot3(uv, qk, tn)
            g_end = gend_sc[grow, sl][0:1, :]
            s_sc[hp] = s0[hp] * g_end + jnp.where(
                first_head, full[0:HEAD_DIM], full[HEAD_DIM:2 * HEAD_DIM])
        return carry

    lax.fori_loop(0, nc // cpi, intra_body, 0)
    lax.fori_loop(0, nc, state_body, 0)

    for hp in range(NPAIR):
        sl = slice(hp * 128, (hp + 1) * 128)
        yh = yo_sc[:, sl]
        mean = head_sum(yh, 1) * (1.0 / HEAD_DIM)
        d = yh - mean
        var = head_sum(d * d, 1) * (1.0 / HEAD_DIM)
        yn = d * lax.rsqrt(var + RW_GN_EPS) * lnw_ref[:, sl] + lnb_ref[:, sl]
        y_ref[0, :, sl] = _bf((yn + bonus_sc[:, sl]) * g_sc[:, sl])

    @pl.when(i == pl.num_programs(1) - 1)
    def _():
        for hp in range(NPAIR):
            sp = s_sc[hp]
            sfin_ref[0, 2 * hp] = sp[:, 0:HEAD_DIM]
            sfin_ref[0, 2 * hp + 1] = sp[:, HEAD_DIM:2 * HEAD_DIM]


def _rwkv(u, shift_hist, s0, p, tt):
    b, t, _ = u.shape
    row = lambda a: a.reshape(1, -1)
    sh = shift_hist.reshape(b, 1, RW_IN)
    sh_rkv = sh[:, :, :3 * BW]
    sh_lora = sh[:, :, 3 * BW:]
    mu = p["rw_mu"]
    full = lambda shape: pl.BlockSpec(shape, lambda i, j: (0,) * len(shape))
    scr = lambda: pltpu.VMEM((NPAIR, tt, 128), F32)
    wide = lambda rows: pltpu.VMEM((rows, BW), F32)
    return pl.pallas_call(
        _rwkv_kernel,
        grid=(b, t // tt),
        in_specs=[
            pl.BlockSpec((1, tt, 3 * BW), lambda i, j: (i, j, C_RKV // (3 * BW))),
            pl.BlockSpec((1, tt, RW_LORA), lambda i, j: (i, j, C_LORA // RW_LORA)),
            pl.BlockSpec((1, 1, 3 * BW), lambda i, j: (i, 0, 0)),
            pl.BlockSpec((1, 1, RW_LORA), lambda i, j: (i, 0, 0)),
            pl.BlockSpec((1, N_HEADS, HEAD_DIM, HEAD_DIM), lambda i, j: (i, 0, 0, 0)),
            full((1, 3 * BW)), full((1, RW_LORA)),
            full((1, BW)), full((64, BW)), full((1, BW)), full((64, BW)), full((128, BW)),
            full((1, BW)), full((1, BW)),
            full((1, BW)), full((1, BW)), full((1, BW)),
        ],
        out_specs=[
            pl.BlockSpec((1, tt, BW), lambda i, j: (i, j, 0)),
            pl.BlockSpec((1, N_HEADS, HEAD_DIM, HEAD_DIM), lambda i, j: (i, 0, 0, 0)),
        ],
        out_shape=[
            jax.ShapeDtypeStruct((b, t, BW), BF16),
            jax.ShapeDtypeStruct((b, N_HEADS, HEAD_DIM, HEAD_DIM), F32),
        ],
        scratch_shapes=[
            pltpu.VMEM((1, 3 * BW), F32), pltpu.VMEM((1, RW_LORA), F32),
            pltpu.VMEM((NPAIR, HEAD_DIM, 128), F32),
            wide(tt), wide(8 * (tt // CHUNK)), wide(tt), wide(tt),
            scr(), scr(), scr(), scr(), scr(), scr(), scr(),
            pltpu.VMEM((2, tt // CHUNK, NPAIR, 2 * CHUNK, 128), BF16),
            pltpu.VMEM((tt // CHUNK, NPAIR, CHUNK, 128), F32),
            pltpu.VMEM((tt // CHUNK, NPAIR, CHUNK, 256), BF16),
        ],
        compiler_params=_cparams(("parallel", "arbitrary")),
        name="rwkv7",
    )(u, u, sh_rkv, sh_lora, s0, row(mu[:3 * BW]), row(mu[3 * BW:]),
      row(p["rw_w0"]), p["rw_w2"], row(p["rw_a0"]), p["rw_a2"], p["rw_g2"],
      row(p["rw_kk"]), row(p["rw_ka"]), row(p["rw_rk"]), row(p["rw_ln_w"]), row(p["rw_ln_b"]))


def _pool_kernel(pos0, u_ref, prev_ref, hist_ref, w_ref, scale_ref, y_ref):
    tt = u_ref.shape[1]
    i = pl.program_id(1)
    x = u_ref[0]
    prev = jnp.where(i == 0, hist_ref[0], prev_ref[0])
    s = jnp.concatenate([prev, x], axis=0)
    pos = pos0 + i * tt + lax.broadcasted_iota(jnp.int32, (tt, 1), 0)
    outs = []
    for gi, win in enumerate(POOL_WINDOWS):
        s = s + pltpu.roll(s, win // 2, 0)
        sl = slice(gi * POOL_GW, (gi + 1) * POOL_GW)
        cnt = jnp.minimum(pos + 1, win).astype(F32)
        pg = s[16:, sl] / cnt - x[:, sl]
        outs.append(_mm(pg, w_ref[gi]))
    y_ref[0] = _bf(jnp.concatenate(outs, axis=1) * scale_ref[...])


def _pool(u, hist16, pos0, w_pool, scale, tt):
    b, t, _ = u.shape
    nprev = tt // 16
    return pl.pallas_call(
        functools.partial(_pool_kernel, pos0),
        grid=(b, t // tt),
        in_specs=[
            pl.BlockSpec((1, tt, BW), lambda i, j: (i, j, C_POOL // BW)),
            pl.BlockSpec((1, 16, BW), lambda i, j: (i, jnp.maximum(j * nprev - 1, 0), C_POOL // BW)),
            pl.BlockSpec((1, 16, BW), lambda i, j: (i, 0, 0)),
            pl.BlockSpec((4, POOL_GW, POOL_GW), lambda i, j: (0, 0, 0)),
            pl.BlockSpec((1, BW), lambda i, j: (0, 0)),
        ],
        out_specs=pl.BlockSpec((1, tt, BW), lambda i, j: (i, j, 0)),
        out_shape=jax.ShapeDtypeStruct((b, t, BW), BF16),
        compiler_params=_cparams(("parallel", "parallel")),
        name="pool",
    )(u, u, hist16, w_pool, scale.reshape(1, BW))


def _swa_kernel(mask_history, q_ref, kp2_ref, kp1_ref, kc_ref, vp2_ref, vp1_ref, vc_ref, sink_ref, y_ref):
    tq = q_ref.shape[1]
    nc = tq // CHUNK
    i = pl.program_id(1)
    q = q_ref[0]
    k_all = jnp.concatenate([kp2_ref[0], kp1_ref[0], kc_ref[0]], axis=0)
    v_all = jnp.concatenate([vp2_ref[0], vp1_ref[0], vc_ref[0]], axis=0)
    nk = 3 * CHUNK
    qi = lax.broadcasted_iota(jnp.int32, (CHUNK, nk), 0)
    si = lax.broadcasted_iota(jnp.int32, (CHUNK, nk), 1)
    dist1 = jnp.abs(qi + 2 * CHUNK - si).astype(F32)
    dist = jnp.concatenate([dist1] * SWA_GROUP, axis=0)
    scol = lax.broadcasted_iota(jnp.int32, (SWA_GROUP * CHUNK, nk), 1)
    units = [(c, g) for c in range(nc) for g in range(SWA_KV)]
    bias = []
    sinks = []
    for g in range(SWA_KV):
        heads = [g * SWA_GROUP + hh for hh in range(SWA_GROUP)]
        slope = jnp.concatenate(
            [jnp.full((CHUNK, 1), 2.0 ** (-8.0 * (h + 1) / N_HEADS), F32) for h in heads], axis=0)
        bias.append(slope * dist)
        sinks.append(jnp.concatenate([jnp.full((CHUNK, 1), sink_ref[h], F32) for h in heads], axis=0))

    def scores(c, g):
        ks = k_all[c * CHUNK:c * CHUNK + nk, g * HEAD_DIM:(g + 1) * HEAD_DIM]
        qs = jnp.concatenate(
            [q[c * CHUNK:(c + 1) * CHUNK, (g * SWA_GROUP + hh) * HEAD_DIM:(g * SWA_GROUP + hh + 1) * HEAD_DIM]
             for hh in range(SWA_GROUP)], axis=0)
        return _mm_nt(qs, ks)

    def probs(s, c, g):
        s = s * (HEAD_DIM ** -0.5) - bias[g]
        if mask_history:
            s = jnp.where(scol >= (2 - (i * nc + c)) * CHUNK, s, -1e30)
        m = jnp.maximum(jnp.max(s, axis=-1, keepdims=True), sinks[g])
        p = jnp.exp(s - m)
        denom = jnp.sum(p, axis=-1, keepdims=True) + jnp.exp(sinks[g] - m)
        return _bf(p * (1.0 / denom))

    s_all = [scores(c, g) for c, g in units]
    p_all = [probs(s, c, g) for s, (c, g) in zip(s_all, units)]
    o_all = [_mm(p, v_all[c * CHUNK:c * CHUNK + nk, g * HEAD_DIM:(g + 1) * HEAD_DIM])
             for p, (c, g) in zip(p_all, units)]
    for o, (c, g) in zip(o_all, units):
        for hh in range(SWA_GROUP):
            h = g * SWA_GROUP + hh
            y_ref[0, c * CHUNK:(c + 1) * CHUNK, h * HEAD_DIM:(h + 1) * HEAD_DIM] = _bf(
                o[hh * CHUNK:(hh + 1) * CHUNK])


def _swa(u, hist_k, hist_v, sinks, tq):
    b, t, _ = u.shape
    per = tq // CHUNK
    kvw = SWA_KV * HEAD_DIM
    mask_history = hist_k is None
    cur_k = pl.BlockSpec((1, tq, kvw), lambda i, j: (i, j, C_KS // kvw))
    cur_v = pl.BlockSpec((1, tq, kvw), lambda i, j: (i, j, C_VS // kvw))
    if mask_history:
        prev = lambda d, col: pl.BlockSpec(
            (1, CHUNK, kvw), lambda i, j: (i, jnp.maximum(j * per - d, 0), col))
        specs = [prev(2, C_KS // kvw), prev(1, C_KS // kvw), cur_k,
                 prev(2, C_VS // kvw), prev(1, C_VS // kvw), cur_v]
        args = (u, u, u, u, u, u)
    else:
        assert t == tq
        hk = hist_k.reshape(b, WINDOW, kvw)
        hv = hist_v.reshape(b, WINDOW, kvw)
        hist = lambda blk: pl.BlockSpec((1, CHUNK, kvw), lambda i, j: (i, blk, 0))
        specs = [hist(0), hist(1), cur_k, hist(0), hist(1), cur_v]
        args = (hk, hk, u, hv, hv, u)
    return pl.pallas_call(
        functools.partial(_swa_kernel, mask_history),
        grid=(b, t // tq),
        in_specs=[pl.BlockSpec((1, tq, BW), lambda i, j: (i, j, C_Q // BW))] + specs
        + [pl.BlockSpec(memory_space=pltpu.SMEM)],
        out_specs=pl.BlockSpec((1, tq, BW), lambda i, j: (i, j, 0)),
        out_shape=jax.ShapeDtypeStruct((b, t, BW), BF16),
        compiler_params=_cparams(("parallel", "parallel")),
        name="swa",
    )(u, *args, sinks)


def _ssd_kernel(z_ref, x_ref, b_ref, c_ref, dt_ref, hx_ref, hb_ref, hc_ref, s0_ref,
                cwx_ref, cwb_ref, cwc_ref, cbx_ref, cbb_ref, cbc_ref,
                dtb_ref, alog_ref, dsk_ref, nw_ref,
                y_ref, sfin_ref,
                px_sc, pb_sc, pc_sc, s_sc, xa_sc, ba_sc, ca_sc, acs_sc, gend_sc,
                acsx_sc, eout_sc, y_sc, xdt_sc, xde_sc):
    tt = x_ref.shape[1]
    nc = tt // CHUNK
    i = pl.program_id(1)

    @pl.when(i == 0)
    def _():
        px_sc[...] = hx_ref[0]
        pb_sc[...] = hb_ref[0]
        pc_sc[...] = hc_ref[0]
        s_sc[...] = s0_ref[0].reshape(N_HEADS * HEAD_DIM, SSM_STATE)

    rid8 = lax.broadcasted_iota(jnp.int32, (8, 1), 0)

    def conv_silu(x, prev8, w_ref, b_ref_):
        acc = None
        for wi in range(SSM_CONV):
            sh = SSM_CONV - 1 - wi
            if sh == 0:
                xs = x
            else:
                rolled = pltpu.roll(x, sh, 0)
                top = jnp.where(rid8 < sh, pltpu.roll(prev8, sh, 0), rolled[0:8])
                xs = jnp.concatenate([top, rolled[8:]], axis=0)
            term = xs * w_ref[wi:wi + 1, :]
            acc = (b_ref_[...] + term) if acc is None else acc + term
        return _silu(acc)

    xr = x_ref[0]
    br = b_ref[0]
    cr = c_ref[0]
    xa_sc[...] = conv_silu(xr, px_sc[...], cwx_ref, cbx_ref)
    ba_sc[...] = conv_silu(br, pb_sc[...], cwb_ref, cbb_ref)
    ca_sc[...] = conv_silu(cr, pc_sc[...], cwc_ref, cbc_ref)
    px_sc[...] = xr[tt - 8:tt, :]
    pb_sc[...] = br[tt - 8:tt, :]
    pc_sc[...] = cr[tt - 8:tt, :]
    hl = 128
    dt = _softplus(dt_ref[0][:, 0:hl] + dtb_ref[...])
    ad = -jnp.exp(alog_ref[...]) * dt

    tri_incl_bf = _tri(CHUNK).astype(BF16)
    sel16 = (lax.broadcasted_iota(jnp.int32, (N_HEADS, hl), 0)
             == lax.broadcasted_iota(jnp.int32, (N_HEADS, hl), 1)).astype(BF16)
    spread = (lax.broadcasted_iota(jnp.int32, (hl, BW), 0)
              == lax.broadcasted_iota(jnp.int32, (hl, BW), 1) // HEAD_DIM).astype(BF16)

    def per_channel(x, terms):
        acc = None
        for part in _split(x, terms):
            d = jnp.dot(part, spread, preferred_element_type=F32)
            acc = d if acc is None else acc + d
        return acc

    acs = jnp.concatenate(
        [_mm_sel(tri_incl_bf, ad[c * CHUNK:(c + 1) * CHUNK, :]) for c in range(nc)], axis=0)
    a_last = jnp.concatenate(
        [jnp.broadcast_to(acs[(c + 1) * CHUNK - 1:(c + 1) * CHUNK, :], (CHUNK, hl)) for c in range(nc)], axis=0)
    acs_sc[...] = acs
    acsx_sc[...] = per_channel(acs, 3)
    eout_sc[...] = per_channel(jnp.exp(acs), 2)
    gend_sc[...] = jnp.exp(jnp.concatenate(
        [jnp.broadcast_to(acs[(c + 1) * CHUNK - 1:(c + 1) * CHUNK, :], (8, hl)) for c in range(nc)], axis=0))
    xdt = xa_sc[...] * per_channel(dt, 2)
    xdt_sc[...] = _bf(xdt)
    xde_sc[...] = _bf(xdt * per_channel(jnp.exp(a_last - acs), 2))

    lane = lax.broadcasted_iota(jnp.int32, (CHUNK, 128), 1)
    row = lax.broadcasted_iota(jnp.int32, (CHUNK, 128), 0)
    tri_incl2 = row >= (lane & (HEAD_DIM - 1))
    r128 = lax.broadcasted_iota(jnp.int32, (128, 128), 0)
    c128 = lax.broadcasted_iota(jnp.int32, (128, 128), 1)
    same_head = (r128 < HEAD_DIM) == (c128 < HEAD_DIM)
    hpg = N_HEADS // SSM_GROUPS
    rpg = hpg * HEAD_DIM

    def chunk_body(c, carry):
        rows = pl.ds(pl.multiple_of(c * CHUNK, CHUNK), CHUNK)
        grow = pl.ds(pl.multiple_of(c * 8, 8), 8)
        bc = _bf(ba_sc[rows, :])
        cc = _bf(ca_sc[rows, :])
        acs_t = _mm_sel_nt(sel16, acs_sc[rows, :])
        g_end = gend_sc[grow, :][0:1, :]
        gsl = [slice(gi * SSM_STATE, (gi + 1) * SSM_STATE) for gi in range(SSM_GROUPS)]
        scores = [_mm_nt(cc[:, gs], bc[:, gs]) for gs in gsl]
        scores2 = [jnp.concatenate([s, s], axis=1) for s in scores]
        y_off = [_mm_nt(cc[:, gsl[gi]], s_sc[gi * rpg:(gi + 1) * rpg, :]) for gi in range(SSM_GROUPS)]
        upd = [_mm_tn(xde_sc[rows, gi * rpg:(gi + 1) * rpg], bc[:, gsl[gi]]) for gi in range(SSM_GROUPS)]
        y_diag = []
        for hp in range(N_HEADS // 2):
            sl = slice(hp * 128, (hp + 1) * 128)
            seg = acsx_sc[rows, sl] - jnp.concatenate([acs_t[2 * hp:2 * hp + 1, :],
                                                       acs_t[2 * hp + 1:2 * hp + 2, :]], axis=1)
            m = scores2[(2 * hp) // hpg] * jnp.exp(jnp.where(tri_incl2, seg, -1e30))
            xd = xdt_sc[rows, sl]
            y_diag.append(_mm(m, jnp.where(same_head, jnp.concatenate([xd, xd], axis=0),
                                           jnp.zeros((), BF16))))
        for h in range(N_HEADS):
            gi, hl_ = divmod(h, hpg)
            hrows = slice(h * HEAD_DIM, (h + 1) * HEAD_DIM)
            s_sc[hrows, :] = (s_sc[hrows, :] * g_end[:, h:h + 1]
                              + upd[gi][hl_ * HEAD_DIM:(hl_ + 1) * HEAD_DIM, :])
        y_sc[rows, :] = (jnp.concatenate(y_diag, axis=1)
                         + jnp.concatenate(y_off, axis=1) * eout_sc[rows, :])
        return carry

    lax.fori_loop(0, nc, chunk_body, 0)

    y = (y_sc[...] + xa_sc[...] * dsk_ref[...]) * _silu(z_ref[0])
    gw = BW // SSM_GROUPS
    parts = []
    for gi in range(SSM_GROUPS):
        yg = y[:, gi * gw:(gi + 1) * gw]
        parts.append(yg * lax.rsqrt(jnp.mean(yg * yg, axis=-1, keepdims=True) + NORM_EPS))
    y_ref[0] = _bf(jnp.concatenate(parts, axis=1) * nw_ref[...])

    @pl.when(i == pl.num_programs(1) - 1)
    def _():
        sfin_ref[0] = s_sc[...].reshape(N_HEADS, HEAD_DIM, SSM_STATE)


def _pad_rows8(a):
    return jnp.pad(a, ((0, 0), (8 - a.shape[1], 0), (0, 0)))


def _pad_lanes(a, n):
    return jnp.pad(a.reshape(1, -1), ((0, 0), (0, n - a.shape[-1])))


def _ssd(u, conv_hist, s0, p, tt):
    b, t, _ = u.shape
    h8 = _pad_rows8(conv_hist)
    hx, hb, hc = h8[:, :, :BW], h8[:, :, BW:BW + SSM_BC], h8[:, :, BW + SSM_BC:]
    cw, cb = p["ssm_conv_w"], p["ssm_conv_b"].reshape(1, -1)
    full = lambda shape: pl.BlockSpec(shape, lambda i, j: (0,) * len(shape))
    col = lambda w, c0: pl.BlockSpec((1, tt, w), lambda i, j: (i, j, c0 // w))
    hist = lambda w: pl.BlockSpec((1, 8, w), lambda i, j: (i, 0, 0))
    return pl.pallas_call(
        _ssd_kernel,
        grid=(b, t // tt),
        in_specs=[
            col(BW, C_Z), col(BW, C_X), col(SSM_BC, C_B), col(SSM_BC, C_C), col(SSM_BC, C_DT),
            hist(BW), hist(SSM_BC), hist(SSM_BC),
            pl.BlockSpec((1, N_HEADS, HEAD_DIM, SSM_STATE), lambda i, j: (i, 0, 0, 0)),
            full((SSM_CONV, BW)), full((SSM_CONV, SSM_BC)), full((SSM_CONV, SSM_BC)),
            full((1, BW)), full((1, SSM_BC)), full((1, SSM_BC)),
            full((1, 128)), full((1, 128)), full((1, BW)), full((1, BW)),
        ],
        out_specs=[
            pl.BlockSpec((1, tt, BW), lambda i, j: (i, j, 0)),
            pl.BlockSpec((1, N_HEADS, HEAD_DIM, SSM_STATE), lambda i, j: (i, 0, 0, 0)),
        ],
        out_shape=[
            jax.ShapeDtypeStruct((b, t, BW), BF16),
            jax.ShapeDtypeStruct((b, N_HEADS, HEAD_DIM, SSM_STATE), F32),
        ],
        scratch_shapes=[
            pltpu.VMEM((8, BW), F32), pltpu.VMEM((8, SSM_BC), F32), pltpu.VMEM((8, SSM_BC), F32),
            pltpu.VMEM((N_HEADS * HEAD_DIM, SSM_STATE), F32),
            pltpu.VMEM((tt, BW), F32), pltpu.VMEM((tt, SSM_BC), F32), pltpu.VMEM((tt, SSM_BC), F32),
            pltpu.VMEM((tt, 128), F32), pltpu.VMEM((8 * (tt // CHUNK), 128), F32),
            pltpu.VMEM((tt, BW), F32), pltpu.VMEM((tt, BW), F32), pltpu.VMEM((tt, BW), F32),
            pltpu.VMEM((tt, BW), BF16), pltpu.VMEM((tt, BW), BF16),
        ],
        compiler_params=_cparams(("parallel", "arbitrary")),
        name="ssd",
    )(u, u, u, u, u, hx, hb, hc, s0,
      cw[:, :BW], cw[:, BW:BW + SSM_BC], cw[:, BW + SSM_BC:],
      cb[:, :BW], cb[:, BW:BW + SSM_BC], cb[:, BW + SSM_BC:],
      _pad_lanes(p["ssm_dt_bias"], 128), _pad_lanes(p["ssm_a_log"], 128),
      jnp.repeat(p["ssm_d"], HEAD_DIM).reshape(1, BW), p["ssm_norm"].reshape(1, BW))


def _prep_w_in(w_in):
    c = lambda a, n: w_in[:, :, a:a + n]
    ssm = _N_SSM
    parts = [
        c(0, 3 * BW),
        c(ssm, BW),
        c(_N_POOL, BW),
        c(_N_SWA, BW),
        c(ssm + BW, BW),
        c(3 * BW, RW_LORA),
        c(_N_SWA + BW, 2 * SWA_KV * HEAD_DIM),
        c(ssm + 2 * BW, 2 * SSM_BC + N_HEADS),
    ]
    w = jnp.concatenate(parts, axis=-1).astype(BF16)
    return jnp.pad(w, ((0, 0), (0, 0), (0, U_COLS - w.shape[-1])))


def _run_group(x, mods, st, p, final_norm, wts, pos0, bb, tt, tt_branch):
    b, t, _ = x.shape
    kvw = SWA_KV * HEAD_DIM
    outs = {k: [] for k in ("rwkv", "shift", "pool", "k", "v", "ssm", "conv")}
    for l in range(DEPTH):
        pl_ = {k: v[l] for k, v in p.items()}
        mod = mods[l]
        u, h = _inproj(x, pl_["norm_mix"], mod, wts["w_in"][l], bb, tt)
        if st is None:
            shift_hist = jnp.zeros((b, 1, RW_IN), F32)
            s_rwkv = jnp.zeros((b, N_HEADS, HEAD_DIM, HEAD_DIM), F32)
            pool_hist = jnp.zeros((b, POOL_HIST, BW), F32)
            hk = hv = None
            s_ssm = jnp.zeros((b, N_HEADS, HEAD_DIM, SSM_STATE), F32)
            conv_hist = jnp.zeros((b, SSM_CONV - 1, SSM_CONV_DIM), F32)
        else:
            shift_hist, s_rwkv, pool_hist = st["shift"][l], st["rwkv"][l], st["pool"][l]
            hk, hv, s_ssm, conv_hist = st["k"][l], st["v"][l], st["ssm"][l], st["conv"][l]
        y_a, n_rwkv = _rwkv(u, shift_hist, s_rwkv, pl_, min(tt_branch, 256))
        hist16 = jnp.pad(pool_hist, ((0, 0), (1, 0), (0, 0)))
        y_b = _pool(u, hist16, pos0, wts["pool_w"][l], pl_["pool_scale"], tt_branch)
        y_c = _swa(u, hk, hv, pl_["swa_sinks"], min(tt_branch, 256))
        y_d, n_ssm = _ssd(u, conv_hist, s_ssm, pl_, tt_branch)
        merged = _merge(h, (y_a, y_b, y_c, y_d), wts["w_gate"][l], wts["w_branch"][l], bb, tt)
        x = _outproj(merged, wts["w_out"][l], x, mod, bb, tt)
        x = _ffn(x, pl_["norm_ffn"], mod, wts["w_up"][l], wts["w_down"][l], bb, tt)

        outs["rwkv"].append(n_rwkv)
        outs["shift"].append(jnp.concatenate(
            [u[:, t - 1:, C_RKV:C_RKV + 3 * BW], u[:, t - 1:, C_LORA:C_LORA + RW_LORA]], axis=-1))
        outs["pool"].append(u[:, t - POOL_HIST:, C_POOL:C_POOL + BW])
        k_new = u[:, :, C_KS:C_KS + kvw]
        v_new = u[:, :, C_VS:C_VS + kvw]
        if hk is not None:
            k_new = jnp.concatenate([hk.reshape(b, WINDOW, kvw), k_new], axis=1)
            v_new = jnp.concatenate([hv.reshape(b, WINDOW, kvw), v_new], axis=1)
        outs["k"].append(k_new[:, -WINDOW:].reshape(b, WINDOW, SWA_KV, HEAD_DIM))
        outs["v"].append(v_new[:, -WINDOW:].reshape(b, WINDOW, SWA_KV, HEAD_DIM))
        outs["ssm"].append(n_ssm)
        outs["conv"].append(jnp.concatenate(
            [u[:, t - 3:, C_X:C_X + BW], u[:, t - 3:, C_B:C_B + 2 * SSM_BC]], axis=-1))
    y = _final_norm(x, final_norm, bb, tt)
    order = ("rwkv", "shift", "pool", "k", "v", "ssm", "conv")
    return y, tuple(jnp.stack(outs[k]) for k in order)


def kernel(x_prompt, x_sample, state_rwkv, state_rwkv_shift, state_pool, cache_swa_k, cache_swa_v,
           state_ssm, state_ssm_conv, c_prompt, c_sample, ada_w, ada_b, norm_mix, norm_ffn, w_in,
           rw_mu, rw_w0, rw_w2, rw_a0, rw_a2, rw_g2, rw_kk, rw_ka, rw_rk, rw_ln_w, rw_ln_b, pool_w,
           pool_scale, swa_sinks, ssm_conv_w, ssm_conv_b, ssm_dt_bias, ssm_a_log, ssm_d, ssm_norm,
           w_gate, w_branch, w_out, w_up, w_down, final_norm):
    bp, tp, _ = x_prompt.shape
    bs, ts, _ = x_sample.shape
    p = dict(norm_mix=norm_mix, norm_ffn=norm_ffn, rw_mu=rw_mu, rw_w0=rw_w0, rw_w2=rw_w2, rw_a0=rw_a0,
             rw_a2=rw_a2, rw_g2=rw_g2, rw_kk=rw_kk, rw_ka=rw_ka, rw_rk=rw_rk, rw_ln_w=rw_ln_w,
             rw_ln_b=rw_ln_b, pool_scale=pool_scale, swa_sinks=swa_sinks, ssm_conv_w=ssm_conv_w,
             ssm_conv_b=ssm_conv_b, ssm_dt_bias=ssm_dt_bias, ssm_a_log=ssm_a_log, ssm_d=ssm_d,
             ssm_norm=ssm_norm)
    wts = dict(w_in=_prep_w_in(w_in), pool_w=pool_w.astype(BF16), w_gate=w_gate.astype(BF16),
               w_branch=w_branch.astype(BF16), w_out=w_out.astype(BF16), w_up=w_up.astype(BF16),
               w_down=w_down.astype(BF16))

    nb = bp + bs
    nb_pad = -(-nb // 8) * 8
    c_all = jnp.pad(jnp.concatenate([c_prompt, c_sample], axis=0), ((0, nb_pad - nb), (0, 0)))
    mod_all = _adaln(c_all, ada_w, ada_b)
    mods_p = mod_all[:, :bp].reshape(DEPTH, bp, 6, 1, D_MODEL)
    mods_s = mod_all[:, bp:nb].reshape(DEPTH, bs, 6, 1, D_MODEL)

    y_prompt, st_p = _run_group(x_prompt, mods_p, None, p, final_norm, wts, 0,
                                1, min(tp, 1024), min(tp, 512))
    st_s = dict(rwkv=state_rwkv, shift=state_rwkv_shift, pool=state_pool, k=cache_swa_k,
                v=cache_swa_v, ssm=state_ssm, conv=state_ssm_conv)
    y_sample, st_o = _run_group(x_sample, mods_s, st_s, p, final_norm, wts, PAST_LEN, bs, ts, ts)
    return (y_prompt, y_sample) + st_p + st_o
```

```python
import functools
import math

import jax
import jax.numpy as jnp
from jax import lax
from jax.experimental import pallas as pl
from jax.experimental.pallas import tpu as pltpu

F32 = jnp.float32
BF16 = jnp.bfloat16

D_MODEL = 2048
DEPTH = 4
PAST_LEN = 1024
CHUNK = 64
HEAD_DIM = 64
BW = D_MODEL // 2
D_FF = 4 * D_MODEL
NORM_EPS = 1e-6
N_HEADS = BW // HEAD_DIM
RW_LORA = 256
RW_IN = 3 * BW + RW_LORA
RW_GN_EPS = HEAD_DIM * 1e-5
POOL_WINDOWS = (2, 4, 8, 16)
POOL_GW = BW // 4
POOL_HIST = 15
SWA_KV = 4
SWA_GROUP = N_HEADS // SWA_KV
WINDOW = 128
SSM_GROUPS = 2
SSM_STATE = 128
SSM_CONV = 4
SSM_BC = SSM_GROUPS * SSM_STATE
SSM_CONV_DIM = BW + 2 * SSM_BC

C_RKV = 0
C_Z = 3072
C_POOL = 4096
C_Q = 5120
C_X = 6144
C_LORA = 7168
C_KS = 7424
C_VS = 7680
C_B = 7936
C_C = 8192
C_DT = 8448
U_COLS = 8704

_N_POOL = RW_IN
_N_SWA = _N_POOL + BW
_N_SSM = _N_SWA + (N_HEADS + 2 * SWA_KV) * HEAD_DIM
IN_COLS = _N_SSM + BW + SSM_CONV_DIM + N_HEADS

VMEM_LIMIT = 56 * 1024 * 1024


def _cparams(sem):
    return pltpu.CompilerParams(dimension_semantics=sem, vmem_limit_bytes=VMEM_LIMIT)


def _bf(x):
    return x.astype(BF16)


def _mm(a, b):
    return jnp.dot(_bf(a), _bf(b), preferred_element_type=F32)


def _mm_nt(a, b):
    return lax.dot_general(_bf(a), _bf(b), (((1,), (1,)), ((), ())), preferred_element_type=F32)


def _mm_tn(a, b):
    return lax.dot_general(_bf(a), _bf(b), (((0,), (0,)), ((), ())), preferred_element_type=F32)


def _split3(x):
    hi = _bf(x)
    r1 = x - hi.astype(F32)
    mid = _bf(r1)
    lo = _bf(r1 - mid.astype(F32))
    return hi, mid, lo


def _split(x, terms):
    parts = []
    for _ in range(terms - 1):
        hi = _bf(x)
        parts.append(hi)
        x = x - hi.astype(F32)
    parts.append(_bf(x))
    return parts


def _mm_sel(c, x, terms=3):
    acc = None
    for part in _split(x, terms):
        d = jnp.dot(c, part, preferred_element_type=F32)
        acc = d if acc is None else acc + d
    return acc


def _mm_sel_nt(c, x):
    hi, mid, lo = _split3(x)
    d = lambda p: lax.dot_general(c, p, (((1,), (1,)), ((), ())), preferred_element_type=F32)
    return d(hi) + d(mid) + d(lo)


def _softplus(x):
    return jnp.maximum(x, 0.0) + jnp.log1p(jnp.exp(-jnp.abs(x)))


def _silu(x):
    return x * jax.nn.sigmoid(x)


def _tri(n, strict=False):
    r = lax.broadcasted_iota(jnp.int32, (n, n), 0)
    c = lax.broadcasted_iota(jnp.int32, (n, n), 1)
    return (r > c) if strict else (r >= c)


def _adaln_kernel(c_ref, w_ref, b_ref, o_ref):
    s = _silu(c_ref[...])
    o_ref[0] = _mm(s, w_ref[0]) + b_ref[0]


def _adaln(c_all, ada_w, ada_b):
    nb = c_all.shape[0]
    tn = 1024
    n_out = ada_w.shape[2]
    return pl.pallas_call(
        _adaln_kernel,
        grid=(DEPTH, n_out // tn),
        in_specs=[
            pl.BlockSpec((nb, D_MODEL), lambda l, n: (0, 0)),
            pl.BlockSpec((1, D_MODEL, tn), lambda l, n: (l, 0, n)),
            pl.BlockSpec((1, 1, tn), lambda l, n: (l, 0, n)),
        ],
        out_specs=pl.BlockSpec((1, nb, tn), lambda l, n: (l, 0, n)),
        out_shape=jax.ShapeDtypeStruct((DEPTH, nb, n_out), F32),
        compiler_params=_cparams(("parallel", "parallel")),
        name="adaln",
    )(c_all, ada_w, ada_b.reshape(DEPTH, 1, n_out))


def _norm_mod(x, g, sc, sh):
    y = x * lax.rsqrt(jnp.mean(x * x, axis=-1, keepdims=True) + NORM_EPS)
    return (y * g) * (1.0 + sc) + sh


def _inproj_kernel(x_ref, g_ref, sc_ref, sh_ref, w_ref, u_ref, h_ref):
    bb, tt, _ = x_ref.shape

    @pl.when(pl.program_id(2) == 0)
    def _():
        h = _norm_mod(x_ref[...], g_ref[...], sc_ref[:, 0], sh_ref[:, 0])
        h_ref[...] = _bf(h)

    h = h_ref[...].reshape(bb * tt, D_MODEL)
    u_ref[...] = jnp.dot(h, w_ref[...], preferred_element_type=F32).reshape(u_ref.shape)


def _inproj(x, g, mod, w, bb, tt):
    b, t, _ = x.shape
    tn = U_COLS // 4
    if bb * tt > 512:
        tt = 512 // bb
    return pl.pallas_call(
        _inproj_kernel,
        grid=(b // bb, t // tt, U_COLS // tn),
        in_specs=[
            pl.BlockSpec((bb, tt, D_MODEL), lambda i, j, n: (i, j, 0)),
            pl.BlockSpec((1, 1, D_MODEL), lambda i, j, n: (0, 0, 0)),
            pl.BlockSpec((bb, 1, 1, D_MODEL), lambda i, j, n: (i, 1, 0, 0)),
            pl.BlockSpec((bb, 1, 1, D_MODEL), lambda i, j, n: (i, 0, 0, 0)),
            pl.BlockSpec((D_MODEL, tn), lambda i, j, n: (0, n)),
        ],
        out_specs=[
            pl.BlockSpec((bb, tt, tn), lambda i, j, n: (i, j, n)),
            pl.BlockSpec((bb, tt, D_MODEL), lambda i, j, n: (i, j, 0)),
        ],
        out_shape=[
            jax.ShapeDtypeStruct((b, t, U_COLS), F32),
            jax.ShapeDtypeStruct((b, t, D_MODEL), BF16),
        ],
        compiler_params=_cparams(("parallel", "parallel", "arbitrary")),
        name="inproj",
    )(x, g.reshape(1, 1, D_MODEL), mod, mod, w)


def _merge_kernel(h_ref, ya_ref, yb_ref, yc_ref, yd_ref, wg_ref, wb_ref, o_ref):
    bb, tt, _ = h_ref.shape
    m = bb * tt
    h = h_ref[...].reshape(m, D_MODEL)
    acc = None
    for i, y_ref in enumerate((ya_ref, yb_ref, yc_ref, yd_ref)):
        gate = jax.nn.sigmoid(jnp.dot(h, wg_ref[i], preferred_element_type=F32))
        br = jnp.dot(y_ref[...].reshape(m, BW), wb_ref[i], preferred_element_type=F32)
        acc = gate * br if acc is None else acc + gate * br
    o_ref[...] = _bf(acc).reshape(o_ref.shape)


def _merge(h, ys, wg, wb, bb, tt):
    b, t, _ = h.shape
    tn = 256
    yspec = pl.BlockSpec((bb, tt, BW), lambda i, j, n: (i, j, 0))
    return pl.pallas_call(
        _merge_kernel,
        grid=(b // bb, t // tt, D_MODEL // tn),
        in_specs=[
            pl.BlockSpec((bb, tt, D_MODEL), lambda i, j, n: (i, j, 0)),
            yspec, yspec, yspec, yspec,
            pl.BlockSpec((4, D_MODEL, tn), lambda i, j, n: (0, 0, n)),
            pl.BlockSpec((4, BW, tn), lambda i, j, n: (0, 0, n)),
        ],
        out_specs=pl.BlockSpec((bb, tt, tn), lambda i, j, n: (i, j, n)),
        out_shape=jax.ShapeDtypeStruct((b, t, D_MODEL), BF16),
        compiler_params=_cparams(("parallel", "parallel", "arbitrary")),
        name="merge",
    )(h, *ys, wg, wb)


def _outproj_kernel(m_ref, w_ref, x_ref, g_ref, o_ref):
    bb, tt, _ = m_ref.shape
    y = jnp.dot(m_ref[...].reshape(bb * tt, D_MODEL), w_ref[...], preferred_element_type=F32)
    o_ref[...] = x_ref[...] + g_ref[:, 0] * y.reshape(o_ref.shape)


def _outproj(merged, w, x, mod, bb, tt):
    b, t, _ = x.shape
    tn = 1024
    return pl.pallas_call(
        _outproj_kernel,
        grid=(b // bb, t // tt, D_MODEL // tn),
        in_specs=[
            pl.BlockSpec((bb, tt, D_MODEL), lambda i, j, n: (i, j, 0)),
            pl.BlockSpec((D_MODEL, tn), lambda i, j, n: (0, n)),
            pl.BlockSpec((bb, tt, tn), lambda i, j, n: (i, j, n)),
            pl.BlockSpec((bb, 1, 1, tn), lambda i, j, n: (i, 2, 0, n)),
        ],
        out_specs=pl.BlockSpec((bb, tt, tn), lambda i, j, n: (i, j, n)),
        out_shape=jax.ShapeDtypeStruct((b, t, D_MODEL), F32),
        compiler_params=_cparams(("parallel", "parallel", "arbitrary")),
        name="outproj",
    )(merged, w, x, mod)


def _ffn_kernel(final, x_ref, g_ref, sc_ref, sh_ref, gate_ref, wu_ref, wd_ref, fg_ref, o_ref, h_sc):
    bb, tt, _ = x_ref.shape
    m = bb * tt
    f = pl.program_id(2)

    @pl.when(f == 0)
    def _():
        h = _norm_mod(x_ref[...], g_ref[...], sc_ref[:, 0], sh_ref[:, 0])
        h_sc[...] = _bf(h).reshape(m, D_MODEL)
        o_ref[...] = jnp.zeros(o_ref.shape, F32)

    a = jnp.dot(h_sc[...], wu_ref[...], preferred_element_type=F32)
    a = jnp.square(jnp.maximum(a, 0.0))
    o_ref[...] += jnp.dot(_bf(a), wd_ref[...], preferred_element_type=F32).reshape(o_ref.shape)

    @pl.when(f == pl.num_programs(2) - 1)
    def _():
        y = x_ref[...] + gate_ref[:, 0] * o_ref[...]
        if final:
            y = (y * lax.rsqrt(jnp.mean(y * y, axis=-1, keepdims=True) + NORM_EPS)) * fg_ref[...]
        o_ref[...] = y


def _ffn(x, g, mod, wu, wd, final_g, final, bb, tt):
    b, t, _ = x.shape
    tf = 1024
    if bb * tt > 512:
        tt = 512 // bb
    vec = pl.BlockSpec((1, 1, D_MODEL), lambda i, j, f: (0, 0, 0))
    return pl.pallas_call(
        functools.partial(_ffn_kernel, final),
        grid=(b // bb, t // tt, D_FF // tf),
        in_specs=[
            pl.BlockSpec((bb, tt, D_MODEL), lambda i, j, f: (i, j, 0)),
            vec,
            pl.BlockSpec((bb, 1, 1, D_MODEL), lambda i, j, f: (i, 4, 0, 0)),
            pl.BlockSpec((bb, 1, 1, D_MODEL), lambda i, j, f: (i, 3, 0, 0)),
            pl.BlockSpec((bb, 1, 1, D_MODEL), lambda i, j, f: (i, 5, 0, 0)),
            pl.BlockSpec((D_MODEL, tf), lambda i, j, f: (0, f)),
            pl.BlockSpec((tf, D_MODEL), lambda i, j, f: (f, 0)),
            vec,
        ],
        out_specs=pl.BlockSpec((bb, tt, D_MODEL), lambda i, j, f: (i, j, 0)),
        out_shape=jax.ShapeDtypeStruct((b, t, D_MODEL), F32),
        scratch_shapes=[pltpu.VMEM((bb * tt, D_MODEL), BF16)],
        compiler_params=_cparams(("parallel", "parallel", "arbitrary")),
        name="ffn",
    )(x, g.reshape(1, 1, D_MODEL), mod, mod, mod, wu, wd, final_g.reshape(1, 1, D_MODEL))


NPAIR = N_HEADS // 2


def _rwkv_kernel(rkv_ref, lora_ref, sh_rkv_ref, sh_lora_ref, s0_ref, mu_rkv_ref, mu_lora_ref,
                 w0_ref, w2_ref, a0_ref, a2_ref, g2_ref, kkp_ref, ka_ref, rk_ref, lnw_ref, lnb_ref,
                 y_ref, sfin_ref,
                 prev_rkv, prev_lora, s_sc, g_sc, gend_sc, bonus_sc, yo_sc,
                 rt_sc, pt_sc, qt_sc, kt_sc, qe_sc, ke_sc, v_sc, lhs_sc, z_sc, arqk_sc):
    tt = rkv_ref.shape[1]
    nc = tt // CHUNK
    i = pl.program_id(1)

    @pl.when(i == 0)
    def _():
        prev_rkv[...] = sh_rkv_ref[0]
        prev_lora[...] = sh_lora_ref[0]
        for hp in range(NPAIR):
            s_sc[hp] = jnp.concatenate([s0_ref[0, 2 * hp], s0_ref[0, 2 * hp + 1]], axis=1)

    rid = lax.broadcasted_iota(jnp.int32, (tt, 1), 0)

    def tshift(x, prev_row, mu):
        prev = jnp.where(rid == 0, prev_row, pltpu.roll(x, 1, 0))
        return x + (prev - x) * mu

    u = rkv_ref[0]
    ul = lora_ref[0]
    xs = tshift(u, prev_rkv[...], mu_rkv_ref[...])
    xl = tshift(ul, prev_lora[...], mu_lora_ref[...])
    prev_rkv[...] = u[tt - 1:tt, :]
    prev_lora[...] = ul[tt - 1:tt, :]

    r = xs[:, 0:BW]
    k = xs[:, BW:2 * BW]
    v = xs[:, 2 * BW:3 * BW]
    wd = xl[:, 0:64]
    ad = xl[:, 64:128]
    gd = xl[:, 128:256]
    lw = -math.exp(-0.5) * jax.nn.sigmoid(w0_ref[...] + _mm(jnp.tanh(wd), w2_ref[...]))
    a = jax.nn.sigmoid(a0_ref[...] + _mm(ad, a2_ref[...]))
    g_sc[...] = _mm(jax.nn.sigmoid(gd), g2_ref[...])
    kkr = k * kkp_ref[...]
    kh = k * (1.0 + (a - 1.0) * ka_ref[...])

    lane = lax.broadcasted_iota(jnp.int32, (CHUNK, 128), 1)
    row = lax.broadcasted_iota(jnp.int32, (CHUNK, 128), 0)
    col_in_head = lane & (HEAD_DIM - 1)
    tri_strict2 = row > col_in_head
    tri_incl2 = row >= col_in_head
    eye2 = (row == col_in_head).astype(F32)
    first_head = lane < HEAD_DIM
    r128 = lax.broadcasted_iota(jnp.int32, (128, 128), 0)
    c128 = lax.broadcasted_iota(jnp.int32, (128, 128), 1)
    same_head = (r128 < HEAD_DIM) == (c128 < HEAD_DIM)
    ones_bd = same_head.astype(BF16)
    tri_incl_bf = _tri(CHUNK).astype(BF16)

    def head_sum(x, terms):
        acc = None
        for part in _split(x, terms):
            d = jnp.dot(part, ones_bd, preferred_element_type=F32)
            acc = d if acc is None else acc + d
        return acc

    cl = jnp.concatenate(
        [_mm_sel(tri_incl_bf, lw[c * CHUNK:(c + 1) * CHUNK, :], terms=2) for c in range(nc)], axis=0)
    cl_last = jnp.concatenate(
        [jnp.broadcast_to(cl[(c + 1) * CHUNK - 1:(c + 1) * CHUNK, :], (CHUNK, BW)) for c in range(nc)], axis=0)
    e_in = jnp.exp(cl)
    e_ex = jnp.exp(cl - lw)
    e_inv = jnp.exp(-cl)
    e_end = jnp.exp(cl_last - cl)
    gend_sc[...] = jnp.exp(jnp.concatenate(
        [jnp.broadcast_to(cl[(c + 1) * CHUNK - 1:(c + 1) * CHUNK, :], (8, BW)) for c in range(nc)], axis=0))
    for hp in range(NPAIR):
        sl = slice(hp * 128, (hp + 1) * 128)
        kk = kkr[:, sl]
        kk = kk * lax.rsqrt(jnp.maximum(head_sum(kk * kk, 2), 1e-24))
        q = kk * a[:, sl]
        rt_sc[hp] = r[:, sl] * e_in[:, sl]
        pt_sc[hp] = -kk * e_ex[:, sl]
        qt_sc[hp] = q * e_inv[:, sl]
        kt_sc[hp] = kh[:, sl] * e_inv[:, sl]
        qe_sc[hp] = q * e_end[:, sl]
        ke_sc[hp] = kh[:, sl] * e_end[:, sl]
        v_sc[hp] = v[:, sl]
        bonus_sc[:, sl] = head_sum(r[:, sl] * kh[:, sl] * rk_ref[:, sl], 1) * v[:, sl]

    def bd(x):
        return jnp.where(same_head, jnp.concatenate([x, x], axis=0), jnp.zeros((), x.dtype))

    def parts(x):
        hi = _bf(x)
        return hi, _bf(x - hi.astype(F32))

    def bd2(p):
        return bd(p[0]), bd(p[1])

    nn, nt, tn = ((1,), (0,)), ((1,), (1,)), ((0,), (0,))

    def dot3_shared(a_list, b, dims):
        d = lambda x, y: lax.dot_general(x, y, (dims, ((), ())), preferred_element_type=F32)
        ax = 1 if dims == tn else 0
        m = a_list[0][0].shape[ax]
        big = d(jnp.concatenate([t for a in a_list for t in a], axis=ax), b[0])
        small = d(jnp.concatenate([a[0] for a in a_list], axis=ax), b[1])
        return [big[2 * i * m:(2 * i + 1) * m] + big[(2 * i + 1) * m:(2 * i + 2) * m] + small[i * m:(i + 1) * m]
                for i in range(len(a_list))]

    def dot3(a, b, dims):
        return dot3_shared([a], b, dims)[0]

    pairs = range(NPAIR)

    cpi = 2 if nc % 2 == 0 else 1

    def intra_body(n, carry):
        units = [(n * cpi + j, hp) for j in range(cpi) for hp in pairs]
        rows = [pl.ds(pl.multiple_of(c * CHUNK, CHUNK), CHUNK) for c, _ in units]
        un = range(len(units))
        pt = [pt_sc[units[k][1], rows[k], :] for k in un]
        rt = [rt_sc[units[k][1], rows[k], :] for k in un]
        lhs = [parts(jnp.concatenate([pt[k], rt[k]], axis=0)) for k in un]
        gq = [dot3(lhs[k], bd2(parts(qt_sc[units[k][1], rows[k], :])), nt) for k in un]
        gk = [dot3(lhs[k], bd2(parts(kt_sc[units[k][1], rows[k], :])), nt) for k in un]
        l_pq = [jnp.where(tri_strict2, gq[k][0:CHUNK], 0.0) for k in un]
        lv = [dot3(parts(jnp.where(tri_strict2, gk[k][0:CHUNK], 0.0)),
                   bd2(parts(v_sc[units[k][1], rows[k], :])), nn) for k in un]
        tinv = [eye2 + l_pq[k] for k in un]
        xpp = [parts(l_pq[k]) for k in un]
        xpp = [parts(dot3(xpp[k], bd2(xpp[k]), nn)) for k in un]
        for it in range(5):
            tp = [parts(tinv[k]) for k in un]
            if it < 4:
                res = [dot3_shared([xpp[k], tp[k]], bd2(xpp[k]), nn) for k in un]
                xpp = [parts(res[k][0]) for k in un]
                tinv = [tinv[k] + res[k][1] for k in un]
            else:
                tinv = [tinv[k] + dot3(tp[k], bd2(xpp[k]), nn) for k in un]
        tp = [parts(tinv[k]) for k in un]
        wmat = [dot3(tp[k], bd2((lhs[k][0][0:CHUNK], lhs[k][1][0:CHUNK])), nn) for k in un]
        for k, (c, hp) in enumerate(units):
            z_sc[c, hp] = dot3(tp[k], bd2(parts(lv[k])), nn)
            for t, part in enumerate(parts(jnp.concatenate([wmat[k], rt[k]], axis=0))):
                lhs_sc[t, c, hp] = part
            arqk_sc[c, hp] = jnp.concatenate(
                [_bf(jnp.where(tri_incl2, gq[k][CHUNK:2 * CHUNK], 0.0)),
                 _bf(jnp.where(tri_incl2, gk[k][CHUNK:2 * CHUNK], 0.0))], axis=1)
        return carry

    def state_body(c, carry):
        rows = pl.ds(pl.multiple_of(c * CHUNK, CHUNK), CHUNK)
        grow = pl.ds(pl.multiple_of(c * 8, 8), 8)
        vv = [parts(v_sc[hp, rows, :]) for hp in pairs]
        s0 = [s_sc[hp] for hp in pairs]
        ps = [dot3((lhs_sc[0, c, hp], lhs_sc[1, c, hp]), bd2(parts(s0[hp])), nt)
              for hp in pairs]
        uu = [parts(ps[hp][0:CHUNK] + z_sc[c, hp]) for hp in pairs]
        for hp in pairs:
            sl = slice(hp * 128, (hp + 1) * 128)
            yo_sc[rows, sl] = ps[hp][CHUNK:2 * CHUNK] + jnp.dot(
                arqk_sc[c, hp], jnp.concatenate([bd(uu[hp][0]), bd(vv[hp][0])], axis=0),
                preferred_element_type=F32)
        for hp in pairs:
            sl = slice(hp * 128, (hp + 1) * 128)
            uv = tuple(jnp.concatenate([uu[hp][t], vv[hp][t]], axis=0) for t in range(2))
            qk = parts(jnp.concatenate([qe_sc[hp, rows, :], ke_sc[hp, rows, :]], axis=0))
            full = dot3(uv, qk, tn)
            g_end = gend_sc[grow, sl][0:1, :]
            s_sc[hp] = s0[hp] * g_end + jnp.where(
                first_head, full[0:HEAD_DIM], full[HEAD_DIM:2 * HEAD_DIM])
        return carry

    lax.fori_loop(0, nc // cpi, intra_body, 0)
    lax.fori_loop(0, nc, state_body, 0)

    for hp in range(NPAIR):
        sl = slice(hp * 128, (hp + 1) * 128)
        yh = yo_sc[:, sl]
        mean = head_sum(yh, 1) * (1.0 / HEAD_DIM)
        d = yh - mean
        var = head_sum(d * d, 1) * (1.0 / HEAD_DIM)
        yn = d * lax.rsqrt(var + RW_GN_EPS) * lnw_ref[:, sl] + lnb_ref[:, sl]
        y_ref[0, :, sl] = _bf((yn + bonus_sc[:, sl]) * g_sc[:, sl])

    @pl.when(i == pl.num_programs(1) - 1)
    def _():
        for hp in range(NPAIR):
            sp = s_sc[hp]
            sfin_ref[0, 2 * hp] = sp[:, 0:HEAD_DIM]
            sfin_ref[0, 2 * hp + 1] = sp[:, HEAD_DIM:2 * HEAD_DIM]


def _rwkv(u, shift_hist, s0, p, tt):
    b, t, _ = u.shape
    row = lambda a: a.reshape(1, -1)
    sh = shift_hist.reshape(b, 1, RW_IN)
    sh_rkv = sh[:, :, :3 * BW]
    sh_lora = sh[:, :, 3 * BW:]
    mu = p["rw_mu"]
    full = lambda shape: pl.BlockSpec(shape, lambda i, j: (0,) * len(shape))
    scr = lambda: pltpu.VMEM((NPAIR, tt, 128), F32)
    wide = lambda rows: pltpu.VMEM((rows, BW), F32)
    return pl.pallas_call(
        _rwkv_kernel,
        grid=(b, t // tt),
        in_specs=[
            pl.BlockSpec((1, tt, 3 * BW), lambda i, j: (i, j, C_RKV // (3 * BW))),
            pl.BlockSpec((1, tt, RW_LORA), lambda i, j: (i, j, C_LORA // RW_LORA)),
            pl.BlockSpec((1, 1, 3 * BW), lambda i, j: (i, 0, 0)),
            pl.BlockSpec((1, 1, RW_LORA), lambda i, j: (i, 0, 0)),
            pl.BlockSpec((1, N_HEADS, HEAD_DIM, HEAD_DIM), lambda i, j: (i, 0, 0, 0)),
            full((1, 3 * BW)), full((1, RW_LORA)),
            full((1, BW)), full((64, BW)), full((1, BW)), full((64, BW)), full((128, BW)),
            full((1, BW)), full((1, BW)),
            full((1, BW)), full((1, BW)), full((1, BW)),
        ],
        out_specs=[
            pl.BlockSpec((1, tt, BW), lambda i, j: (i, j, 0)),
            pl.BlockSpec((1, N_HEADS, HEAD_DIM, HEAD_DIM), lambda i, j: (i, 0, 0, 0)),
        ],
        out_shape=[
            jax.ShapeDtypeStruct((b, t, BW), BF16),
            jax.ShapeDtypeStruct((b, N_HEADS, HEAD_DIM, HEAD_DIM), F32),
        ],
        scratch_shapes=[
            pltpu.VMEM((1, 3 * BW), F32), pltpu.VMEM((1, RW_LORA), F32),
            pltpu.VMEM((NPAIR, HEAD_DIM, 128), F32),
            wide(tt), wide(8 * (tt // CHUNK)), wide(tt), wide(tt),
            scr(), scr(), scr(), scr(), scr(), scr(), scr(),
            pltpu.VMEM((2, tt // CHUNK, NPAIR, 2 * CHUNK, 128), BF16),
            pltpu.VMEM((tt // CHUNK, NPAIR, CHUNK, 128), F32),
            pltpu.VMEM((tt // CHUNK, NPAIR, CHUNK, 256), BF16),
        ],
        compiler_params=_cparams(("parallel", "arbitrary")),
        name="rwkv7",
    )(u, u, sh_rkv, sh_lora, s0, row(mu[:3 * BW]), row(mu[3 * BW:]),
      row(p["rw_w0"]), p["rw_w2"], row(p["rw_a0"]), p["rw_a2"], p["rw_g2"],
      row(p["rw_kk"]), row(p["rw_ka"]), row(p["rw_rk"]), row(p["rw_ln_w"]), row(p["rw_ln_b"]))


def _pool_kernel(pos0, u_ref, prev_ref, hist_ref, w_ref, scale_ref, y_ref):
    tt = u_ref.shape[1]
    i = pl.program_id(1)
    x = u_ref[0]
    prev = jnp.where(i == 0, hist_ref[0], prev_ref[0])
    s = jnp.concatenate([prev, x], axis=0)
    pos = pos0 + i * tt + lax.broadcasted_iota(jnp.int32, (tt, 1), 0)
    outs = []
    for gi, win in enumerate(POOL_WINDOWS):
        s = s + pltpu.roll(s, win // 2, 0)
        sl = slice(gi * POOL_GW, (gi + 1) * POOL_GW)
        cnt = jnp.minimum(pos + 1, win).astype(F32)
        pg = s[16:, sl] / cnt - x[:, sl]
        outs.append(_mm(pg, w_ref[gi]))
    y_ref[0] = _bf(jnp.concatenate(outs, axis=1) * scale_ref[...])


def _pool(u, hist16, pos0, w_pool, scale, tt):
    b, t, _ = u.shape
    nprev = tt // 16
    return pl.pallas_call(
        functools.partial(_pool_kernel, pos0),
        grid=(b, t // tt),
        in_specs=[
            pl.BlockSpec((1, tt, BW), lambda i, j: (i, j, C_POOL // BW)),
            pl.BlockSpec((1, 16, BW), lambda i, j: (i, jnp.maximum(j * nprev - 1, 0), C_POOL // BW)),
            pl.BlockSpec((1, 16, BW), lambda i, j: (i, 0, 0)),
            pl.BlockSpec((4, POOL_GW, POOL_GW), lambda i, j: (0, 0, 0)),
            pl.BlockSpec((1, BW), lambda i, j: (0, 0)),
        ],
        out_specs=pl.BlockSpec((1, tt, BW), lambda i, j: (i, j, 0)),
        out_shape=jax.ShapeDtypeStruct((b, t, BW), BF16),
        compiler_params=_cparams(("parallel", "parallel")),
        name="pool",
    )(u, u, hist16, w_pool, scale.reshape(1, BW))


def _swa_kernel(mask_history, q_ref, kp2_ref, kp1_ref, kc_ref, vp2_ref, vp1_ref, vc_ref, sink_ref, y_ref):
    tq = q_ref.shape[1]
    nc = tq // CHUNK
    i = pl.program_id(1)
    q = q_ref[0]
    k_all = jnp.concatenate([kp2_ref[0], kp1_ref[0], kc_ref[0]], axis=0)
    v_all = jnp.concatenate([vp2_ref[0], vp1_ref[0], vc_ref[0]], axis=0)
    nk = 3 * CHUNK
    qi = lax.broadcasted_iota(jnp.int32, (CHUNK, nk), 0)
    si = lax.broadcasted_iota(jnp.int32, (CHUNK, nk), 1)
    dist1 = jnp.abs(qi + 2 * CHUNK - si).astype(F32)
    dist = jnp.concatenate([dist1] * SWA_GROUP, axis=0)
    scol = lax.broadcasted_iota(jnp.int32, (SWA_GROUP * CHUNK, nk), 1)
    units = [(c, g) for c in range(nc) for g in range(SWA_KV)]
    bias = []
    sinks = []
    for g in range(SWA_KV):
        heads = [g * SWA_GROUP + hh for hh in range(SWA_GROUP)]
        slope = jnp.concatenate(
            [jnp.full((CHUNK, 1), 2.0 ** (-8.0 * (h + 1) / N_HEADS), F32) for h in heads], axis=0)
        bias.append(slope * dist)
        sinks.append(jnp.concatenate([jnp.full((CHUNK, 1), sink_ref[h], F32) for h in heads], axis=0))

    def scores(c, g):
        ks = k_all[c * CHUNK:c * CHUNK + nk, g * HEAD_DIM:(g + 1) * HEAD_DIM]
        qs = jnp.concatenate(
            [q[c * CHUNK:(c + 1) * CHUNK, (g * SWA_GROUP + hh) * HEAD_DIM:(g * SWA_GROUP + hh + 1) * HEAD_DIM]
             for hh in range(SWA_GROUP)], axis=0)
        return _mm_nt(qs, ks)

    def probs(s, c, g):
        s = s * (HEAD_DIM ** -0.5) - bias[g]
        if mask_history:
            s = jnp.where(scol >= (2 - (i * nc + c)) * CHUNK, s, -1e30)
        m = jnp.maximum(jnp.max(s, axis=-1, keepdims=True), sinks[g])
        p = jnp.exp(s - m)
        denom = jnp.sum(p, axis=-1, keepdims=True) + jnp.exp(sinks[g] - m)
        return _bf(p * (1.0 / denom))

    s_all = [scores(c, g) for c, g in units]
    p_all = [probs(s, c, g) for s, (c, g) in zip(s_all, units)]
    o_all = [_mm(p, v_all[c * CHUNK:c * CHUNK + nk, g * HEAD_DIM:(g + 1) * HEAD_DIM])
             for p, (c, g) in zip(p_all, units)]
    for o, (c, g) in zip(o_all, units):
        for hh in range(SWA_GROUP):
            h = g * SWA_GROUP + hh
            y_ref[0, c * CHUNK:(c + 1) * CHUNK, h * HEAD_DIM:(h + 1) * HEAD_DIM] = _bf(
                o[hh * CHUNK:(hh + 1) * CHUNK])


def _swa(u, hist_k, hist_v, sinks, tq):
    b, t, _ = u.shape
    per = tq // CHUNK
    kvw = SWA_KV * HEAD_DIM
    mask_history = hist_k is None
    cur_k = pl.BlockSpec((1, tq, kvw), lambda i, j: (i, j, C_KS // kvw))
    cur_v = pl.BlockSpec((1, tq, kvw), lambda i, j: (i, j, C_VS // kvw))
    if mask_history:
        prev = lambda d, col: pl.BlockSpec(
            (1, CHUNK, kvw), lambda i, j: (i, jnp.maximum(j * per - d, 0), col))
        specs = [prev(2, C_KS // kvw), prev(1, C_KS // kvw), cur_k,
                 prev(2, C_VS // kvw), prev(1, C_VS // kvw), cur_v]
        args = (u, u, u, u, u, u)
    else:
        assert t == tq
        hk = hist_k.reshape(b, WINDOW, kvw)
        hv = hist_v.reshape(b, WINDOW, kvw)
        hist = lambda blk: pl.BlockSpec((1, CHUNK, kvw), lambda i, j: (i, blk, 0))
        specs = [hist(0), hist(1), cur_k, hist(0), hist(1), cur_v]
        args = (hk, hk, u, hv, hv, u)
    return pl.pallas_call(
        functools.partial(_swa_kernel, mask_history),
        grid=(b, t // tq),
        in_specs=[pl.BlockSpec((1, tq, BW), lambda i, j: (i, j, C_Q // BW))] + specs
        + [pl.BlockSpec(memory_space=pltpu.SMEM)],
        out_specs=pl.BlockSpec((1, tq, BW), lambda i, j: (i, j, 0)),
        out_shape=jax.ShapeDtypeStruct((b, t, BW), BF16),
        compiler_params=_cparams(("parallel", "parallel")),
        name="swa",
    )(u, *args, sinks)


def _ssd_kernel(z_ref, x_ref, b_ref, c_ref, dt_ref, hx_ref, hb_ref, hc_ref, s0_ref,
                cwx_ref, cwb_ref, cwc_ref, cbx_ref, cbb_ref, cbc_ref,
                dtb_ref, alog_ref, dsk_ref, nw_ref,
                y_ref, sfin_ref,
                px_sc, pb_sc, pc_sc, s_sc, xa_sc, ba_sc, ca_sc, acs_sc, gend_sc,
                acsx_sc, eout_sc, y_sc, xdt_sc, xde_sc):
    tt = x_ref.shape[1]
    nc = tt // CHUNK
    i = pl.program_id(1)

    @pl.when(i == 0)
    def _():
        px_sc[...] = hx_ref[0]
        pb_sc[...] = hb_ref[0]
        pc_sc[...] = hc_ref[0]
        s_sc[...] = s0_ref[0].reshape(N_HEADS * HEAD_DIM, SSM_STATE)

    rid8 = lax.broadcasted_iota(jnp.int32, (8, 1), 0)

    def conv_silu(x, prev8, w_ref, b_ref_):
        acc = None
        for wi in range(SSM_CONV):
            sh = SSM_CONV - 1 - wi
            if sh == 0:
                xs = x
            else:
                rolled = pltpu.roll(x, sh, 0)
                top = jnp.where(rid8 < sh, pltpu.roll(prev8, sh, 0), rolled[0:8])
                xs = jnp.concatenate([top, rolled[8:]], axis=0)
            term = xs * w_ref[wi:wi + 1, :]
            acc = (b_ref_[...] + term) if acc is None else acc + term
        return _silu(acc)

    xr = x_ref[0]
    br = b_ref[0]
    cr = c_ref[0]
    xa_sc[...] = conv_silu(xr, px_sc[...], cwx_ref, cbx_ref)
    ba_sc[...] = conv_silu(br, pb_sc[...], cwb_ref, cbb_ref)
    ca_sc[...] = conv_silu(cr, pc_sc[...], cwc_ref, cbc_ref)
    px_sc[...] = xr[tt - 8:tt, :]
    pb_sc[...] = br[tt - 8:tt, :]
    pc_sc[...] = cr[tt - 8:tt, :]
    hl = 128
    dt = _softplus(dt_ref[0][:, 0:hl] + dtb_ref[...])
    ad = -jnp.exp(alog_ref[...]) * dt

    tri_incl_bf = _tri(CHUNK).astype(BF16)
    sel16 = (lax.broadcasted_iota(jnp.int32, (N_HEADS, hl), 0)
             == lax.broadcasted_iota(jnp.int32, (N_HEADS, hl), 1)).astype(BF16)
    spread = (lax.broadcasted_iota(jnp.int32, (hl, BW), 0)
              == lax.broadcasted_iota(jnp.int32, (hl, BW), 1) // HEAD_DIM).astype(BF16)

    def per_channel(x, terms):
        acc = None
        for part in _split(x, terms):
            d = jnp.dot(part, spread, preferred_element_type=F32)
            acc = d if acc is None else acc + d
        return acc

    acs = jnp.concatenate(
        [_mm_sel(tri_incl_bf, ad[c * CHUNK:(c + 1) * CHUNK, :]) for c in range(nc)], axis=0)
    a_last = jnp.concatenate(
        [jnp.broadcast_to(acs[(c + 1) * CHUNK - 1:(c + 1) * CHUNK, :], (CHUNK, hl)) for c in range(nc)], axis=0)
    acs_sc[...] = acs
    acsx_sc[...] = per_channel(acs, 3)
    eout_sc[...] = per_channel(jnp.exp(acs), 2)
    gend_sc[...] = jnp.exp(jnp.concatenate(
        [jnp.broadcast_to(acs[(c + 1) * CHUNK - 1:(c + 1) * CHUNK, :], (8, hl)) for c in range(nc)], axis=0))
    xdt = xa_sc[...] * per_channel(dt, 2)
    xdt_sc[...] = _bf(xdt)
    xde_sc[...] = _bf(xdt * per_channel(jnp.exp(a_last - acs), 2))

    lane = lax.broadcasted_iota(jnp.int32, (CHUNK, 128), 1)
    row = lax.broadcasted_iota(jnp.int32, (CHUNK, 128), 0)
    tri_incl2 = row >= (lane & (HEAD_DIM - 1))
    r128 = lax.broadcasted_iota(jnp.int32, (128, 128), 0)
    c128 = lax.broadcasted_iota(jnp.int32, (128, 128), 1)
    same_head = (r128 < HEAD_DIM) == (c128 < HEAD_DIM)
    hpg = N_HEADS // SSM_GROUPS
    rpg = hpg * HEAD_DIM

    def chunk_body(c, carry):
        rows = pl.ds(pl.multiple_of(c * CHUNK, CHUNK), CHUNK)
        grow = pl.ds(pl.multiple_of(c * 8, 8), 8)
        bc = _bf(ba_sc[rows, :])
        cc = _bf(ca_sc[rows, :])
        acs_t = _mm_sel_nt(sel16, acs_sc[rows, :])
        g_end = gend_sc[grow, :][0:1, :]
        gsl = [slice(gi * SSM_STATE, (gi + 1) * SSM_STATE) for gi in range(SSM_GROUPS)]
        scores = [_mm_nt(cc[:, gs], bc[:, gs]) for gs in gsl]
        scores2 = [jnp.concatenate([s, s], axis=1) for s in scores]
        y_off = [_mm_nt(cc[:, gsl[gi]], s_sc[gi * rpg:(gi + 1) * rpg, :]) for gi in range(SSM_GROUPS)]
        upd = [_mm_tn(xde_sc[rows, gi * rpg:(gi + 1) * rpg], bc[:, gsl[gi]]) for gi in range(SSM_GROUPS)]
        y_diag = []
        for hp in range(N_HEADS // 2):
            sl = slice(hp * 128, (hp + 1) * 128)
            seg = acsx_sc[rows, sl] - jnp.concatenate([acs_t[2 * hp:2 * hp + 1, :],
                                                       acs_t[2 * hp + 1:2 * hp + 2, :]], axis=1)
            m = scores2[(2 * hp) // hpg] * jnp.exp(jnp.where(tri_incl2, seg, -1e30))
            xd = xdt_sc[rows, sl]
            y_diag.append(_mm(m, jnp.where(same_head, jnp.concatenate([xd, xd], axis=0),
                                           jnp.zeros((), BF16))))
        for h in range(N_HEADS):
            gi, hl_ = divmod(h, hpg)
            hrows = slice(h * HEAD_DIM, (h + 1) * HEAD_DIM)
            s_sc[hrows, :] = (s_sc[hrows, :] * g_end[:, h:h + 1]
                              + upd[gi][hl_ * HEAD_DIM:(hl_ + 1) * HEAD_DIM, :])
        y_sc[rows, :] = (jnp.concatenate(y_diag, axis=1)
                         + jnp.concatenate(y_off, axis=1) * eout_sc[rows, :])
        return carry

    lax.fori_loop(0, nc, chunk_body, 0)

    y = (y_sc[...] + xa_sc[...] * dsk_ref[...]) * _silu(z_ref[0])
    gw = BW // SSM_GROUPS
    parts = []
    for gi in range(SSM_GROUPS):
        yg = y[:, gi * gw:(gi + 1) * gw]
        parts.append(yg * lax.rsqrt(jnp.mean(yg * yg, axis=-1, keepdims=True) + NORM_EPS))
    y_ref[0] = _bf(jnp.concatenate(parts, axis=1) * nw_ref[...])

    @pl.when(i == pl.num_programs(1) - 1)
    def _():
        sfin_ref[0] = s_sc[...].reshape(N_HEADS, HEAD_DIM, SSM_STATE)


def _pad_rows8(a):
    return jnp.pad(a, ((0, 0), (8 - a.shape[1], 0), (0, 0)))


def _pad_lanes(a, n):
    return jnp.pad(a.reshape(1, -1), ((0, 0), (0, n - a.shape[-1])))


def _ssd(u, conv_hist, s0, p, tt):
    b, t, _ = u.shape
    h8 = _pad_rows8(conv_hist)
    hx, hb, hc = h8[:, :, :BW], h8[:, :, BW:BW + SSM_BC], h8[:, :, BW + SSM_BC:]
    cw, cb = p["ssm_conv_w"], p["ssm_conv_b"].reshape(1, -1)
    full = lambda shape: pl.BlockSpec(shape, lambda i, j: (0,) * len(shape))
    col = lambda w, c0: pl.BlockSpec((1, tt, w), lambda i, j: (i, j, c0 // w))
    hist = lambda w: pl.BlockSpec((1, 8, w), lambda i, j: (i, 0, 0))
    return pl.pallas_call(
        _ssd_kernel,
        grid=(b, t // tt),
        in_specs=[
            col(BW, C_Z), col(BW, C_X), col(SSM_BC, C_B), col(SSM_BC, C_C), col(SSM_BC, C_DT),
            hist(BW), hist(SSM_BC), hist(SSM_BC),
            pl.BlockSpec((1, N_HEADS, HEAD_DIM, SSM_STATE), lambda i, j: (i, 0, 0, 0)),
            full((SSM_CONV, BW)), full((SSM_CONV, SSM_BC)), full((SSM_CONV, SSM_BC)),
            full((1, BW)), full((1, SSM_BC)), full((1, SSM_BC)),
            full((1, 128)), full((1, 128)), full((1, BW)), full((1, BW)),
        ],
        out_specs=[
            pl.BlockSpec((1, tt, BW), lambda i, j: (i, j, 0)),
            pl.BlockSpec((1, N_HEADS, HEAD_DIM, SSM_STATE), lambda i, j: (i, 0, 0, 0)),
        ],
        out_shape=[
            jax.ShapeDtypeStruct((b, t, BW), BF16),
            jax.ShapeDtypeStruct((b, N_HEADS, HEAD_DIM, SSM_STATE), F32),
        ],
        scratch_shapes=[
            pltpu.VMEM((8, BW), F32), pltpu.VMEM((8, SSM_BC), F32), pltpu.VMEM((8, SSM_BC), F32),
            pltpu.VMEM((N_HEADS * HEAD_DIM, SSM_STATE), F32),
            pltpu.VMEM((tt, BW), F32), pltpu.VMEM((tt, SSM_BC), F32), pltpu.VMEM((tt, SSM_BC), F32),
            pltpu.VMEM((tt, 128), F32), pltpu.VMEM((8 * (tt // CHUNK), 128), F32),
            pltpu.VMEM((tt, BW), F32), pltpu.VMEM((tt, BW), F32), pltpu.VMEM((tt, BW), F32),
            pltpu.VMEM((tt, BW), BF16), pltpu.VMEM((tt, BW), BF16),
        ],
        compiler_params=_cparams(("parallel", "arbitrary")),
        name="ssd",
    )(u, u, u, u, u, hx, hb, hc, s0,
      cw[:, :BW], cw[:, BW:BW + SSM_BC], cw[:, BW + SSM_BC:],
      cb[:, :BW], cb[:, BW:BW + SSM_BC], cb[:, BW + SSM_BC:],
      _pad_lanes(p["ssm_dt_bias"], 128), _pad_lanes(p["ssm_a_log"], 128),
      jnp.repeat(p["ssm_d"], HEAD_DIM).reshape(1, BW), p["ssm_norm"].reshape(1, BW))


def _prep_w_in(w_in):
    w_in = w_in.astype(BF16)
    c = lambda a, n: w_in[:, :, a:a + n]
    ssm = _N_SSM
    parts = [
        c(0, 3 * BW),
        c(ssm, BW),
        c(_N_POOL, BW),
        c(_N_SWA, BW),
        c(ssm + BW, BW),
        c(3 * BW, RW_LORA),
        c(_N_SWA + BW, 2 * SWA_KV * HEAD_DIM),
        c(ssm + 2 * BW, 2 * SSM_BC + N_HEADS),
    ]
    parts.append(jnp.zeros(w_in.shape[:2] + (U_COLS - IN_COLS,), BF16))
    return jnp.concatenate(parts, axis=-1)


def _run_group(x, mods, st, p, final_norm, wts, pos0, bb, tt, tt_branch):
    b, t, _ = x.shape
    kvw = SWA_KV * HEAD_DIM
    outs = {k: [] for k in ("rwkv", "shift", "pool", "k", "v", "ssm", "conv")}
    for l in range(DEPTH):
        pl_ = {k: v[l] for k, v in p.items()}
        mod = mods[l]
        u, h = _inproj(x, pl_["norm_mix"], mod, wts["w_in"][l], bb, tt)
        if st is None:
            shift_hist = jnp.zeros((b, 1, RW_IN), F32)
            s_rwkv = jnp.zeros((b, N_HEADS, HEAD_DIM, HEAD_DIM), F32)
            pool_hist = jnp.zeros((b, POOL_HIST, BW), F32)
            hk = hv = None
            s_ssm = jnp.zeros((b, N_HEADS, HEAD_DIM, SSM_STATE), F32)
            conv_hist = jnp.zeros((b, SSM_CONV - 1, SSM_CONV_DIM), F32)
        else:
            shift_hist, s_rwkv, pool_hist = st["shift"][l], st["rwkv"][l], st["pool"][l]
            hk, hv, s_ssm, conv_hist = st["k"][l], st["v"][l], st["ssm"][l], st["conv"][l]
        y_a, n_rwkv = _rwkv(u, shift_hist, s_rwkv, pl_, min(tt_branch, 256))
        hist16 = jnp.pad(pool_hist, ((0, 0), (1, 0), (0, 0)))
        y_b = _pool(u, hist16, pos0, wts["pool_w"][l], pl_["pool_scale"], tt_branch)
        y_c = _swa(u, hk, hv, pl_["swa_sinks"], min(tt_branch, 256))
        y_d, n_ssm = _ssd(u, conv_hist, s_ssm, pl_, tt_branch)
        merged = _merge(h, (y_a, y_b, y_c, y_d), wts["w_gate"][l], wts["w_branch"][l], bb, tt)
        x = _outproj(merged, wts["w_out"][l], x, mod, bb, tt)
        x = _ffn(x, pl_["norm_ffn"], mod, wts["w_up"][l], wts["w_down"][l], final_norm, l == DEPTH - 1,
                 bb, tt)

        outs["rwkv"].append(n_rwkv)
        outs["shift"].append(jnp.concatenate(
            [u[:, t - 1:, C_RKV:C_RKV + 3 * BW], u[:, t - 1:, C_LORA:C_LORA + RW_LORA]], axis=-1))
        outs["pool"].append(u[:, t - POOL_HIST:, C_POOL:C_POOL + BW])
        k_new = u[:, :, C_KS:C_KS + kvw]
        v_new = u[:, :, C_VS:C_VS + kvw]
        if hk is not None:
            k_new = jnp.concatenate([hk.reshape(b, WINDOW, kvw), k_new], axis=1)
            v_new = jnp.concatenate([hv.reshape(b, WINDOW, kvw), v_new], axis=1)
        outs["k"].append(k_new[:, -WINDOW:].reshape(b, WINDOW, SWA_KV, HEAD_DIM))
        outs["v"].append(v_new[:, -WINDOW:].reshape(b, WINDOW, SWA_KV, HEAD_DIM))
        outs["ssm"].append(n_ssm)
        outs["conv"].append(jnp.concatenate(
            [u[:, t - 3:, C_X:C_X + BW], u[:, t - 3:, C_B:C_B + 2 * SSM_BC]], axis=-1))
    order = ("rwkv", "shift", "pool", "k", "v", "ssm", "conv")
    return x, tuple(jnp.stack(outs[k]) for k in order)


def kernel(x_prompt, x_sample, state_rwkv, state_rwkv_shift, state_pool, cache_swa_k, cache_swa_v,
           state_ssm, state_ssm_conv, c_prompt, c_sample, ada_w, ada_b, norm_mix, norm_ffn, w_in,
           rw_mu, rw_w0, rw_w2, rw_a0, rw_a2, rw_g2, rw_kk, rw_ka, rw_rk, rw_ln_w, rw_ln_b, pool_w,
           pool_scale, swa_sinks, ssm_conv_w, ssm_conv_b, ssm_dt_bias, ssm_a_log, ssm_d, ssm_norm,
           w_gate, w_branch, w_out, w_up, w_down, final_norm):
    bp, tp, _ = x_prompt.shape
    bs, ts, _ = x_sample.shape
    p = dict(norm_mix=norm_mix, norm_ffn=norm_ffn, rw_mu=rw_mu, rw_w0=rw_w0, rw_w2=rw_w2, rw_a0=rw_a0,
             rw_a2=rw_a2, rw_g2=rw_g2, rw_kk=rw_kk, rw_ka=rw_ka, rw_rk=rw_rk, rw_ln_w=rw_ln_w,
             rw_ln_b=rw_ln_b, pool_scale=pool_scale, swa_sinks=swa_sinks, ssm_conv_w=ssm_conv_w,
             ssm_conv_b=ssm_conv_b, ssm_dt_bias=ssm_dt_bias, ssm_a_log=ssm_a_log, ssm_d=ssm_d,
             ssm_norm=ssm_norm)
    wts = dict(w_in=_prep_w_in(w_in), pool_w=pool_w.astype(BF16), w_gate=w_gate.astype(BF16),
               w_branch=w_branch.astype(BF16), w_out=w_out.astype(BF16), w_up=w_up.astype(BF16),
               w_down=w_down.astype(BF16))

    nb = bp + bs
    nb_pad = -(-nb // 8) * 8
    c_all = jnp.pad(jnp.concatenate([c_prompt, c_sample], axis=0), ((0, nb_pad - nb), (0, 0)))
    mod_all = _adaln(c_all, ada_w, ada_b)
    mods_p = mod_all[:, :bp].reshape(DEPTH, bp, 6, 1, D_MODEL)
    mods_s = mod_all[:, bp:nb].reshape(DEPTH, bs, 6, 1, D_MODEL)

    y_prompt, st_p = _run_group(x_prompt, mods_p, None, p, final_norm, wts, 0,
                                1, min(tp, 1024), min(tp, 512))
    st_s = dict(rwkv=state_rwkv, shift=state_rwkv_shift, pool=state_pool, k=cache_swa_k,
                v=cache_swa_v, ssm=state_ssm, conv=state_ssm_conv)
    y_sample, st_o = _run_group(x_sample, mods_s, st_s, p, final_norm, wts, PAST_LEN, bs, ts, ts)
    return (y_prompt, y_sample) + st_p + st_o
```

```python
import functools
import math

import jax
import jax.numpy as jnp
from jax import lax
from jax.experimental import pallas as pl
from jax.experimental.pallas import tpu as pltpu

F32 = jnp.float32
BF16 = jnp.bfloat16

D_MODEL = 2048
DEPTH = 4
PAST_LEN = 1024
CHUNK = 64
HEAD_DIM = 64
BW = D_MODEL // 2
D_FF = 4 * D_MODEL
NORM_EPS = 1e-6
N_HEADS = BW // HEAD_DIM
RW_LORA = 256
RW_IN = 3 * BW + RW_LORA
RW_GN_EPS = HEAD_DIM * 1e-5
POOL_WINDOWS = (2, 4, 8, 16)
POOL_GW = BW // 4
POOL_HIST = 15
SWA_KV = 4
SWA_GROUP = N_HEADS // SWA_KV
WINDOW = 128
SSM_GROUPS = 2
SSM_STATE = 128
SSM_CONV = 4
SSM_BC = SSM_GROUPS * SSM_STATE
SSM_CONV_DIM = BW + 2 * SSM_BC

C_RKV = 0
C_Z = 3072
C_POOL = 4096
C_Q = 5120
C_X = 6144
C_LORA = 7168
C_KS = 7424
C_VS = 7680
C_B = 7936
C_C = 8192
C_DT = 8448
U_COLS = 8704

_N_POOL = RW_IN
_N_SWA = _N_POOL + BW
_N_SSM = _N_SWA + (N_HEADS + 2 * SWA_KV) * HEAD_DIM
IN_COLS = _N_SSM + BW + SSM_CONV_DIM + N_HEADS

VMEM_LIMIT = 56 * 1024 * 1024


def _cparams(sem):
    return pltpu.CompilerParams(dimension_semantics=sem, vmem_limit_bytes=VMEM_LIMIT)


def _bf(x):
    return x.astype(BF16)


def _mm(a, b):
    return jnp.dot(_bf(a), _bf(b), preferred_element_type=F32)


def _mm_nt(a, b):
    return lax.dot_general(_bf(a), _bf(b), (((1,), (1,)), ((), ())), preferred_element_type=F32)


def _mm_tn(a, b):
    return lax.dot_general(_bf(a), _bf(b), (((0,), (0,)), ((), ())), preferred_element_type=F32)


def _split3(x):
    hi = _bf(x)
    r1 = x - hi.astype(F32)
    mid = _bf(r1)
    lo = _bf(r1 - mid.astype(F32))
    return hi, mid, lo


def _split(x, terms):
    parts = []
    for _ in range(terms - 1):
        hi = _bf(x)
        parts.append(hi)
        x = x - hi.astype(F32)
    parts.append(_bf(x))
    return parts


def _mm_sel(c, x, terms=3):
    acc = None
    for part in _split(x, terms):
        d = jnp.dot(c, part, preferred_element_type=F32)
        acc = d if acc is None else acc + d
    return acc


def _mm_sel_nt(c, x):
    hi, mid, lo = _split3(x)
    d = lambda p: lax.dot_general(c, p, (((1,), (1,)), ((), ())), preferred_element_type=F32)
    return d(hi) + d(mid) + d(lo)


def _softplus(x):
    return jnp.maximum(x, 0.0) + jnp.log1p(jnp.exp(-jnp.abs(x)))


def _sigmoid(x):
    return 0.5 * jnp.tanh(0.5 * x) + 0.5


def _silu(x):
    return x * _sigmoid(x)


def _tri(n, strict=False):
    r = lax.broadcasted_iota(jnp.int32, (n, n), 0)
    c = lax.broadcasted_iota(jnp.int32, (n, n), 1)
    return (r > c) if strict else (r >= c)


def _adaln_kernel(c_ref, w_ref, b_ref, o_ref):
    s = _silu(c_ref[...])
    o_ref[0] = _mm(s, w_ref[0]) + b_ref[0]


def _adaln(c_all, ada_w, ada_b):
    nb = c_all.shape[0]
    tn = 1024
    n_out = ada_w.shape[2]
    return pl.pallas_call(
        _adaln_kernel,
        grid=(DEPTH, n_out // tn),
        in_specs=[
            pl.BlockSpec((nb, D_MODEL), lambda l, n: (0, 0)),
            pl.BlockSpec((1, D_MODEL, tn), lambda l, n: (l, 0, n)),
            pl.BlockSpec((1, 1, tn), lambda l, n: (l, 0, n)),
        ],
        out_specs=pl.BlockSpec((1, nb, tn), lambda l, n: (l, 0, n)),
        out_shape=jax.ShapeDtypeStruct((DEPTH, nb, n_out), F32),
        compiler_params=_cparams(("parallel", "parallel")),
        name="adaln",
    )(c_all, ada_w, ada_b.reshape(DEPTH, 1, n_out))


def _norm_mod(x, g, sc, sh):
    y = x * lax.rsqrt(jnp.mean(x * x, axis=-1, keepdims=True) + NORM_EPS)
    return (y * g) * (1.0 + sc) + sh


def _inproj_kernel(x_ref, g_ref, sc_ref, sh_ref, w_ref, u_ref, h_ref):
    bb, tt, _ = x_ref.shape

    @pl.when(pl.program_id(2) == 0)
    def _():
        h = _norm_mod(x_ref[...], g_ref[...], sc_ref[:, 0], sh_ref[:, 0])
        h_ref[...] = _bf(h)

    h = h_ref[...].reshape(bb * tt, D_MODEL)
    u_ref[...] = jnp.dot(h, w_ref[...], preferred_element_type=F32).reshape(u_ref.shape)


def _inproj(x, g, mod, w, l, bb, tt):
    b, t, _ = x.shape
    tn = U_COLS // 4
    if bb * tt > 512:
        tt = 512 // bb
    return pl.pallas_call(
        _inproj_kernel,
        grid=(b // bb, t // tt, U_COLS // tn),
        in_specs=[
            pl.BlockSpec((bb, tt, D_MODEL), lambda i, j, n: (i, j, 0)),
            pl.BlockSpec((1, 1, D_MODEL), lambda i, j, n: (0, 0, 0)),
            pl.BlockSpec((bb, 1, 1, D_MODEL), lambda i, j, n: (i, 1, 0, 0)),
            pl.BlockSpec((bb, 1, 1, D_MODEL), lambda i, j, n: (i, 0, 0, 0)),
            pl.BlockSpec((None, D_MODEL, tn), lambda i, j, n: (l, 0, n)),
        ],
        out_specs=[
            pl.BlockSpec((bb, tt, tn), lambda i, j, n: (i, j, n)),
            pl.BlockSpec((bb, tt, D_MODEL), lambda i, j, n: (i, j, 0)),
        ],
        out_shape=[
            jax.ShapeDtypeStruct((b, t, U_COLS), F32),
            jax.ShapeDtypeStruct((b, t, D_MODEL), BF16),
        ],
        compiler_params=_cparams(("parallel", "parallel", "arbitrary")),
        name="inproj",
    )(x, g.reshape(1, 1, D_MODEL), mod, mod, w)


def _merge_kernel(h_ref, ya_ref, yb_ref, yc_ref, yd_ref, wg_ref, wb_ref, o_ref):
    bb, tt, _ = h_ref.shape
    m = bb * tt
    h = h_ref[...].reshape(m, D_MODEL)
    acc = None
    for i, y_ref in enumerate((ya_ref, yb_ref, yc_ref, yd_ref)):
        gate = _sigmoid(jnp.dot(h, wg_ref[i], preferred_element_type=F32))
        br = jnp.dot(y_ref[...].reshape(m, BW), wb_ref[i], preferred_element_type=F32)
        acc = gate * br if acc is None else acc + gate * br
    o_ref[...] = _bf(acc).reshape(o_ref.shape)


def _merge(h, ys, wg, wb, l, bb, tt):
    b, t, _ = h.shape
    tn = 256
    yspec = pl.BlockSpec((bb, tt, BW), lambda i, j, n: (i, j, 0))
    return pl.pallas_call(
        _merge_kernel,
        grid=(b // bb, t // tt, D_MODEL // tn),
        in_specs=[
            pl.BlockSpec((bb, tt, D_MODEL), lambda i, j, n: (i, j, 0)),
            yspec, yspec, yspec, yspec,
            pl.BlockSpec((None, 4, D_MODEL, tn), lambda i, j, n: (l, 0, 0, n)),
            pl.BlockSpec((None, 4, BW, tn), lambda i, j, n: (l, 0, 0, n)),
        ],
        out_specs=pl.BlockSpec((bb, tt, tn), lambda i, j, n: (i, j, n)),
        out_shape=jax.ShapeDtypeStruct((b, t, D_MODEL), BF16),
        compiler_params=_cparams(("parallel", "parallel", "arbitrary")),
        name="merge",
    )(h, *ys, wg, wb)


def _outproj_kernel(m_ref, w_ref, x_ref, g_ref, o_ref):
    bb, tt, _ = m_ref.shape
    y = jnp.dot(m_ref[...].reshape(bb * tt, D_MODEL), w_ref[...], preferred_element_type=F32)
    o_ref[...] = x_ref[...] + g_ref[:, 0] * y.reshape(o_ref.shape)


def _outproj(merged, w, l, x, mod, bb, tt):
    b, t, _ = x.shape
    tn = 1024
    return pl.pallas_call(
        _outproj_kernel,
        grid=(b // bb, t // tt, D_MODEL // tn),
        in_specs=[
            pl.BlockSpec((bb, tt, D_MODEL), lambda i, j, n: (i, j, 0)),
            pl.BlockSpec((None, D_MODEL, tn), lambda i, j, n: (l, 0, n)),
            pl.BlockSpec((bb, tt, tn), lambda i, j, n: (i, j, n)),
            pl.BlockSpec((bb, 1, 1, tn), lambda i, j, n: (i, 2, 0, n)),
        ],
        out_specs=pl.BlockSpec((bb, tt, tn), lambda i, j, n: (i, j, n)),
        out_shape=jax.ShapeDtypeStruct((b, t, D_MODEL), F32),
        compiler_params=_cparams(("parallel", "parallel", "arbitrary")),
        name="outproj",
    )(merged, w, x, mod)


def _ffn_kernel(final, x_ref, g_ref, sc_ref, sh_ref, gate_ref, wu_ref, wd_ref, fg_ref, o_ref, h_sc):
    bb, tt, _ = x_ref.shape
    m = bb * tt
    f = pl.program_id(2)

    @pl.when(f == 0)
    def _():
        h = _norm_mod(x_ref[...], g_ref[...], sc_ref[:, 0], sh_ref[:, 0])
        h_sc[...] = _bf(h).reshape(m, D_MODEL)
        o_ref[...] = jnp.zeros(o_ref.shape, F32)

    a = jnp.dot(h_sc[...], wu_ref[...], preferred_element_type=F32)
    a = jnp.square(jnp.maximum(a, 0.0))
    o_ref[...] += jnp.dot(_bf(a), wd_ref[...], preferred_element_type=F32).reshape(o_ref.shape)

    @pl.when(f == pl.num_programs(2) - 1)
    def _():
        y = x_ref[...] + gate_ref[:, 0] * o_ref[...]
        if final:
            y = (y * lax.rsqrt(jnp.mean(y * y, axis=-1, keepdims=True) + NORM_EPS)) * fg_ref[...]
        o_ref[...] = y


def _ffn(x, g, mod, wu, wd, l, final_g, final, bb, tt):
    b, t, _ = x.shape
    tf = 1024
    if bb * tt > 512:
        tt = 512 // bb
    vec = pl.BlockSpec((1, 1, D_MODEL), lambda i, j, f: (0, 0, 0))
    return pl.pallas_call(
        functools.partial(_ffn_kernel, final),
        grid=(b // bb, t // tt, D_FF // tf),
        in_specs=[
            pl.BlockSpec((bb, tt, D_MODEL), lambda i, j, f: (i, j, 0)),
            vec,
            pl.BlockSpec((bb, 1, 1, D_MODEL), lambda i, j, f: (i, 4, 0, 0)),
            pl.BlockSpec((bb, 1, 1, D_MODEL), lambda i, j, f: (i, 3, 0, 0)),
            pl.BlockSpec((bb, 1, 1, D_MODEL), lambda i, j, f: (i, 5, 0, 0)),
            pl.BlockSpec((None, D_MODEL, tf), lambda i, j, f: (l, 0, f)),
            pl.BlockSpec((None, tf, D_MODEL), lambda i, j, f: (l, f, 0)),
            vec,
        ],
        out_specs=pl.BlockSpec((bb, tt, D_MODEL), lambda i, j, f: (i, j, 0)),
        out_shape=jax.ShapeDtypeStruct((b, t, D_MODEL), F32),
        scratch_shapes=[pltpu.VMEM((bb * tt, D_MODEL), BF16)],
        compiler_params=_cparams(("parallel", "parallel", "arbitrary")),
        name="ffn",
    )(x, g.reshape(1, 1, D_MODEL), mod, mod, mod, wu, wd, final_g.reshape(1, 1, D_MODEL))


NPAIR = N_HEADS // 2


def _rwkv_kernel(rkv_ref, lora_ref, sh_rkv_ref, sh_lora_ref, s0_ref, mu_rkv_ref, mu_lora_ref,
                 w0_ref, w2_ref, a0_ref, a2_ref, g2_ref, kkp_ref, ka_ref, rk_ref, lnw_ref, lnb_ref,
                 y_ref, sfin_ref,
                 prev_rkv, prev_lora, s_sc, g_sc, gend_sc, bonus_sc, yo_sc,
                 rt_sc, pt_sc, qt_sc, kt_sc, qe_sc, ke_sc, v_sc, lhs_sc, z_sc, arqk_sc):
    tt = rkv_ref.shape[1]
    nc = tt // CHUNK
    i = pl.program_id(1)

    @pl.when(i == 0)
    def _():
        prev_rkv[...] = sh_rkv_ref[0]
        prev_lora[...] = sh_lora_ref[0]
        for hp in range(NPAIR):
            s_sc[hp] = jnp.concatenate([s0_ref[0, 2 * hp], s0_ref[0, 2 * hp + 1]], axis=1)

    rid8 = lax.broadcasted_iota(jnp.int32, (8, 1), 0)

    def tshift(x, prev_row, mu):
        rolled = pltpu.roll(x, 1, 0)
        prev = jnp.concatenate([jnp.where(rid8 == 0, prev_row, rolled[0:8]), rolled[8:]], axis=0)
        return x + (prev - x) * mu

    u = rkv_ref[0]
    ul = lora_ref[0]
    xs = tshift(u, prev_rkv[...], mu_rkv_ref[...])
    xl = tshift(ul, prev_lora[...], mu_lora_ref[...])
    prev_rkv[...] = u[tt - 1:tt, :]
    prev_lora[...] = ul[tt - 1:tt, :]

    r = xs[:, 0:BW]
    k = xs[:, BW:2 * BW]
    v = xs[:, 2 * BW:3 * BW]
    wd = xl[:, 0:64]
    ad = xl[:, 64:128]
    gd = xl[:, 128:256]
    lw = -math.exp(-0.5) * _sigmoid(w0_ref[...] + _mm(jnp.tanh(wd), w2_ref[...]))
    a = _sigmoid(a0_ref[...] + _mm(ad, a2_ref[...]))
    g_sc[...] = _mm(_sigmoid(gd), g2_ref[...])
    kkr = k * kkp_ref[...]
    kh = k * (1.0 + (a - 1.0) * ka_ref[...])

    lane = lax.broadcasted_iota(jnp.int32, (CHUNK, 128), 1)
    row = lax.broadcasted_iota(jnp.int32, (CHUNK, 128), 0)
    col_in_head = lane & (HEAD_DIM - 1)
    tri_strict2 = row > col_in_head
    tri_incl2 = row >= col_in_head
    eye2 = (row == col_in_head).astype(F32)
    first_head = lane < HEAD_DIM
    r128 = lax.broadcasted_iota(jnp.int32, (128, 128), 0)
    c128 = lax.broadcasted_iota(jnp.int32, (128, 128), 1)
    same_head = (r128 < HEAD_DIM) == (c128 < HEAD_DIM)
    ones_bd = same_head.astype(BF16)
    tri_incl_bf = _tri(CHUNK).astype(BF16)

    def head_sum(x, terms):
        acc = None
        for part in _split(x, terms):
            d = jnp.dot(part, ones_bd, preferred_element_type=F32)
            acc = d if acc is None else acc + d
        return acc

    cl = jnp.concatenate(
        [_mm_sel(tri_incl_bf, lw[c * CHUNK:(c + 1) * CHUNK, :], terms=2) for c in range(nc)], axis=0)
    cl_last = jnp.concatenate(
        [jnp.broadcast_to(cl[(c + 1) * CHUNK - 1:(c + 1) * CHUNK, :], (CHUNK, BW)) for c in range(nc)], axis=0)
    e_in = jnp.exp(cl)
    e_ex = jnp.exp(cl - lw)
    e_inv = jnp.exp(-cl)
    e_end = jnp.exp(cl_last - cl)
    gend_sc[...] = jnp.exp(jnp.concatenate(
        [jnp.broadcast_to(cl[(c + 1) * CHUNK - 1:(c + 1) * CHUNK, :], (8, BW)) for c in range(nc)], axis=0))
    for hp in range(NPAIR):
        sl = slice(hp * 128, (hp + 1) * 128)
        kk = kkr[:, sl]
        kk = kk * lax.rsqrt(jnp.maximum(head_sum(kk * kk, 2), 1e-24))
        q = kk * a[:, sl]
        rt_sc[hp] = r[:, sl] * e_in[:, sl]
        pt_sc[hp] = -kk * e_ex[:, sl]
        qt_sc[hp] = q * e_inv[:, sl]
        kt_sc[hp] = kh[:, sl] * e_inv[:, sl]
        qe_sc[hp] = q * e_end[:, sl]
        ke_sc[hp] = kh[:, sl] * e_end[:, sl]
        v_sc[hp] = v[:, sl]
        bonus_sc[:, sl] = head_sum(r[:, sl] * kh[:, sl] * rk_ref[:, sl], 1) * v[:, sl]

    def bd(x):
        return jnp.where(same_head, jnp.concatenate([x, x], axis=0), jnp.zeros((), x.dtype))

    def parts(x):
        hi = _bf(x)
        return hi, _bf(x - hi.astype(F32))

    def bd2(p):
        return bd(p[0]), bd(p[1])

    nn, nt, tn = ((1,), (0,)), ((1,), (1,)), ((0,), (0,))

    def dot3_shared(a_list, b, dims):
        d = lambda x, y: lax.dot_general(x, y, (dims, ((), ())), preferred_element_type=F32)
        ax = 1 if dims == tn else 0
        m = a_list[0][0].shape[ax]
        big = d(jnp.concatenate([t for a in a_list for t in a], axis=ax), b[0])
        small = d(jnp.concatenate([a[0] for a in a_list], axis=ax), b[1])
        return [big[2 * i * m:(2 * i + 1) * m] + big[(2 * i + 1) * m:(2 * i + 2) * m] + small[i * m:(i + 1) * m]
                for i in range(len(a_list))]

    def dot3(a, b, dims):
        return dot3_shared([a], b, dims)[0]

    pairs = range(NPAIR)

    cpi = 2 if nc % 2 == 0 else 1

    def intra_body(n, carry):
        units = [(n * cpi + j, hp) for j in range(cpi) for hp in pairs]
        rows = [pl.ds(pl.multiple_of(c * CHUNK, CHUNK), CHUNK) for c, _ in units]
        un = range(len(units))
        pt = [pt_sc[units[k][1], rows[k], :] for k in un]
        rt = [rt_sc[units[k][1], rows[k], :] for k in un]
        lhs = [parts(jnp.concatenate([pt[k], rt[k]], axis=0)) for k in un]
        gq = [dot3(lhs[k], bd2(parts(qt_sc[units[k][1], rows[k], :])), nt) for k in un]
        gk = [dot3(lhs[k], bd2(parts(kt_sc[units[k][1], rows[k], :])), nt) for k in un]
        l_pq = [jnp.where(tri_strict2, gq[k][0:CHUNK], 0.0) for k in un]
        lv = [dot3(parts(jnp.where(tri_strict2, gk[k][0:CHUNK], 0.0)),
                   bd2(parts(v_sc[units[k][1], rows[k], :])), nn) for k in un]
        tinv = [eye2 + l_pq[k] for k in un]
        xpp = [parts(l_pq[k]) for k in un]
        xpp = [parts(dot3(xpp[k], bd2(xpp[k]), nn)) for k in un]
        for it in range(5):
            tp = [parts(tinv[k]) for k in un]
            if it < 4:
                res = [dot3_shared([xpp[k], tp[k]], bd2(xpp[k]), nn) for k in un]
                xpp = [parts(res[k][0]) for k in un]
                tinv = [tinv[k] + res[k][1] for k in un]
            else:
                tinv = [tinv[k] + dot3(tp[k], bd2(xpp[k]), nn) for k in un]
        tp = [parts(tinv[k]) for k in un]
        wmat = [dot3(tp[k], bd2((lhs[k][0][0:CHUNK], lhs[k][1][0:CHUNK])), nn) for k in un]
        for k, (c, hp) in enumerate(units):
            z_sc[c, hp] = dot3(tp[k], bd2(parts(lv[k])), nn)
            for t, part in enumerate(parts(jnp.concatenate([wmat[k], rt[k]], axis=0))):
                lhs_sc[t, c, hp] = part
            arqk_sc[c, hp] = jnp.concatenate(
                [_bf(jnp.where(tri_incl2, gq[k][CHUNK:2 * CHUNK], 0.0)),
                 _bf(jnp.where(tri_incl2, gk[k][CHUNK:2 * CHUNK], 0.0))], axis=1)
        return carry

    def state_body(c, carry):
        rows = pl.ds(pl.multiple_of(c * CHUNK, CHUNK), CHUNK)
        grow = pl.ds(pl.multiple_of(c * 8, 8), 8)
        vv = [parts(v_sc[hp, rows, :]) for hp in pairs]
        s0 = [s_sc[hp] for hp in pairs]
        ps = [dot3((lhs_sc[0, c, hp], lhs_sc[1, c, hp]), bd2(parts(s0[hp])), nt)
              for hp in pairs]
        uu = [parts(ps[hp][0:CHUNK] + z_sc[c, hp]) for hp in pairs]
        for hp in pairs:
            sl = slice(hp * 128, (hp + 1) * 128)
            yo_sc[rows, sl] = ps[hp][CHUNK:2 * CHUNK] + jnp.dot(
                arqk_sc[c, hp], jnp.concatenate([bd(uu[hp][0]), bd(vv[hp][0])], axis=0),
                preferred_element_type=F32)
        for hp in pairs:
            sl = slice(hp * 128, (hp + 1) * 128)
            uv = tuple(jnp.concatenate([uu[hp][t], vv[hp][t]], axis=0) for t in range(2))
            qk = parts(jnp.concatenate([qe_sc[hp, rows, :], ke_sc[hp, rows, :]], axis=0))
            full = dot3(uv, qk, tn)
            g_end = gend_sc[grow, sl][0:1, :]
            s_sc[hp] = s0[hp] * g_end + jnp.where(
                first_head, full[0:HEAD_DIM], full[HEAD_DIM:2 * HEAD_DIM])
        return carry

    lax.fori_loop(0, nc // cpi, intra_body, 0)
    lax.fori_loop(0, nc, state_body, 0)

    for hp in range(NPAIR):
        sl = slice(hp * 128, (hp + 1) * 128)
        yh = yo_sc[:, sl]
        mean = head_sum(yh, 1) * (1.0 / HEAD_DIM)
        d = yh - mean
        var = head_sum(d * d, 1) * (1.0 / HEAD_DIM)
        yn = d * lax.rsqrt(var + RW_GN_EPS) * lnw_ref[:, sl] + lnb_ref[:, sl]
        y_ref[0, :, sl] = _bf((yn + bonus_sc[:, sl]) * g_sc[:, sl])

    @pl.when(i == pl.num_programs(1) - 1)
    def _():
        for hp in range(NPAIR):
            sp = s_sc[hp]
            sfin_ref[0, 2 * hp] = sp[:, 0:HEAD_DIM]
            sfin_ref[0, 2 * hp + 1] = sp[:, HEAD_DIM:2 * HEAD_DIM]


def _rwkv(u, shift_hist, s0, p, tt):
    b, t, _ = u.shape
    row = lambda a: a.reshape(1, -1)
    sh = shift_hist.reshape(b, 1, RW_IN)
    sh_rkv = sh[:, :, :3 * BW]
    sh_lora = sh[:, :, 3 * BW:]
    mu = p["rw_mu"]
    full = lambda shape: pl.BlockSpec(shape, lambda i, j: (0,) * len(shape))
    scr = lambda: pltpu.VMEM((NPAIR, tt, 128), F32)
    wide = lambda rows: pltpu.VMEM((rows, BW), F32)
    return pl.pallas_call(
        _rwkv_kernel,
        grid=(b, t // tt),
        in_specs=[
            pl.BlockSpec((1, tt, 3 * BW), lambda i, j: (i, j, C_RKV // (3 * BW))),
            pl.BlockSpec((1, tt, RW_LORA), lambda i, j: (i, j, C_LORA // RW_LORA)),
            pl.BlockSpec((1, 1, 3 * BW), lambda i, j: (i, 0, 0)),
            pl.BlockSpec((1, 1, RW_LORA), lambda i, j: (i, 0, 0)),
            pl.BlockSpec((1, N_HEADS, HEAD_DIM, HEAD_DIM), lambda i, j: (i, 0, 0, 0)),
            full((1, 3 * BW)), full((1, RW_LORA)),
            full((1, BW)), full((64, BW)), full((1, BW)), full((64, BW)), full((128, BW)),
            full((1, BW)), full((1, BW)),
            full((1, BW)), full((1, BW)), full((1, BW)),
        ],
        out_specs=[
            pl.BlockSpec((1, tt, BW), lambda i, j: (i, j, 0)),
            pl.BlockSpec((1, N_HEADS, HEAD_DIM, HEAD_DIM), lambda i, j: (i, 0, 0, 0)),
        ],
        out_shape=[
            jax.ShapeDtypeStruct((b, t, BW), BF16),
            jax.ShapeDtypeStruct((b, N_HEADS, HEAD_DIM, HEAD_DIM), F32),
        ],
        scratch_shapes=[
            pltpu.VMEM((1, 3 * BW), F32), pltpu.VMEM((1, RW_LORA), F32),
            pltpu.VMEM((NPAIR, HEAD_DIM, 128), F32),
            wide(tt), wide(8 * (tt // CHUNK)), wide(tt), wide(tt),
            scr(), scr(), scr(), scr(), scr(), scr(), scr(),
            pltpu.VMEM((2, tt // CHUNK, NPAIR, 2 * CHUNK, 128), BF16),
            pltpu.VMEM((tt // CHUNK, NPAIR, CHUNK, 128), F32),
            pltpu.VMEM((tt // CHUNK, NPAIR, CHUNK, 256), BF16),
        ],
        compiler_params=_cparams(("parallel", "arbitrary")),
        name="rwkv7",
    )(u, u, sh_rkv, sh_lora, s0, row(mu[:3 * BW]), row(mu[3 * BW:]),
      row(p["rw_w0"]), p["rw_w2"], row(p["rw_a0"]), p["rw_a2"], p["rw_g2"],
      row(p["rw_kk"]), row(p["rw_ka"]), row(p["rw_rk"]), row(p["rw_ln_w"]), row(p["rw_ln_b"]))


def _pool_kernel(pos0, u_ref, prev_ref, hist_ref, w_ref, scale_ref, y_ref):
    tt = u_ref.shape[1]
    i = pl.program_id(1)
    x = u_ref[0]
    prev = jnp.where(i == 0, hist_ref[0], prev_ref[0])
    s = jnp.concatenate([prev, x], axis=0)
    pos = pos0 + i * tt + lax.broadcasted_iota(jnp.int32, (tt, 1), 0)
    outs = []
    for gi, win in enumerate(POOL_WINDOWS):
        s = s + pltpu.roll(s, win // 2, 0)
        sl = slice(gi * POOL_GW, (gi + 1) * POOL_GW)
        cnt = jnp.minimum(pos + 1, win).astype(F32)
        pg = s[16:, sl] / cnt - x[:, sl]
        outs.append(_mm(pg, w_ref[gi]))
    y_ref[0] = _bf(jnp.concatenate(outs, axis=1) * scale_ref[...])


def _pool(u, hist16, pos0, w_pool, scale, tt):
    b, t, _ = u.shape
    nprev = tt // 16
    return pl.pallas_call(
        functools.partial(_pool_kernel, pos0),
        grid=(b, t // tt),
        in_specs=[
            pl.BlockSpec((1, tt, BW), lambda i, j: (i, j, C_POOL // BW)),
            pl.BlockSpec((1, 16, BW), lambda i, j: (i, jnp.maximum(j * nprev - 1, 0), C_POOL // BW)),
            pl.BlockSpec((1, 16, BW), lambda i, j: (i, 0, 0)),
            pl.BlockSpec((4, POOL_GW, POOL_GW), lambda i, j: (0, 0, 0)),
            pl.BlockSpec((1, BW), lambda i, j: (0, 0)),
        ],
        out_specs=pl.BlockSpec((1, tt, BW), lambda i, j: (i, j, 0)),
        out_shape=jax.ShapeDtypeStruct((b, t, BW), BF16),
        compiler_params=_cparams(("parallel", "parallel")),
        name="pool",
    )(u, u, hist16, w_pool, scale.reshape(1, BW))


def _swa_kernel(mask_history, q_ref, kp2_ref, kp1_ref, kc_ref, vp2_ref, vp1_ref, vc_ref, sink_ref, y_ref):
    tq = q_ref.shape[1]
    nc = tq // CHUNK
    i = pl.program_id(1)
    q = q_ref[0]
    k_all = jnp.concatenate([kp2_ref[0], kp1_ref[0], kc_ref[0]], axis=0)
    v_all = jnp.concatenate([vp2_ref[0], vp1_ref[0], vc_ref[0]], axis=0)
    nk = 3 * CHUNK
    qi = lax.broadcasted_iota(jnp.int32, (CHUNK, nk), 0)
    si = lax.broadcasted_iota(jnp.int32, (CHUNK, nk), 1)
    dist1 = jnp.abs(qi + 2 * CHUNK - si).astype(F32)
    dist = jnp.concatenate([dist1] * SWA_GROUP, axis=0)
    scol = lax.broadcasted_iota(jnp.int32, (SWA_GROUP * CHUNK, nk), 1)
    units = [(c, g) for c in range(nc) for g in range(SWA_KV)]
    bias = []
    sinks = []
    for g in range(SWA_KV):
        heads = [g * SWA_GROUP + hh for hh in range(SWA_GROUP)]
        slope = jnp.concatenate(
            [jnp.full((CHUNK, 1), 2.0 ** (-8.0 * (h + 1) / N_HEADS), F32) for h in heads], axis=0)
        bias.append(slope * dist)
        sinks.append(jnp.concatenate([jnp.full((CHUNK, 1), sink_ref[h], F32) for h in heads], axis=0))

    def scores(c, g):
        ks = k_all[c * CHUNK:c * CHUNK + nk, g * HEAD_DIM:(g + 1) * HEAD_DIM]
        qs = jnp.concatenate(
            [q[c * CHUNK:(c + 1) * CHUNK, (g * SWA_GROUP + hh) * HEAD_DIM:(g * SWA_GROUP + hh + 1) * HEAD_DIM]
             for hh in range(SWA_GROUP)], axis=0)
        return _mm_nt(qs, ks)

    def probs(s, c, g):
        s = s * (HEAD_DIM ** -0.5) - bias[g]
        if mask_history:
            s = jnp.where(scol >= (2 - (i * nc + c)) * CHUNK, s, -1e30)
        m = jnp.maximum(jnp.max(s, axis=-1, keepdims=True), sinks[g])
        p = jnp.exp(s - m)
        denom = jnp.sum(p, axis=-1, keepdims=True) + jnp.exp(sinks[g] - m)
        return _bf(p * (1.0 / denom))

    s_all = [scores(c, g) for c, g in units]
    p_all = [probs(s, c, g) for s, (c, g) in zip(s_all, units)]
    o_all = [_mm(p, v_all[c * CHUNK:c * CHUNK + nk, g * HEAD_DIM:(g + 1) * HEAD_DIM])
             for p, (c, g) in zip(p_all, units)]
    for o, (c, g) in zip(o_all, units):
        for hh in range(SWA_GROUP):
            h = g * SWA_GROUP + hh
            y_ref[0, c * CHUNK:(c + 1) * CHUNK, h * HEAD_DIM:(h + 1) * HEAD_DIM] = _bf(
                o[hh * CHUNK:(hh + 1) * CHUNK])


def _swa(u, hist_k, hist_v, sinks, tq):
    b, t, _ = u.shape
    per = tq // CHUNK
    kvw = SWA_KV * HEAD_DIM
    mask_history = hist_k is None
    cur_k = pl.BlockSpec((1, tq, kvw), lambda i, j: (i, j, C_KS // kvw))
    cur_v = pl.BlockSpec((1, tq, kvw), lambda i, j: (i, j, C_VS // kvw))
    if mask_history:
        prev = lambda d, col: pl.BlockSpec(
            (1, CHUNK, kvw), lambda i, j: (i, jnp.maximum(j * per - d, 0), col))
        specs = [prev(2, C_KS // kvw), prev(1, C_KS // kvw), cur_k,
                 prev(2, C_VS // kvw), prev(1, C_VS // kvw), cur_v]
        args = (u, u, u, u, u, u)
    else:
        assert t == tq
        hk = hist_k.reshape(b, WINDOW, kvw)
        hv = hist_v.reshape(b, WINDOW, kvw)
        hist = lambda blk: pl.BlockSpec((1, CHUNK, kvw), lambda i, j: (i, blk, 0))
        specs = [hist(0), hist(1), cur_k, hist(0), hist(1), cur_v]
        args = (hk, hk, u, hv, hv, u)
    return pl.pallas_call(
        functools.partial(_swa_kernel, mask_history),
        grid=(b, t // tq),
        in_specs=[pl.BlockSpec((1, tq, BW), lambda i, j: (i, j, C_Q // BW))] + specs
        + [pl.BlockSpec(memory_space=pltpu.SMEM)],
        out_specs=pl.BlockSpec((1, tq, BW), lambda i, j: (i, j, 0)),
        out_shape=jax.ShapeDtypeStruct((b, t, BW), BF16),
        compiler_params=_cparams(("parallel", "parallel")),
        name="swa",
    )(u, *args, sinks)


def _ssd_kernel(z_ref, x_ref, b_ref, c_ref, dt_ref, hx_ref, hb_ref, hc_ref, s0_ref,
                cwx_ref, cwb_ref, cwc_ref, cbx_ref, cbb_ref, cbc_ref,
                dtb_ref, alog_ref, dsk_ref, nw_ref,
                y_ref, sfin_ref,
                px_sc, pb_sc, pc_sc, s_sc, xa_sc, ba_sc, ca_sc, acs_sc, gend_sc,
                acsx_sc, eout_sc, y_sc, xdt_sc, xde_sc):
    tt = x_ref.shape[1]
    nc = tt // CHUNK
    i = pl.program_id(1)

    @pl.when(i == 0)
    def _():
        px_sc[...] = hx_ref[0]
        pb_sc[...] = hb_ref[0]
        pc_sc[...] = hc_ref[0]
        s_sc[...] = s0_ref[0].reshape(N_HEADS * HEAD_DIM, SSM_STATE)

    rid8 = lax.broadcasted_iota(jnp.int32, (8, 1), 0)

    def conv_silu(x, prev8, w_ref, b_ref_):
        acc = None
        for wi in range(SSM_CONV):
            sh = SSM_CONV - 1 - wi
            if sh == 0:
                xs = x
            else:
                rolled = pltpu.roll(x, sh, 0)
                top = jnp.where(rid8 < sh, pltpu.roll(prev8, sh, 0), rolled[0:8])
                xs = jnp.concatenate([top, rolled[8:]], axis=0)
            term = xs * w_ref[wi:wi + 1, :]
            acc = (b_ref_[...] + term) if acc is None else acc + term
        return _silu(acc)

    xr = x_ref[0]
    br = b_ref[0]
    cr = c_ref[0]
    xa_sc[...] = conv_silu(xr, px_sc[...], cwx_ref, cbx_ref)
    ba_sc[...] = conv_silu(br, pb_sc[...], cwb_ref, cbb_ref)
    ca_sc[...] = conv_silu(cr, pc_sc[...], cwc_ref, cbc_ref)
    px_sc[...] = xr[tt - 8:tt, :]
    pb_sc[...] = br[tt - 8:tt, :]
    pc_sc[...] = cr[tt - 8:tt, :]
    hl = 128
    dt = _softplus(dt_ref[0][:, 0:hl] + dtb_ref[...])
    ad = -jnp.exp(alog_ref[...]) * dt

    tri_incl_bf = _tri(CHUNK).astype(BF16)
    sel16 = (lax.broadcasted_iota(jnp.int32, (N_HEADS, hl), 0)
             == lax.broadcasted_iota(jnp.int32, (N_HEADS, hl), 1)).astype(BF16)
    max_terms = 3
    srow = lax.broadcasted_iota(jnp.int32, (hl, BW), 0)
    spread = (((srow & (N_HEADS - 1)) == lax.broadcasted_iota(jnp.int32, (hl, BW), 1) // HEAD_DIM)
              & (srow < max_terms * N_HEADS)).astype(BF16)
    head_lane = lax.broadcasted_iota(jnp.int32, (1, hl), 1) < N_HEADS

    def per_channel(x, terms):
        rem = jnp.where(head_lane, x, 0.0)
        packed = None
        for t in range(terms):
            part = _bf(rem).astype(F32)
            if t < terms - 1:
                rem = rem - part
            placed = part if t == 0 else pltpu.roll(part, t * N_HEADS, 1)
            packed = placed if packed is None else packed + placed
        return jnp.dot(_bf(packed), spread, preferred_element_type=F32)

    acs = jnp.concatenate(
        [_mm_sel(tri_incl_bf, ad[c * CHUNK:(c + 1) * CHUNK, :]) for c in range(nc)], axis=0)
    a_last = jnp.concatenate(
        [jnp.broadcast_to(acs[(c + 1) * CHUNK - 1:(c + 1) * CHUNK, :], (CHUNK, hl)) for c in range(nc)], axis=0)
    acs_sc[...] = acs
    acsx_sc[...] = per_channel(acs, 3)
    eout_sc[...] = per_channel(jnp.exp(acs), 2)
    gend_sc[...] = jnp.exp(jnp.concatenate(
        [jnp.broadcast_to(acs[(c + 1) * CHUNK - 1:(c + 1) * CHUNK, :], (8, hl)) for c in range(nc)], axis=0))
    xdt = xa_sc[...] * per_channel(dt, 2)
    xdt_sc[...] = _bf(xdt)
    xde_sc[...] = _bf(xdt * per_channel(jnp.exp(a_last - acs), 2))

    lane = lax.broadcasted_iota(jnp.int32, (CHUNK, 128), 1)
    row = lax.broadcasted_iota(jnp.int32, (CHUNK, 128), 0)
    tri_incl2 = row >= (lane & (HEAD_DIM - 1))
    r128 = lax.broadcasted_iota(jnp.int32, (128, 128), 0)
    c128 = lax.broadcasted_iota(jnp.int32, (128, 128), 1)
    same_head = (r128 < HEAD_DIM) == (c128 < HEAD_DIM)
    hpg = N_HEADS // SSM_GROUPS
    rpg = hpg * HEAD_DIM

    def chunk_body(c, carry):
        rows = pl.ds(pl.multiple_of(c * CHUNK, CHUNK), CHUNK)
        grow = pl.ds(pl.multiple_of(c * 8, 8), 8)
        bc = _bf(ba_sc[rows, :])
        cc = _bf(ca_sc[rows, :])
        acs_t = _mm_sel_nt(sel16, acs_sc[rows, :])
        g_end = gend_sc[grow, :][0:1, :]
        gsl = [slice(gi * SSM_STATE, (gi + 1) * SSM_STATE) for gi in range(SSM_GROUPS)]
        scores = [_mm_nt(cc[:, gs], bc[:, gs]) for gs in gsl]
        scores2 = [jnp.concatenate([s, s], axis=1) for s in scores]
        y_off = [_mm_nt(cc[:, gsl[gi]], s_sc[gi * rpg:(gi + 1) * rpg, :]) for gi in range(SSM_GROUPS)]
        upd = [_mm_tn(xde_sc[rows, gi * rpg:(gi + 1) * rpg], bc[:, gsl[gi]]) for gi in range(SSM_GROUPS)]
        y_diag = []
        for hp in range(N_HEADS // 2):
            sl = slice(hp * 128, (hp + 1) * 128)
            seg = acsx_sc[rows, sl] - jnp.concatenate([acs_t[2 * hp:2 * hp + 1, :],
                                                       acs_t[2 * hp + 1:2 * hp + 2, :]], axis=1)
            m = scores2[(2 * hp) // hpg] * jnp.exp(jnp.where(tri_incl2, seg, -1e30))
            xd = xdt_sc[rows, sl]
            y_diag.append(_mm(m, jnp.where(same_head, jnp.concatenate([xd, xd], axis=0),
                                           jnp.zeros((), BF16))))
        for h in range(N_HEADS):
            gi, hl_ = divmod(h, hpg)
            hrows = slice(h * HEAD_DIM, (h + 1) * HEAD_DIM)
            s_sc[hrows, :] = (s_sc[hrows, :] * g_end[:, h:h + 1]
                              + upd[gi][hl_ * HEAD_DIM:(hl_ + 1) * HEAD_DIM, :])
        y_sc[rows, :] = (jnp.concatenate(y_diag, axis=1)
                         + jnp.concatenate(y_off, axis=1) * eout_sc[rows, :])
        return carry

    lax.fori_loop(0, nc, chunk_body, 0)

    y = (y_sc[...] + xa_sc[...] * dsk_ref[...]) * _silu(z_ref[0])
    gw = BW // SSM_GROUPS
    parts = []
    for gi in range(SSM_GROUPS):
        yg = y[:, gi * gw:(gi + 1) * gw]
        parts.append(yg * lax.rsqrt(jnp.mean(yg * yg, axis=-1, keepdims=True) + NORM_EPS))
    y_ref[0] = _bf(jnp.concatenate(parts, axis=1) * nw_ref[...])

    @pl.when(i == pl.num_programs(1) - 1)
    def _():
        sfin_ref[0] = s_sc[...].reshape(N_HEADS, HEAD_DIM, SSM_STATE)


def _pad_rows8(a):
    return jnp.pad(a, ((0, 0), (8 - a.shape[1], 0), (0, 0)))


def _pad_lanes(a, n):
    return jnp.pad(a.reshape(1, -1), ((0, 0), (0, n - a.shape[-1])))


def _ssd(u, conv_hist, s0, p, tt):
    b, t, _ = u.shape
    h8 = _pad_rows8(conv_hist)
    hx, hb, hc = h8[:, :, :BW], h8[:, :, BW:BW + SSM_BC], h8[:, :, BW + SSM_BC:]
    cw, cb = p["ssm_conv_w"], p["ssm_conv_b"].reshape(1, -1)
    full = lambda shape: pl.BlockSpec(shape, lambda i, j: (0,) * len(shape))
    col = lambda w, c0: pl.BlockSpec((1, tt, w), lambda i, j: (i, j, c0 // w))
    hist = lambda w: pl.BlockSpec((1, 8, w), lambda i, j: (i, 0, 0))
    return pl.pallas_call(
        _ssd_kernel,
        grid=(b, t // tt),
        in_specs=[
            col(BW, C_Z), col(BW, C_X), col(SSM_BC, C_B), col(SSM_BC, C_C), col(SSM_BC, C_DT),
            hist(BW), hist(SSM_BC), hist(SSM_BC),
            pl.BlockSpec((1, N_HEADS, HEAD_DIM, SSM_STATE), lambda i, j: (i, 0, 0, 0)),
            full((SSM_CONV, BW)), full((SSM_CONV, SSM_BC)), full((SSM_CONV, SSM_BC)),
            full((1, BW)), full((1, SSM_BC)), full((1, SSM_BC)),
            full((1, 128)), full((1, 128)), full((1, BW)), full((1, BW)),
        ],
        out_specs=[
            pl.BlockSpec((1, tt, BW), lambda i, j: (i, j, 0)),
            pl.BlockSpec((1, N_HEADS, HEAD_DIM, SSM_STATE), lambda i, j: (i, 0, 0, 0)),
        ],
        out_shape=[
            jax.ShapeDtypeStruct((b, t, BW), BF16),
            jax.ShapeDtypeStruct((b, N_HEADS, HEAD_DIM, SSM_STATE), F32),
        ],
        scratch_shapes=[
            pltpu.VMEM((8, BW), F32), pltpu.VMEM((8, SSM_BC), F32), pltpu.VMEM((8, SSM_BC), F32),
            pltpu.VMEM((N_HEADS * HEAD_DIM, SSM_STATE), F32),
            pltpu.VMEM((tt, BW), F32), pltpu.VMEM((tt, SSM_BC), F32), pltpu.VMEM((tt, SSM_BC), F32),
            pltpu.VMEM((tt, 128), F32), pltpu.VMEM((8 * (tt // CHUNK), 128), F32),
            pltpu.VMEM((tt, BW), F32), pltpu.VMEM((tt, BW), F32), pltpu.VMEM((tt, BW), F32),
            pltpu.VMEM((tt, BW), BF16), pltpu.VMEM((tt, BW), BF16),
        ],
        compiler_params=_cparams(("parallel", "arbitrary")),
        name="ssd",
    )(u, u, u, u, u, hx, hb, hc, s0,
      cw[:, :BW], cw[:, BW:BW + SSM_BC], cw[:, BW + SSM_BC:],
      cb[:, :BW], cb[:, BW:BW + SSM_BC], cb[:, BW + SSM_BC:],
      _pad_lanes(p["ssm_dt_bias"], 128), _pad_lanes(p["ssm_a_log"], 128),
      jnp.repeat(p["ssm_d"], HEAD_DIM).reshape(1, BW), p["ssm_norm"].reshape(1, BW))


def _prep_w_in(w_in):
    w_in = w_in.astype(BF16)
    c = lambda a, n: w_in[:, :, a:a + n]
    ssm = _N_SSM
    parts = [
        c(0, 3 * BW),
        c(ssm, BW),
        c(_N_POOL, BW),
        c(_N_SWA, BW),
        c(ssm + BW, BW),
        c(3 * BW, RW_LORA),
        c(_N_SWA + BW, 2 * SWA_KV * HEAD_DIM),
        c(ssm + 2 * BW, 2 * SSM_BC + N_HEADS),
    ]
    parts.append(jnp.zeros(w_in.shape[:2] + (U_COLS - IN_COLS,), BF16))
    return jnp.concatenate(parts, axis=-1)


def _run_group(x, mods, st, p, final_norm, wts, pos0, bb, tt, tt_branch):
    b, t, _ = x.shape
    kvw = SWA_KV * HEAD_DIM
    outs = {k: [] for k in ("rwkv", "shift", "pool", "k", "v", "ssm", "conv")}
    for l in range(DEPTH):
        pl_ = {k: v[l] for k, v in p.items()}
        mod = mods[l]
        u, h = _inproj(x, pl_["norm_mix"], mod, wts["w_in"], l, bb, tt)
        if st is None:
            shift_hist = jnp.zeros((b, 1, RW_IN), F32)
            s_rwkv = jnp.zeros((b, N_HEADS, HEAD_DIM, HEAD_DIM), F32)
            pool_hist = jnp.zeros((b, POOL_HIST, BW), F32)
            hk = hv = None
            s_ssm = jnp.zeros((b, N_HEADS, HEAD_DIM, SSM_STATE), F32)
            conv_hist = jnp.zeros((b, SSM_CONV - 1, SSM_CONV_DIM), F32)
        else:
            shift_hist, s_rwkv, pool_hist = st["shift"][l], st["rwkv"][l], st["pool"][l]
            hk, hv, s_ssm, conv_hist = st["k"][l], st["v"][l], st["ssm"][l], st["conv"][l]
        y_a, n_rwkv = _rwkv(u, shift_hist, s_rwkv, pl_, min(tt_branch, 256))
        hist16 = jnp.pad(pool_hist, ((0, 0), (1, 0), (0, 0)))
        y_b = _pool(u, hist16, pos0, wts["pool_w"][l], pl_["pool_scale"], tt_branch)
        y_c = _swa(u, hk, hv, pl_["swa_sinks"], min(tt_branch, 256))
        y_d, n_ssm = _ssd(u, conv_hist, s_ssm, pl_, tt_branch)
        merged = _merge(h, (y_a, y_b, y_c, y_d), wts["w_gate"], wts["w_branch"], l, bb, tt)
        x = _outproj(merged, wts["w_out"], l, x, mod, bb, tt)
        x = _ffn(x, pl_["norm_ffn"], mod, wts["w_up"], wts["w_down"], l, final_norm, l == DEPTH - 1,
                 bb, tt)

        outs["rwkv"].append(n_rwkv)
        outs["shift"].append(jnp.concatenate(
            [u[:, t - 1:, C_RKV:C_RKV + 3 * BW], u[:, t - 1:, C_LORA:C_LORA + RW_LORA]], axis=-1))
        outs["pool"].append(u[:, t - POOL_HIST:, C_POOL:C_POOL + BW])
        k_new = u[:, :, C_KS:C_KS + kvw]
        v_new = u[:, :, C_VS:C_VS + kvw]
        if hk is not None:
            k_new = jnp.concatenate([hk.reshape(b, WINDOW, kvw), k_new], axis=1)
            v_new = jnp.concatenate([hv.reshape(b, WINDOW, kvw), v_new], axis=1)
        outs["k"].append(k_new[:, -WINDOW:].reshape(b, WINDOW, SWA_KV, HEAD_DIM))
        outs["v"].append(v_new[:, -WINDOW:].reshape(b, WINDOW, SWA_KV, HEAD_DIM))
        outs["ssm"].append(n_ssm)
        outs["conv"].append(jnp.concatenate(
            [u[:, t - 3:, C_X:C_X + BW], u[:, t - 3:, C_B:C_B + 2 * SSM_BC]], axis=-1))
    order = ("rwkv", "shift", "pool", "k", "v", "ssm", "conv")
    return x, tuple(jnp.stack(outs[k]) for k in order)


def kernel(x_prompt, x_sample, state_rwkv, state_rwkv_shift, state_pool, cache_swa_k, cache_swa_v,
           state_ssm, state_ssm_conv, c_prompt, c_sample, ada_w, ada_b, norm_mix, norm_ffn, w_in,
           rw_mu, rw_w0, rw_w2, rw_a0, rw_a2, rw_g2, rw_kk, rw_ka, rw_rk, rw_ln_w, rw_ln_b, pool_w,
           pool_scale, swa_sinks, ssm_conv_w, ssm_conv_b, ssm_dt_bias, ssm_a_log, ssm_d, ssm_norm,
           w_gate, w_branch, w_out, w_up, w_down, final_norm):
    bp, tp, _ = x_prompt.shape
    bs, ts, _ = x_sample.shape
    p = dict(norm_mix=norm_mix, norm_ffn=norm_ffn, rw_mu=rw_mu, rw_w0=rw_w0, rw_w2=rw_w2, rw_a0=rw_a0,
             rw_a2=rw_a2, rw_g2=rw_g2, rw_kk=rw_kk, rw_ka=rw_ka, rw_rk=rw_rk, rw_ln_w=rw_ln_w,
             rw_ln_b=rw_ln_b, pool_scale=pool_scale, swa_sinks=swa_sinks, ssm_conv_w=ssm_conv_w,
             ssm_conv_b=ssm_conv_b, ssm_dt_bias=ssm_dt_bias, ssm_a_log=ssm_a_log, ssm_d=ssm_d,
             ssm_norm=ssm_norm)
    wts = dict(w_in=_prep_w_in(w_in), pool_w=pool_w.astype(BF16), w_gate=w_gate.astype(BF16),
               w_branch=w_branch.astype(BF16), w_out=w_out.astype(BF16), w_up=w_up.astype(BF16),
               w_down=w_down.astype(BF16))

    nb = bp + bs
    nb_pad = -(-nb // 8) * 8
    c_all = jnp.pad(jnp.concatenate([c_prompt, c_sample], axis=0), ((0, nb_pad - nb), (0, 0)))
    mod_all = _adaln(c_all, ada_w, ada_b)
    mods_p = mod_all[:, :bp].reshape(DEPTH, bp, 6, 1, D_MODEL)
    mods_s = mod_all[:, bp:nb].reshape(DEPTH, bs, 6, 1, D_MODEL)

    y_prompt, st_p = _run_group(x_prompt, mods_p, None, p, final_norm, wts, 0,
                                1, min(tp, 1024), min(tp, 512))
    st_s = dict(rwkv=state_rwkv, shift=state_rwkv_shift, pool=state_pool, k=cache_swa_k,
                v=cache_swa_v, ssm=state_ssm, conv=state_ssm_conv)
    y_sample, st_o = _run_group(x_sample, mods_s, st_s, p, final_norm, wts, PAST_LEN, bs, ts, ts)
    return (y_prompt, y_sample) + st_p + st_o
```

```python
import functools
import math

import jax
import jax.numpy as jnp
from jax import lax
from jax.experimental import pallas as pl
from jax.experimental.pallas import tpu as pltpu

F32 = jnp.float32
BF16 = jnp.bfloat16

D_MODEL = 2048
DEPTH = 4
PAST_LEN = 1024
CHUNK = 64
HEAD_DIM = 64
BW = D_MODEL // 2
D_FF = 4 * D_MODEL
NORM_EPS = 1e-6
N_HEADS = BW // HEAD_DIM
RW_LORA = 256
RW_IN = 3 * BW + RW_LORA
RW_GN_EPS = HEAD_DIM * 1e-5
POOL_WINDOWS = (2, 4, 8, 16)
POOL_GW = BW // 4
POOL_HIST = 15
SWA_KV = 4
SWA_GROUP = N_HEADS // SWA_KV
WINDOW = 128
SSM_GROUPS = 2
SSM_STATE = 128
SSM_CONV = 4
SSM_BC = SSM_GROUPS * SSM_STATE
SSM_CONV_DIM = BW + 2 * SSM_BC

C_RKV = 0
C_Z = 3072
C_POOL = 4096
C_Q = 5120
C_X = 6144
C_LORA = 7168
C_KS = 7424
C_VS = 7680
C_B = 7936
C_C = 8192
C_DT = 8448
U_COLS = 8704

_N_POOL = RW_IN
_N_SWA = _N_POOL + BW
_N_SSM = _N_SWA + (N_HEADS + 2 * SWA_KV) * HEAD_DIM
IN_COLS = _N_SSM + BW + SSM_CONV_DIM + N_HEADS

VMEM_LIMIT = 56 * 1024 * 1024


def _cparams(sem):
    return pltpu.CompilerParams(dimension_semantics=sem, vmem_limit_bytes=VMEM_LIMIT)


def _bf(x):
    return x.astype(BF16)


def _mm(a, b):
    return jnp.dot(_bf(a), _bf(b), preferred_element_type=F32)


def _mm_nt(a, b):
    return lax.dot_general(_bf(a), _bf(b), (((1,), (1,)), ((), ())), preferred_element_type=F32)


def _mm_tn(a, b):
    return lax.dot_general(_bf(a), _bf(b), (((0,), (0,)), ((), ())), preferred_element_type=F32)


def _split3(x):
    hi = _bf(x)
    r1 = x - hi.astype(F32)
    mid = _bf(r1)
    lo = _bf(r1 - mid.astype(F32))
    return hi, mid, lo


def _split(x, terms):
    parts = []
    for _ in range(terms - 1):
        hi = _bf(x)
        parts.append(hi)
        x = x - hi.astype(F32)
    parts.append(_bf(x))
    return parts


def _mm_sel(c, x, terms=3):
    acc = None
    for part in _split(x, terms):
        d = jnp.dot(c, part, preferred_element_type=F32)
        acc = d if acc is None else acc + d
    return acc


def _mm_sel_nt(c, x):
    hi, mid, lo = _split3(x)
    d = lambda p: lax.dot_general(c, p, (((1,), (1,)), ((), ())), preferred_element_type=F32)
    return d(hi) + d(mid) + d(lo)


def _softplus(x):
    return jnp.maximum(x, 0.0) + jnp.log1p(jnp.exp(-jnp.abs(x)))


def _sigmoid(x):
    return 0.5 * jnp.tanh(0.5 * x) + 0.5


def _silu(x):
    return x * _sigmoid(x)


def _tri(n, strict=False):
    r = lax.broadcasted_iota(jnp.int32, (n, n), 0)
    c = lax.broadcasted_iota(jnp.int32, (n, n), 1)
    return (r > c) if strict else (r >= c)


def _adaln_kernel(c_ref, w_ref, b_ref, o_ref):
    s = _silu(c_ref[...])
    o_ref[0] = _mm(s, w_ref[0]) + b_ref[0]


def _adaln(c_all, ada_w, ada_b):
    nb = c_all.shape[0]
    tn = 1024
    n_out = ada_w.shape[2]
    return pl.pallas_call(
        _adaln_kernel,
        grid=(DEPTH, n_out // tn),
        in_specs=[
            pl.BlockSpec((nb, D_MODEL), lambda l, n: (0, 0)),
            pl.BlockSpec((1, D_MODEL, tn), lambda l, n: (l, 0, n)),
            pl.BlockSpec((1, 1, tn), lambda l, n: (l, 0, n)),
        ],
        out_specs=pl.BlockSpec((1, nb, tn), lambda l, n: (l, 0, n)),
        out_shape=jax.ShapeDtypeStruct((DEPTH, nb, n_out), F32),
        compiler_params=_cparams(("parallel", "parallel")),
        name="adaln",
    )(c_all, ada_w, ada_b.reshape(DEPTH, 1, n_out))


def _norm_mod(x, g, sc, sh):
    y = x * lax.rsqrt(jnp.mean(x * x, axis=-1, keepdims=True) + NORM_EPS)
    return (y * g) * (1.0 + sc) + sh


def _inproj_kernel(x_ref, g_ref, sc_ref, sh_ref, w_ref, u_ref, h_ref):
    bb, tt, _ = x_ref.shape

    @pl.when(pl.program_id(2) == 0)
    def _():
        h = _norm_mod(x_ref[...], g_ref[...], sc_ref[:, 0], sh_ref[:, 0])
        h_ref[...] = _bf(h)

    h = h_ref[...].reshape(bb * tt, D_MODEL)
    u_ref[...] = jnp.dot(h, w_ref[...], preferred_element_type=F32).reshape(u_ref.shape)


def _inproj(x, g, mod, w, l, bb, tt):
    b, t, _ = x.shape
    tn = U_COLS // 4
    if bb * tt > 512:
        tt = 512 // bb
    return pl.pallas_call(
        _inproj_kernel,
        grid=(b // bb, t // tt, U_COLS // tn),
        in_specs=[
            pl.BlockSpec((bb, tt, D_MODEL), lambda i, j, n: (i, j, 0)),
            pl.BlockSpec((1, 1, D_MODEL), lambda i, j, n: (0, 0, 0)),
            pl.BlockSpec((bb, 1, 1, D_MODEL), lambda i, j, n: (i, 1, 0, 0)),
            pl.BlockSpec((bb, 1, 1, D_MODEL), lambda i, j, n: (i, 0, 0, 0)),
            pl.BlockSpec((None, D_MODEL, tn), lambda i, j, n: (l, 0, n)),
        ],
        out_specs=[
            pl.BlockSpec((bb, tt, tn), lambda i, j, n: (i, j, n)),
            pl.BlockSpec((bb, tt, D_MODEL), lambda i, j, n: (i, j, 0)),
        ],
        out_shape=[
            jax.ShapeDtypeStruct((b, t, U_COLS), F32),
            jax.ShapeDtypeStruct((b, t, D_MODEL), BF16),
        ],
        compiler_params=_cparams(("parallel", "parallel", "arbitrary")),
        name="inproj",
    )(x, g.reshape(1, 1, D_MODEL), mod, mod, w)


def _merge_kernel(h_ref, ya_ref, yb_ref, yc_ref, yd_ref, wg_ref, wb_ref, o_ref):
    bb, tt, _ = h_ref.shape
    m = bb * tt
    h = h_ref[...].reshape(m, D_MODEL)
    acc = None
    for i, y_ref in enumerate((ya_ref, yb_ref, yc_ref, yd_ref)):
        gate = _sigmoid(jnp.dot(h, wg_ref[i], preferred_element_type=F32))
        br = jnp.dot(y_ref[...].reshape(m, BW), wb_ref[i], preferred_element_type=F32)
        acc = gate * br if acc is None else acc + gate * br
    o_ref[...] = _bf(acc).reshape(o_ref.shape)


def _merge(h, ys, wg, wb, l, bb, tt):
    b, t, _ = h.shape
    tn = 256
    yspec = pl.BlockSpec((bb, tt, BW), lambda i, j, n: (i, j, 0))
    return pl.pallas_call(
        _merge_kernel,
        grid=(b // bb, t // tt, D_MODEL // tn),
        in_specs=[
            pl.BlockSpec((bb, tt, D_MODEL), lambda i, j, n: (i, j, 0)),
            yspec, yspec, yspec, yspec,
            pl.BlockSpec((None, 4, D_MODEL, tn), lambda i, j, n: (l, 0, 0, n)),
            pl.BlockSpec((None, 4, BW, tn), lambda i, j, n: (l, 0, 0, n)),
        ],
        out_specs=pl.BlockSpec((bb, tt, tn), lambda i, j, n: (i, j, n)),
        out_shape=jax.ShapeDtypeStruct((b, t, D_MODEL), BF16),
        compiler_params=_cparams(("parallel", "parallel", "arbitrary")),
        name="merge",
    )(h, *ys, wg, wb)


def _outproj_kernel(m_ref, w_ref, x_ref, g_ref, o_ref):
    bb, tt, _ = m_ref.shape
    y = jnp.dot(m_ref[...].reshape(bb * tt, D_MODEL), w_ref[...], preferred_element_type=F32)
    o_ref[...] = x_ref[...] + g_ref[:, 0] * y.reshape(o_ref.shape)


def _outproj(merged, w, l, x, mod, bb, tt):
    b, t, _ = x.shape
    tn = 1024
    return pl.pallas_call(
        _outproj_kernel,
        grid=(b // bb, t // tt, D_MODEL // tn),
        in_specs=[
            pl.BlockSpec((bb, tt, D_MODEL), lambda i, j, n: (i, j, 0)),
            pl.BlockSpec((None, D_MODEL, tn), lambda i, j, n: (l, 0, n)),
            pl.BlockSpec((bb, tt, tn), lambda i, j, n: (i, j, n)),
            pl.BlockSpec((bb, 1, 1, tn), lambda i, j, n: (i, 2, 0, n)),
        ],
        out_specs=pl.BlockSpec((bb, tt, tn), lambda i, j, n: (i, j, n)),
        out_shape=jax.ShapeDtypeStruct((b, t, D_MODEL), F32),
        compiler_params=_cparams(("parallel", "parallel", "arbitrary")),
        name="outproj",
    )(merged, w, x, mod)


def _ffn_kernel(final, x_ref, g_ref, sc_ref, sh_ref, gate_ref, wu_ref, wd_ref, fg_ref, o_ref, h_sc):
    bb, tt, _ = x_ref.shape
    m = bb * tt
    f = pl.program_id(2)

    @pl.when(f == 0)
    def _():
        h = _norm_mod(x_ref[...], g_ref[...], sc_ref[:, 0], sh_ref[:, 0])
        h_sc[...] = _bf(h).reshape(m, D_MODEL)
        o_ref[...] = jnp.zeros(o_ref.shape, F32)

    a = jnp.dot(h_sc[...], wu_ref[...], preferred_element_type=F32)
    a = jnp.square(jnp.maximum(a, 0.0))
    o_ref[...] += jnp.dot(_bf(a), wd_ref[...], preferred_element_type=F32).reshape(o_ref.shape)

    @pl.when(f == pl.num_programs(2) - 1)
    def _():
        y = x_ref[...] + gate_ref[:, 0] * o_ref[...]
        if final:
            y = (y * lax.rsqrt(jnp.mean(y * y, axis=-1, keepdims=True) + NORM_EPS)) * fg_ref[...]
        o_ref[...] = y


def _ffn(x, g, mod, wu, wd, l, final_g, final, bb, tt):
    b, t, _ = x.shape
    tf = 1024
    if bb * tt > 512:
        tt = 512 // bb
    vec = pl.BlockSpec((1, 1, D_MODEL), lambda i, j, f: (0, 0, 0))
    return pl.pallas_call(
        functools.partial(_ffn_kernel, final),
        grid=(b // bb, t // tt, D_FF // tf),
        in_specs=[
            pl.BlockSpec((bb, tt, D_MODEL), lambda i, j, f: (i, j, 0)),
            vec,
            pl.BlockSpec((bb, 1, 1, D_MODEL), lambda i, j, f: (i, 4, 0, 0)),
            pl.BlockSpec((bb, 1, 1, D_MODEL), lambda i, j, f: (i, 3, 0, 0)),
            pl.BlockSpec((bb, 1, 1, D_MODEL), lambda i, j, f: (i, 5, 0, 0)),
            pl.BlockSpec((None, D_MODEL, tf), lambda i, j, f: (l, 0, f)),
            pl.BlockSpec((None, tf, D_MODEL), lambda i, j, f: (l, f, 0)),
            vec,
        ],
        out_specs=pl.BlockSpec((bb, tt, D_MODEL), lambda i, j, f: (i, j, 0)),
        out_shape=jax.ShapeDtypeStruct((b, t, D_MODEL), F32),
        scratch_shapes=[pltpu.VMEM((bb * tt, D_MODEL), BF16)],
        compiler_params=_cparams(("parallel", "parallel", "arbitrary")),
        name="ffn",
    )(x, g.reshape(1, 1, D_MODEL), mod, mod, mod, wu, wd, final_g.reshape(1, 1, D_MODEL))


NPAIR = N_HEADS // 2


def _rwkv_kernel(rkv_ref, lora_ref, sh_rkv_ref, sh_lora_ref, s0_ref, mu_rkv_ref, mu_lora_ref,
                 w0_ref, w2_ref, a0_ref, a2_ref, g2_ref, kkp_ref, ka_ref, rk_ref, lnw_ref, lnb_ref,
                 y_ref, sfin_ref,
                 prev_rkv, prev_lora, s_sc, g_sc, gend_sc, bonus_sc, yo_sc,
                 rt_sc, pt_sc, qt_sc, kt_sc, qe_sc, ke_sc, v_sc, lhs_sc, z_sc, arqk_sc):
    tt = rkv_ref.shape[1]
    nc = tt // CHUNK
    i = pl.program_id(1)

    @pl.when(i == 0)
    def _():
        prev_rkv[...] = sh_rkv_ref[0]
        prev_lora[...] = sh_lora_ref[0]
        for hp in range(NPAIR):
            s_sc[hp] = jnp.concatenate([s0_ref[0, 2 * hp], s0_ref[0, 2 * hp + 1]], axis=1)

    rid8 = lax.broadcasted_iota(jnp.int32, (8, 1), 0)

    def tshift(x, prev_row, mu):
        rolled = pltpu.roll(x, 1, 0)
        prev = jnp.concatenate([jnp.where(rid8 == 0, prev_row, rolled[0:8]), rolled[8:]], axis=0)
        return x + (prev - x) * mu

    u = rkv_ref[0]
    ul = lora_ref[0]
    xs = tshift(u, prev_rkv[...], mu_rkv_ref[...])
    xl = tshift(ul, prev_lora[...], mu_lora_ref[...])
    prev_rkv[...] = u[tt - 1:tt, :]
    prev_lora[...] = ul[tt - 1:tt, :]

    r = xs[:, 0:BW]
    k = xs[:, BW:2 * BW]
    v = xs[:, 2 * BW:3 * BW]
    wd = xl[:, 0:64]
    ad = xl[:, 64:128]
    gd = xl[:, 128:256]
    lw = -math.exp(-0.5) * _sigmoid(w0_ref[...] + _mm(jnp.tanh(wd), w2_ref[...]))
    a = _sigmoid(a0_ref[...] + _mm(ad, a2_ref[...]))
    g_sc[...] = _mm(_sigmoid(gd), g2_ref[...])
    kkr = k * kkp_ref[...]
    kh = k * (1.0 + (a - 1.0) * ka_ref[...])

    lane = lax.broadcasted_iota(jnp.int32, (CHUNK, 128), 1)
    row = lax.broadcasted_iota(jnp.int32, (CHUNK, 128), 0)
    col_in_head = lane & (HEAD_DIM - 1)
    tri_strict2 = row > col_in_head
    tri_incl2 = row >= col_in_head
    eye2 = (row == col_in_head).astype(F32)
    first_head = lane < HEAD_DIM
    r128 = lax.broadcasted_iota(jnp.int32, (128, 128), 0)
    c128 = lax.broadcasted_iota(jnp.int32, (128, 128), 1)
    same_head = (r128 < HEAD_DIM) == (c128 < HEAD_DIM)
    ones_bd = same_head.astype(BF16)
    tri_incl_bf = _tri(CHUNK).astype(BF16)

    def head_sum(x, terms):
        acc = None
        for part in _split(x, terms):
            d = jnp.dot(part, ones_bd, preferred_element_type=F32)
            acc = d if acc is None else acc + d
        return acc

    cl = jnp.concatenate(
        [_mm_sel(tri_incl_bf, lw[c * CHUNK:(c + 1) * CHUNK, :], terms=2) for c in range(nc)], axis=0)
    cl_last = jnp.concatenate(
        [jnp.broadcast_to(cl[(c + 1) * CHUNK - 1:(c + 1) * CHUNK, :], (CHUNK, BW)) for c in range(nc)], axis=0)
    e_in = jnp.exp(cl)
    e_ex = jnp.exp(cl - lw)
    e_inv = jnp.exp(-cl)
    e_end = jnp.exp(cl_last - cl)
    gend_sc[...] = jnp.exp(jnp.concatenate(
        [jnp.broadcast_to(cl[(c + 1) * CHUNK - 1:(c + 1) * CHUNK, :], (8, BW)) for c in range(nc)], axis=0))
    for hp in range(NPAIR):
        sl = slice(hp * 128, (hp + 1) * 128)
        kk = kkr[:, sl]
        kk = kk * lax.rsqrt(jnp.maximum(head_sum(kk * kk, 2), 1e-24))
        q = kk * a[:, sl]
        rt_sc[hp] = r[:, sl] * e_in[:, sl]
        pt_sc[hp] = -kk * e_ex[:, sl]
        qt_sc[hp] = q * e_inv[:, sl]
        kt_sc[hp] = kh[:, sl] * e_inv[:, sl]
        qe_sc[hp] = q * e_end[:, sl]
        ke_sc[hp] = kh[:, sl] * e_end[:, sl]
        v_sc[hp] = v[:, sl]
        bonus_sc[:, sl] = head_sum(r[:, sl] * kh[:, sl] * rk_ref[:, sl], 1) * v[:, sl]

    def bd(x):
        return jnp.where(same_head, jnp.concatenate([x, x], axis=0), jnp.zeros((), x.dtype))

    def parts(x):
        hi = _bf(x)
        return hi, _bf(x - hi.astype(F32))

    def bd2(p):
        return bd(p[0]), bd(p[1])

    nn, nt, tn = ((1,), (0,)), ((1,), (1,)), ((0,), (0,))

    def dot3_shared(a_list, b, dims):
        d = lambda x, y: lax.dot_general(x, y, (dims, ((), ())), preferred_element_type=F32)
        ax = 1 if dims == tn else 0
        m = a_list[0][0].shape[ax]
        big = d(jnp.concatenate([t for a in a_list for t in a], axis=ax), b[0])
        small = d(jnp.concatenate([a[0] for a in a_list], axis=ax), b[1])
        return [big[2 * i * m:(2 * i + 1) * m] + big[(2 * i + 1) * m:(2 * i + 2) * m] + small[i * m:(i + 1) * m]
                for i in range(len(a_list))]

    def dot3(a, b, dims):
        return dot3_shared([a], b, dims)[0]

    pairs = range(NPAIR)

    cpi = 2 if nc % 2 == 0 else 1

    def intra_body(n, carry):
        units = [(n * cpi + j, hp) for j in range(cpi) for hp in pairs]
        rows = [pl.ds(pl.multiple_of(c * CHUNK, CHUNK), CHUNK) for c, _ in units]
        un = range(len(units))
        pt = [pt_sc[units[k][1], rows[k], :] for k in un]
        rt = [rt_sc[units[k][1], rows[k], :] for k in un]
        lhs = [parts(jnp.concatenate([pt[k], rt[k]], axis=0)) for k in un]
        gq = [dot3(lhs[k], bd2(parts(qt_sc[units[k][1], rows[k], :])), nt) for k in un]
        gk = [dot3(lhs[k], bd2(parts(kt_sc[units[k][1], rows[k], :])), nt) for k in un]
        l_pq = [jnp.where(tri_strict2, gq[k][0:CHUNK], 0.0) for k in un]
        lv = [dot3(parts(jnp.where(tri_strict2, gk[k][0:CHUNK], 0.0)),
                   bd2(parts(v_sc[units[k][1], rows[k], :])), nn) for k in un]
        tinv = [eye2 + l_pq[k] for k in un]
        xpp = [parts(l_pq[k]) for k in un]
        xpp = [parts(dot3(xpp[k], bd2(xpp[k]), nn)) for k in un]
        for it in range(5):
            tp = [parts(tinv[k]) for k in un]
            if it < 4:
                res = [dot3_shared([xpp[k], tp[k]], bd2(xpp[k]), nn) for k in un]
                xpp = [parts(res[k][0]) for k in un]
                tinv = [tinv[k] + res[k][1] for k in un]
            else:
                tinv = [tinv[k] + dot3(tp[k], bd2(xpp[k]), nn) for k in un]
        tp = [parts(tinv[k]) for k in un]
        wmat = [dot3(tp[k], bd2((lhs[k][0][0:CHUNK], lhs[k][1][0:CHUNK])), nn) for k in un]
        for k, (c, hp) in enumerate(units):
            z_sc[c, hp] = dot3(tp[k], bd2(parts(lv[k])), nn)
            for t, part in enumerate(parts(jnp.concatenate([wmat[k], rt[k]], axis=0))):
                lhs_sc[t, c, hp] = part
            arqk_sc[c, hp] = jnp.concatenate(
                [_bf(jnp.where(tri_incl2, gq[k][CHUNK:2 * CHUNK], 0.0)),
                 _bf(jnp.where(tri_incl2, gk[k][CHUNK:2 * CHUNK], 0.0))], axis=1)
        return carry

    def state_body(c, carry):
        rows = pl.ds(pl.multiple_of(c * CHUNK, CHUNK), CHUNK)
        grow = pl.ds(pl.multiple_of(c * 8, 8), 8)
        vv = [parts(v_sc[hp, rows, :]) for hp in pairs]
        s0 = [s_sc[hp] for hp in pairs]
        ps = [dot3((lhs_sc[0, c, hp], lhs_sc[1, c, hp]), bd2(parts(s0[hp])), nt)
              for hp in pairs]
        uu = [parts(ps[hp][0:CHUNK] + z_sc[c, hp]) for hp in pairs]
        for hp in pairs:
            sl = slice(hp * 128, (hp + 1) * 128)
            yo_sc[rows, sl] = ps[hp][CHUNK:2 * CHUNK] + jnp.dot(
                arqk_sc[c, hp], jnp.concatenate([bd(uu[hp][0]), bd(vv[hp][0])], axis=0),
                preferred_element_type=F32)
        for hp in pairs:
            sl = slice(hp * 128, (hp + 1) * 128)
            uv = tuple(jnp.concatenate([uu[hp][t], vv[hp][t]], axis=0) for t in range(2))
            qk = parts(jnp.concatenate([qe_sc[hp, rows, :], ke_sc[hp, rows, :]], axis=0))
            full = dot3(uv, qk, tn)
            g_end = gend_sc[grow, sl][0:1, :]
            s_sc[hp] = s0[hp] * g_end + jnp.where(
                first_head, full[0:HEAD_DIM], full[HEAD_DIM:2 * HEAD_DIM])
        return carry

    lax.fori_loop(0, nc // cpi, intra_body, 0)
    lax.fori_loop(0, nc, state_body, 0)

    for hp in range(NPAIR):
        sl = slice(hp * 128, (hp + 1) * 128)
        yh = yo_sc[:, sl]
        mean = head_sum(yh, 1) * (1.0 / HEAD_DIM)
        d = yh - mean
        var = head_sum(d * d, 1) * (1.0 / HEAD_DIM)
        yn = d * lax.rsqrt(var + RW_GN_EPS) * lnw_ref[:, sl] + lnb_ref[:, sl]
        y_ref[0, :, sl] = _bf((yn + bonus_sc[:, sl]) * g_sc[:, sl])

    @pl.when(i == pl.num_programs(1) - 1)
    def _():
        for hp in range(NPAIR):
            sp = s_sc[hp]
            sfin_ref[0, 2 * hp] = sp[:, 0:HEAD_DIM]
            sfin_ref[0, 2 * hp + 1] = sp[:, HEAD_DIM:2 * HEAD_DIM]


def _rwkv(u, shift_hist, s0, p, tt):
    b, t, _ = u.shape
    row = lambda a: a.reshape(1, -1)
    sh = shift_hist.reshape(b, 1, RW_IN)
    sh_rkv = sh[:, :, :3 * BW]
    sh_lora = sh[:, :, 3 * BW:]
    mu = p["rw_mu"]
    full = lambda shape: pl.BlockSpec(shape, lambda i, j: (0,) * len(shape))
    scr = lambda: pltpu.VMEM((NPAIR, tt, 128), F32)
    wide = lambda rows: pltpu.VMEM((rows, BW), F32)
    return pl.pallas_call(
        _rwkv_kernel,
        grid=(b, t // tt),
        in_specs=[
            pl.BlockSpec((1, tt, 3 * BW), lambda i, j: (i, j, C_RKV // (3 * BW))),
            pl.BlockSpec((1, tt, RW_LORA), lambda i, j: (i, j, C_LORA // RW_LORA)),
            pl.BlockSpec((1, 1, 3 * BW), lambda i, j: (i, 0, 0)),
            pl.BlockSpec((1, 1, RW_LORA), lambda i, j: (i, 0, 0)),
            pl.BlockSpec((1, N_HEADS, HEAD_DIM, HEAD_DIM), lambda i, j: (i, 0, 0, 0)),
            full((1, 3 * BW)), full((1, RW_LORA)),
            full((1, BW)), full((64, BW)), full((1, BW)), full((64, BW)), full((128, BW)),
            full((1, BW)), full((1, BW)),
            full((1, BW)), full((1, BW)), full((1, BW)),
        ],
        out_specs=[
            pl.BlockSpec((1, tt, BW), lambda i, j: (i, j, 0)),
            pl.BlockSpec((1, N_HEADS, HEAD_DIM, HEAD_DIM), lambda i, j: (i, 0, 0, 0)),
        ],
        out_shape=[
            jax.ShapeDtypeStruct((b, t, BW), BF16),
            jax.ShapeDtypeStruct((b, N_HEADS, HEAD_DIM, HEAD_DIM), F32),
        ],
        scratch_shapes=[
            pltpu.VMEM((1, 3 * BW), F32), pltpu.VMEM((1, RW_LORA), F32),
            pltpu.VMEM((NPAIR, HEAD_DIM, 128), F32),
            wide(tt), wide(8 * (tt // CHUNK)), wide(tt), wide(tt),
            scr(), scr(), scr(), scr(), scr(), scr(), scr(),
            pltpu.VMEM((2, tt // CHUNK, NPAIR, 2 * CHUNK, 128), BF16),
            pltpu.VMEM((tt // CHUNK, NPAIR, CHUNK, 128), F32),
            pltpu.VMEM((tt // CHUNK, NPAIR, CHUNK, 256), BF16),
        ],
        compiler_params=_cparams(("parallel", "arbitrary")),
        name="rwkv7",
    )(u, u, sh_rkv, sh_lora, s0, row(mu[:3 * BW]), row(mu[3 * BW:]),
      row(p["rw_w0"]), p["rw_w2"], row(p["rw_a0"]), p["rw_a2"], p["rw_g2"],
      row(p["rw_kk"]), row(p["rw_ka"]), row(p["rw_rk"]), row(p["rw_ln_w"]), row(p["rw_ln_b"]))


def _pool_kernel(pos0, u_ref, prev_ref, hist_ref, w_ref, scale_ref, y_ref):
    tt = u_ref.shape[1]
    i = pl.program_id(1)
    x = u_ref[0]
    prev = jnp.where(i == 0, hist_ref[0], prev_ref[0])
    s = jnp.concatenate([prev, x], axis=0)
    pos = pos0 + i * tt + lax.broadcasted_iota(jnp.int32, (tt, 1), 0)
    outs = []
    for gi, win in enumerate(POOL_WINDOWS):
        s = s + pltpu.roll(s, win // 2, 0)
        sl = slice(gi * POOL_GW, (gi + 1) * POOL_GW)
        cnt = jnp.minimum(pos + 1, win).astype(F32)
        pg = s[16:, sl] / cnt - x[:, sl]
        outs.append(_mm(pg, w_ref[gi]))
    y_ref[0] = _bf(jnp.concatenate(outs, axis=1) * scale_ref[...])


def _pool(u, hist16, pos0, w_pool, scale, tt):
    b, t, _ = u.shape
    nprev = tt // 16
    return pl.pallas_call(
        functools.partial(_pool_kernel, pos0),
        grid=(b, t // tt),
        in_specs=[
            pl.BlockSpec((1, tt, BW), lambda i, j: (i, j, C_POOL // BW)),
            pl.BlockSpec((1, 16, BW), lambda i, j: (i, jnp.maximum(j * nprev - 1, 0), C_POOL // BW)),
            pl.BlockSpec((1, 16, BW), lambda i, j: (i, 0, 0)),
            pl.BlockSpec((4, POOL_GW, POOL_GW), lambda i, j: (0, 0, 0)),
            pl.BlockSpec((1, BW), lambda i, j: (0, 0)),
        ],
        out_specs=pl.BlockSpec((1, tt, BW), lambda i, j: (i, j, 0)),
        out_shape=jax.ShapeDtypeStruct((b, t, BW), BF16),
        compiler_params=_cparams(("parallel", "parallel")),
        name="pool",
    )(u, u, hist16, w_pool, scale.reshape(1, BW))


def _swa_kernel(mask_history, q_ref, kp2_ref, kp1_ref, kc_ref, vp2_ref, vp1_ref, vc_ref, sink_ref, y_ref):
    tq = q_ref.shape[1]
    nc = tq // CHUNK
    i = pl.program_id(1)
    q = q_ref[0]
    k_all = jnp.concatenate([kp2_ref[0], kp1_ref[0], kc_ref[0]], axis=0)
    v_all = jnp.concatenate([vp2_ref[0], vp1_ref[0], vc_ref[0]], axis=0)
    nk = 3 * CHUNK
    qi = lax.broadcasted_iota(jnp.int32, (CHUNK, nk), 0)
    si = lax.broadcasted_iota(jnp.int32, (CHUNK, nk), 1)
    dist1 = jnp.abs(qi + 2 * CHUNK - si).astype(F32)
    dist = jnp.concatenate([dist1] * SWA_GROUP, axis=0)
    scol = lax.broadcasted_iota(jnp.int32, (SWA_GROUP * CHUNK, nk), 1)
    units = [(c, g) for c in range(nc) for g in range(SWA_KV)]
    bias = []
    sinks = []
    for g in range(SWA_KV):
        heads = [g * SWA_GROUP + hh for hh in range(SWA_GROUP)]
        slope = jnp.concatenate(
            [jnp.full((CHUNK, 1), 2.0 ** (-8.0 * (h + 1) / N_HEADS), F32) for h in heads], axis=0)
        bias.append(slope * dist)
        sinks.append(jnp.concatenate([jnp.full((CHUNK, 1), sink_ref[h], F32) for h in heads], axis=0))

    def scores(c, g):
        ks = k_all[c * CHUNK:c * CHUNK + nk, g * HEAD_DIM:(g + 1) * HEAD_DIM]
        qs = jnp.concatenate(
            [q[c * CHUNK:(c + 1) * CHUNK, (g * SWA_GROUP + hh) * HEAD_DIM:(g * SWA_GROUP + hh + 1) * HEAD_DIM]
             for hh in range(SWA_GROUP)], axis=0)
        return _mm_nt(qs, ks)

    def probs(s, c, g):
        s = s * (HEAD_DIM ** -0.5) - bias[g]
        if mask_history:
            s = jnp.where(scol >= (2 - (i * nc + c)) * CHUNK, s, -1e30)
        m = jnp.maximum(jnp.max(s, axis=-1, keepdims=True), sinks[g])
        p = jnp.exp(s - m)
        denom = jnp.sum(p, axis=-1, keepdims=True) + jnp.exp(sinks[g] - m)
        return _bf(p * (1.0 / denom))

    s_all = [scores(c, g) for c, g in units]
    p_all = [probs(s, c, g) for s, (c, g) in zip(s_all, units)]
    o_all = [_mm(p, v_all[c * CHUNK:c * CHUNK + nk, g * HEAD_DIM:(g + 1) * HEAD_DIM])
             for p, (c, g) in zip(p_all, units)]
    for o, (c, g) in zip(o_all, units):
        for hh in range(SWA_GROUP):
            h = g * SWA_GROUP + hh
            y_ref[0, c * CHUNK:(c + 1) * CHUNK, h * HEAD_DIM:(h + 1) * HEAD_DIM] = _bf(
                o[hh * CHUNK:(hh + 1) * CHUNK])


def _swa(u, hist_k, hist_v, sinks, tq):
    b, t, _ = u.shape
    per = tq // CHUNK
    kvw = SWA_KV * HEAD_DIM
    mask_history = hist_k is None
    cur_k = pl.BlockSpec((1, tq, kvw), lambda i, j: (i, j, C_KS // kvw))
    cur_v = pl.BlockSpec((1, tq, kvw), lambda i, j: (i, j, C_VS // kvw))
    if mask_history:
        prev = lambda d, col: pl.BlockSpec(
            (1, CHUNK, kvw), lambda i, j: (i, jnp.maximum(j * per - d, 0), col))
        specs = [prev(2, C_KS // kvw), prev(1, C_KS // kvw), cur_k,
                 prev(2, C_VS // kvw), prev(1, C_VS // kvw), cur_v]
        args = (u, u, u, u, u, u)
    else:
        assert t == tq
        hk = hist_k.reshape(b, WINDOW, kvw)
        hv = hist_v.reshape(b, WINDOW, kvw)
        hist = lambda blk: pl.BlockSpec((1, CHUNK, kvw), lambda i, j: (i, blk, 0))
        specs = [hist(0), hist(1), cur_k, hist(0), hist(1), cur_v]
        args = (hk, hk, u, hv, hv, u)
    return pl.pallas_call(
        functools.partial(_swa_kernel, mask_history),
        grid=(b, t // tq),
        in_specs=[pl.BlockSpec((1, tq, BW), lambda i, j: (i, j, C_Q // BW))] + specs
        + [pl.BlockSpec(memory_space=pltpu.SMEM)],
        out_specs=pl.BlockSpec((1, tq, BW), lambda i, j: (i, j, 0)),
        out_shape=jax.ShapeDtypeStruct((b, t, BW), BF16),
        compiler_params=_cparams(("parallel", "parallel")),
        name="swa",
    )(u, *args, sinks)


def _ssd_kernel(z_ref, x_ref, b_ref, c_ref, dt_ref, hx_ref, hb_ref, hc_ref, s0_ref,
                cwx_ref, cwb_ref, cwc_ref, cbx_ref, cbb_ref, cbc_ref,
                dtb_ref, alog_ref, dsk_ref, nw_ref,
                y_ref, sfin_ref,
                px_sc, pb_sc, pc_sc, s_sc, xa_sc, ba_sc, ca_sc, acs_sc, gend_sc,
                acsx_sc, eout_sc, y_sc, xdt_sc, xde_sc):
    tt = x_ref.shape[1]
    nc = tt // CHUNK
    i = pl.program_id(1)

    @pl.when(i == 0)
    def _():
        px_sc[...] = hx_ref[0]
        pb_sc[...] = hb_ref[0]
        pc_sc[...] = hc_ref[0]
        s_sc[...] = s0_ref[0].reshape(N_HEADS * HEAD_DIM, SSM_STATE)

    rid8 = lax.broadcasted_iota(jnp.int32, (8, 1), 0)

    def conv_silu(x, prev8, w_ref, b_ref_):
        acc = None
        for wi in range(SSM_CONV):
            sh = SSM_CONV - 1 - wi
            if sh == 0:
                xs = x
            else:
                rolled = pltpu.roll(x, sh, 0)
                top = jnp.where(rid8 < sh, pltpu.roll(prev8, sh, 0), rolled[0:8])
                xs = jnp.concatenate([top, rolled[8:]], axis=0)
            term = xs * w_ref[wi:wi + 1, :]
            acc = (b_ref_[...] + term) if acc is None else acc + term
        return _silu(acc)

    xr = x_ref[0]
    br = b_ref[0]
    cr = c_ref[0]
    xa_sc[...] = conv_silu(xr, px_sc[...], cwx_ref, cbx_ref)
    ba_sc[...] = conv_silu(br, pb_sc[...], cwb_ref, cbb_ref)
    ca_sc[...] = conv_silu(cr, pc_sc[...], cwc_ref, cbc_ref)
    px_sc[...] = xr[tt - 8:tt, :]
    pb_sc[...] = br[tt - 8:tt, :]
    pc_sc[...] = cr[tt - 8:tt, :]
    hl = 128
    dt = _softplus(dt_ref[0][:, 0:hl] + dtb_ref[...])
    ad = -jnp.exp(alog_ref[...]) * dt

    tri_incl_bf = _tri(CHUNK).astype(BF16)
    sel16 = (lax.broadcasted_iota(jnp.int32, (N_HEADS, hl), 0)
             == lax.broadcasted_iota(jnp.int32, (N_HEADS, hl), 1)).astype(BF16)
    max_terms = 3
    srow = lax.broadcasted_iota(jnp.int32, (hl, BW), 0)
    spread = (((srow & (N_HEADS - 1)) == lax.broadcasted_iota(jnp.int32, (hl, BW), 1) // HEAD_DIM)
              & (srow < max_terms * N_HEADS)).astype(BF16)
    head_lane = lax.broadcasted_iota(jnp.int32, (1, hl), 1) < N_HEADS

    def per_channel(x, terms):
        rem = jnp.where(head_lane, x, 0.0)
        packed = None
        for t in range(terms):
            part = _bf(rem).astype(F32)
            if t < terms - 1:
                rem = rem - part
            placed = part if t == 0 else pltpu.roll(part, t * N_HEADS, 1)
            packed = placed if packed is None else packed + placed
        return jnp.dot(_bf(packed), spread, preferred_element_type=F32)

    acs = jnp.concatenate(
        [_mm_sel(tri_incl_bf, ad[c * CHUNK:(c + 1) * CHUNK, :]) for c in range(nc)], axis=0)
    a_last = jnp.concatenate(
        [jnp.broadcast_to(acs[(c + 1) * CHUNK - 1:(c + 1) * CHUNK, :], (CHUNK, hl)) for c in range(nc)], axis=0)
    acs_sc[...] = acs
    acsx_sc[...] = per_channel(acs, 3)
    eout_sc[...] = per_channel(jnp.exp(acs), 2)
    gend_sc[...] = jnp.exp(jnp.concatenate(
        [jnp.broadcast_to(acs[(c + 1) * CHUNK - 1:(c + 1) * CHUNK, :], (8, hl)) for c in range(nc)], axis=0))
    xdt = xa_sc[...] * per_channel(dt, 2)
    xdt_sc[...] = _bf(xdt)
    xde_sc[...] = _bf(xdt * per_channel(jnp.exp(a_last - acs), 2))

    lane = lax.broadcasted_iota(jnp.int32, (CHUNK, 128), 1)
    row = lax.broadcasted_iota(jnp.int32, (CHUNK, 128), 0)
    tri_incl2 = row >= (lane & (HEAD_DIM - 1))
    r128 = lax.broadcasted_iota(jnp.int32, (128, 128), 0)
    c128 = lax.broadcasted_iota(jnp.int32, (128, 128), 1)
    same_head = (r128 < HEAD_DIM) == (c128 < HEAD_DIM)
    hpg = N_HEADS // SSM_GROUPS
    rpg = hpg * HEAD_DIM

    def chunk_body(c, carry):
        rows = pl.ds(pl.multiple_of(c * CHUNK, CHUNK), CHUNK)
        grow = pl.ds(pl.multiple_of(c * 8, 8), 8)
        bc = _bf(ba_sc[rows, :])
        cc = _bf(ca_sc[rows, :])
        acs_t = _mm_sel_nt(sel16, acs_sc[rows, :])
        g_end = gend_sc[grow, :][0:1, :]
        gsl = [slice(gi * SSM_STATE, (gi + 1) * SSM_STATE) for gi in range(SSM_GROUPS)]
        scores = [_mm_nt(cc[:, gs], bc[:, gs]) for gs in gsl]
        scores2 = [jnp.concatenate([s, s], axis=1) for s in scores]
        y_off = [_mm_nt(cc[:, gsl[gi]], s_sc[gi * rpg:(gi + 1) * rpg, :]) for gi in range(SSM_GROUPS)]
        upd = [_mm_tn(xde_sc[rows, gi * rpg:(gi + 1) * rpg], bc[:, gsl[gi]]) for gi in range(SSM_GROUPS)]
        y_diag = []
        for hp in range(N_HEADS // 2):
            sl = slice(hp * 128, (hp + 1) * 128)
            seg = acsx_sc[rows, sl] - jnp.concatenate([acs_t[2 * hp:2 * hp + 1, :],
                                                       acs_t[2 * hp + 1:2 * hp + 2, :]], axis=1)
            m = scores2[(2 * hp) // hpg] * jnp.exp(jnp.where(tri_incl2, seg, -1e30))
            xd = xdt_sc[rows, sl]
            y_diag.append(_mm(m, jnp.where(same_head, jnp.concatenate([xd, xd], axis=0),
                                           jnp.zeros((), BF16))))
        for h in range(N_HEADS):
            gi, hl_ = divmod(h, hpg)
            hrows = slice(h * HEAD_DIM, (h + 1) * HEAD_DIM)
            s_sc[hrows, :] = (s_sc[hrows, :] * g_end[:, h:h + 1]
                              + upd[gi][hl_ * HEAD_DIM:(hl_ + 1) * HEAD_DIM, :])
        y_sc[rows, :] = (jnp.concatenate(y_diag, axis=1)
                         + jnp.concatenate(y_off, axis=1) * eout_sc[rows, :])
        return carry

    lax.fori_loop(0, nc, chunk_body, 0)

    y = (y_sc[...] + xa_sc[...] * dsk_ref[...]) * _silu(z_ref[0])
    gw = BW // SSM_GROUPS
    parts = []
    for gi in range(SSM_GROUPS):
        yg = y[:, gi * gw:(gi + 1) * gw]
        parts.append(yg * lax.rsqrt(jnp.mean(yg * yg, axis=-1, keepdims=True) + NORM_EPS))
    y_ref[0] = _bf(jnp.concatenate(parts, axis=1) * nw_ref[...])

    @pl.when(i == pl.num_programs(1) - 1)
    def _():
        sfin_ref[0] = s_sc[...].reshape(N_HEADS, HEAD_DIM, SSM_STATE)


def _pad_rows8(a):
    return jnp.pad(a, ((0, 0), (8 - a.shape[1], 0), (0, 0)))


def _pad_lanes(a, n):
    return jnp.pad(a.reshape(1, -1), ((0, 0), (0, n - a.shape[-1])))


def _ssd(u, conv_hist, s0, p, tt):
    b, t, _ = u.shape
    h8 = _pad_rows8(conv_hist)
    hx, hb, hc = h8[:, :, :BW], h8[:, :, BW:BW + SSM_BC], h8[:, :, BW + SSM_BC:]
    cw, cb = p["ssm_conv_w"], p["ssm_conv_b"].reshape(1, -1)
    full = lambda shape: pl.BlockSpec(shape, lambda i, j: (0,) * len(shape))
    col = lambda w, c0: pl.BlockSpec((1, tt, w), lambda i, j: (i, j, c0 // w))
    hist = lambda w: pl.BlockSpec((1, 8, w), lambda i, j: (i, 0, 0))
    return pl.pallas_call(
        _ssd_kernel,
        grid=(b, t // tt),
        in_specs=[
            col(BW, C_Z), col(BW, C_X), col(SSM_BC, C_B), col(SSM_BC, C_C), col(SSM_BC, C_DT),
            hist(BW), hist(SSM_BC), hist(SSM_BC),
            pl.BlockSpec((1, N_HEADS, HEAD_DIM, SSM_STATE), lambda i, j: (i, 0, 0, 0)),
            full((SSM_CONV, BW)), full((SSM_CONV, SSM_BC)), full((SSM_CONV, SSM_BC)),
            full((1, BW)), full((1, SSM_BC)), full((1, SSM_BC)),
            full((1, 128)), full((1, 128)), full((1, BW)), full((1, BW)),
        ],
        out_specs=[
            pl.BlockSpec((1, tt, BW), lambda i, j: (i, j, 0)),
            pl.BlockSpec((1, N_HEADS, HEAD_DIM, SSM_STATE), lambda i, j: (i, 0, 0, 0)),
        ],
        out_shape=[
            jax.ShapeDtypeStruct((b, t, BW), BF16),
            jax.ShapeDtypeStruct((b, N_HEADS, HEAD_DIM, SSM_STATE), F32),
        ],
        scratch_shapes=[
            pltpu.VMEM((8, BW), F32), pltpu.VMEM((8, SSM_BC), F32), pltpu.VMEM((8, SSM_BC), F32),
            pltpu.VMEM((N_HEADS * HEAD_DIM, SSM_STATE), F32),
            pltpu.VMEM((tt, BW), F32), pltpu.VMEM((tt, SSM_BC), F32), pltpu.VMEM((tt, SSM_BC), F32),
            pltpu.VMEM((tt, 128), F32), pltpu.VMEM((8 * (tt // CHUNK), 128), F32),
            pltpu.VMEM((tt, BW), F32), pltpu.VMEM((tt, BW), F32), pltpu.VMEM((tt, BW), F32),
            pltpu.VMEM((tt, BW), BF16), pltpu.VMEM((tt, BW), BF16),
        ],
        compiler_params=_cparams(("parallel", "arbitrary")),
        name="ssd",
    )(u, u, u, u, u, hx, hb, hc, s0,
      cw[:, :BW], cw[:, BW:BW + SSM_BC], cw[:, BW + SSM_BC:],
      cb[:, :BW], cb[:, BW:BW + SSM_BC], cb[:, BW + SSM_BC:],
      _pad_lanes(p["ssm_dt_bias"], 128), _pad_lanes(p["ssm_a_log"], 128),
      jnp.repeat(p["ssm_d"], HEAD_DIM).reshape(1, BW), p["ssm_norm"].reshape(1, BW))


_RELAYOUT_BLOCK = 256


def _w_in_block_order():
    blk = _RELAYOUT_BLOCK
    segs = [(0, 3 * BW), (_N_SSM, BW), (_N_POOL, BW), (_N_SWA, BW), (_N_SSM + BW, BW), (3 * BW, RW_LORA),
            (_N_SWA + BW, 2 * SWA_KV * HEAD_DIM), (_N_SSM + 2 * BW, 2 * SSM_BC + blk)]
    order = [start // blk + i for start, width in segs for i in range(width // blk)]
    assert len(order) == U_COLS // blk and all(start % blk == 0 for start, _ in segs)
    return order


def _relayout_kernel(order_ref, w_ref, o_ref):
    src = order_ref[pl.program_id(1)]
    last = IN_COLS // _RELAYOUT_BLOCK
    valid = jnp.where(src == last, IN_COLS - last * _RELAYOUT_BLOCK, _RELAYOUT_BLOCK)
    col = lax.broadcasted_iota(jnp.int32, (1, _RELAYOUT_BLOCK), 1)
    o_ref[...] = _bf(jnp.where(col < valid, w_ref[...], 0.0))


def _prep_w_in(w_in):
    blk = _RELAYOUT_BLOCK
    order = jnp.asarray(_w_in_block_order(), jnp.int32)
    return pl.pallas_call(
        _relayout_kernel,
        grid_spec=pltpu.PrefetchScalarGridSpec(
            num_scalar_prefetch=1,
            grid=(DEPTH, U_COLS // blk),
            in_specs=[pl.BlockSpec((None, D_MODEL, blk), lambda l, j, order: (l, 0, order[j]))],
            out_specs=pl.BlockSpec((None, D_MODEL, blk), lambda l, j, order: (l, 0, j)),
        ),
        out_shape=jax.ShapeDtypeStruct((DEPTH, D_MODEL, U_COLS), BF16),
        compiler_params=_cparams(("parallel", "parallel")),
        name="w_in_relayout",
    )(order, w_in)


def _run_group(x, mods, st, p, final_norm, wts, pos0, bb, tt, tt_branch):
    b, t, _ = x.shape
    kvw = SWA_KV * HEAD_DIM
    outs = {k: [] for k in ("rwkv", "shift", "pool", "k", "v", "ssm", "conv")}
    for l in range(DEPTH):
        pl_ = {k: v[l] for k, v in p.items()}
        mod = mods[l]
        u, h = _inproj(x, pl_["norm_mix"], mod, wts["w_in"], l, bb, tt)
        if st is None:
            shift_hist = jnp.zeros((b, 1, RW_IN), F32)
            s_rwkv = jnp.zeros((b, N_HEADS, HEAD_DIM, HEAD_DIM), F32)
            pool_hist = jnp.zeros((b, POOL_HIST, BW), F32)
            hk = hv = None
            s_ssm = jnp.zeros((b, N_HEADS, HEAD_DIM, SSM_STATE), F32)
            conv_hist = jnp.zeros((b, SSM_CONV - 1, SSM_CONV_DIM), F32)
        else:
            shift_hist, s_rwkv, pool_hist = st["shift"][l], st["rwkv"][l], st["pool"][l]
            hk, hv, s_ssm, conv_hist = st["k"][l], st["v"][l], st["ssm"][l], st["conv"][l]
        y_a, n_rwkv = _rwkv(u, shift_hist, s_rwkv, pl_, min(tt_branch, 256))
        hist16 = jnp.pad(pool_hist, ((0, 0), (1, 0), (0, 0)))
        y_b = _pool(u, hist16, pos0, wts["pool_w"][l], pl_["pool_scale"], tt_branch)
        y_c = _swa(u, hk, hv, pl_["swa_sinks"], min(tt_branch, 256))
        y_d, n_ssm = _ssd(u, conv_hist, s_ssm, pl_, tt_branch)
        merged = _merge(h, (y_a, y_b, y_c, y_d), wts["w_gate"], wts["w_branch"], l, bb, tt)
        x = _outproj(merged, wts["w_out"], l, x, mod, bb, tt)
        x = _ffn(x, pl_["norm_ffn"], mod, wts["w_up"], wts["w_down"], l, final_norm, l == DEPTH - 1,
                 bb, tt)

        outs["rwkv"].append(n_rwkv)
        outs["shift"].append(jnp.concatenate(
            [u[:, t - 1:, C_RKV:C_RKV + 3 * BW], u[:, t - 1:, C_LORA:C_LORA + RW_LORA]], axis=-1))
        outs["pool"].append(u[:, t - POOL_HIST:, C_POOL:C_POOL + BW])
        k_new = u[:, :, C_KS:C_KS + kvw]
        v_new = u[:, :, C_VS:C_VS + kvw]
        if hk is not None:
            k_new = jnp.concatenate([hk.reshape(b, WINDOW, kvw), k_new], axis=1)
            v_new = jnp.concatenate([hv.reshape(b, WINDOW, kvw), v_new], axis=1)
        outs["k"].append(k_new[:, -WINDOW:].reshape(b, WINDOW, SWA_KV, HEAD_DIM))
        outs["v"].append(v_new[:, -WINDOW:].reshape(b, WINDOW, SWA_KV, HEAD_DIM))
        outs["ssm"].append(n_ssm)
        outs["conv"].append(jnp.concatenate(
            [u[:, t - 3:, C_X:C_X + BW], u[:, t - 3:, C_B:C_B + 2 * SSM_BC]], axis=-1))
    order = ("rwkv", "shift", "pool", "k", "v", "ssm", "conv")
    return x, tuple(jnp.stack(outs[k]) for k in order)


def kernel(x_prompt, x_sample, state_rwkv, state_rwkv_shift, state_pool, cache_swa_k, cache_swa_v,
           state_ssm, state_ssm_conv, c_prompt, c_sample, ada_w, ada_b, norm_mix, norm_ffn, w_in,
           rw_mu, rw_w0, rw_w2, rw_a0, rw_a2, rw_g2, rw_kk, rw_ka, rw_rk, rw_ln_w, rw_ln_b, pool_w,
           pool_scale, swa_sinks, ssm_conv_w, ssm_conv_b, ssm_dt_bias, ssm_a_log, ssm_d, ssm_norm,
           w_gate, w_branch, w_out, w_up, w_down, final_norm):
    bp, tp, _ = x_prompt.shape
    bs, ts, _ = x_sample.shape
    p = dict(norm_mix=norm_mix, norm_ffn=norm_ffn, rw_mu=rw_mu, rw_w0=rw_w0, rw_w2=rw_w2, rw_a0=rw_a0,
             rw_a2=rw_a2, rw_g2=rw_g2, rw_kk=rw_kk, rw_ka=rw_ka, rw_rk=rw_rk, rw_ln_w=rw_ln_w,
             rw_ln_b=rw_ln_b, pool_scale=pool_scale, swa_sinks=swa_sinks, ssm_conv_w=ssm_conv_w,
             ssm_conv_b=ssm_conv_b, ssm_dt_bias=ssm_dt_bias, ssm_a_log=ssm_a_log, ssm_d=ssm_d,
             ssm_norm=ssm_norm)
    wts = dict(w_in=_prep_w_in(w_in), pool_w=pool_w.astype(BF16), w_gate=w_gate.astype(BF16),
               w_branch=w_branch.astype(BF16), w_out=w_out.astype(BF16), w_up=w_up.astype(BF16),
               w_down=w_down.astype(BF16))

    nb = bp + bs
    nb_pad = -(-nb // 8) * 8
    c_all = jnp.pad(jnp.concatenate([c_prompt, c_sample], axis=0), ((0, nb_pad - nb), (0, 0)))
    mod_all = _adaln(c_all, ada_w, ada_b)
    mods_p = mod_all[:, :bp].reshape(DEPTH, bp, 6, 1, D_MODEL)
    mods_s = mod_all[:, bp:nb].reshape(DEPTH, bs, 6, 1, D_MODEL)

    y_prompt, st_p = _run_group(x_prompt, mods_p, None, p, final_norm, wts, 0,
                                1, min(tp, 1024), min(tp, 512))
    st_s = dict(rwkv=state_rwkv, shift=state_rwkv_shift, pool=state_pool, k=cache_swa_k,
                v=cache_swa_v, ssm=state_ssm, conv=state_ssm_conv)
    y_sample, st_o = _run_group(x_sample, mods_s, st_s, p, final_norm, wts, PAST_LEN, bs, ts, ts)
    return (y_prompt, y_sample) + st_p + st_o
```

```python
import functools
import math

import jax
import jax.numpy as jnp
from jax import lax
from jax.experimental import pallas as pl
from jax.experimental.pallas import tpu as pltpu

F32 = jnp.float32
BF16 = jnp.bfloat16

D_MODEL = 2048
DEPTH = 4
PAST_LEN = 1024
CHUNK = 64
HEAD_DIM = 64
BW = D_MODEL // 2
D_FF = 4 * D_MODEL
NORM_EPS = 1e-6
N_HEADS = BW // HEAD_DIM
RW_LORA = 256
RW_IN = 3 * BW + RW_LORA
RW_GN_EPS = HEAD_DIM * 1e-5
POOL_WINDOWS = (2, 4, 8, 16)
POOL_GW = BW // 4
POOL_HIST = 15
SWA_KV = 4
SWA_GROUP = N_HEADS // SWA_KV
WINDOW = 128
SSM_GROUPS = 2
SSM_STATE = 128
SSM_CONV = 4
SSM_BC = SSM_GROUPS * SSM_STATE
SSM_CONV_DIM = BW + 2 * SSM_BC

C_RKV = 0
C_Z = 3072
C_POOL = 4096
C_Q = 5120
C_X = 6144
C_LORA = 7168
C_KS = 7424
C_VS = 7680
C_B = 7936
C_C = 8192
C_DT = 8448
U_COLS = 8704

_N_POOL = RW_IN
_N_SWA = _N_POOL + BW
_N_SSM = _N_SWA + (N_HEADS + 2 * SWA_KV) * HEAD_DIM
IN_COLS = _N_SSM + BW + SSM_CONV_DIM + N_HEADS

VMEM_LIMIT = 56 * 1024 * 1024


def _cparams(sem):
    return pltpu.CompilerParams(dimension_semantics=sem, vmem_limit_bytes=VMEM_LIMIT)


def _bf(x):
    return x.astype(BF16)


def _mm(a, b):
    return jnp.dot(_bf(a), _bf(b), preferred_element_type=F32)


def _mm_nt(a, b):
    return lax.dot_general(_bf(a), _bf(b), (((1,), (1,)), ((), ())), preferred_element_type=F32)


def _mm_tn(a, b):
    return lax.dot_general(_bf(a), _bf(b), (((0,), (0,)), ((), ())), preferred_element_type=F32)


def _split3(x):
    hi = _bf(x)
    r1 = x - hi.astype(F32)
    mid = _bf(r1)
    lo = _bf(r1 - mid.astype(F32))
    return hi, mid, lo


def _split(x, terms):
    parts = []
    for _ in range(terms - 1):
        hi = _bf(x)
        parts.append(hi)
        x = x - hi.astype(F32)
    parts.append(_bf(x))
    return parts


def _mm_sel(c, x, terms=3):
    acc = None
    for part in _split(x, terms):
        d = jnp.dot(c, part, preferred_element_type=F32)
        acc = d if acc is None else acc + d
    return acc


def _mm_sel_nt(c, x):
    hi, mid, lo = _split3(x)
    d = lambda p: lax.dot_general(c, p, (((1,), (1,)), ((), ())), preferred_element_type=F32)
    return d(hi) + d(mid) + d(lo)


def _softplus(x):
    return jnp.maximum(x, 0.0) + jnp.log1p(jnp.exp(-jnp.abs(x)))


def _sigmoid(x):
    return 0.5 * jnp.tanh(0.5 * x) + 0.5


def _silu(x):
    return x * _sigmoid(x)


def _tri(n, strict=False):
    r = lax.broadcasted_iota(jnp.int32, (n, n), 0)
    c = lax.broadcasted_iota(jnp.int32, (n, n), 1)
    return (r > c) if strict else (r >= c)


def _adaln_kernel(c_ref, w_ref, b_ref, o_ref):
    s = _silu(c_ref[...])
    o_ref[0] = _mm(s, w_ref[0]) + b_ref[0]


def _adaln(c_all, ada_w, ada_b):
    nb = c_all.shape[0]
    tn = 1024
    n_out = ada_w.shape[2]
    return pl.pallas_call(
        _adaln_kernel,
        grid=(DEPTH, n_out // tn),
        in_specs=[
            pl.BlockSpec((nb, D_MODEL), lambda l, n: (0, 0)),
            pl.BlockSpec((1, D_MODEL, tn), lambda l, n: (l, 0, n)),
            pl.BlockSpec((1, 1, tn), lambda l, n: (l, 0, n)),
        ],
        out_specs=pl.BlockSpec((1, nb, tn), lambda l, n: (l, 0, n)),
        out_shape=jax.ShapeDtypeStruct((DEPTH, nb, n_out), F32),
        compiler_params=_cparams(("parallel", "parallel")),
        name="adaln",
    )(c_all, ada_w, ada_b.reshape(DEPTH, 1, n_out))


def _norm_mod(x, g, sc, sh):
    y = x * lax.rsqrt(jnp.mean(x * x, axis=-1, keepdims=True) + NORM_EPS)
    return (y * g) * (1.0 + sc) + sh


def _inproj_kernel(x_ref, g_ref, sc_ref, sh_ref, w_ref, u_ref, h_ref):
    bb, tt, _ = x_ref.shape

    @pl.when(pl.program_id(2) == 0)
    def _():
        h = _norm_mod(x_ref[...], g_ref[...], sc_ref[:, 0], sh_ref[:, 0])
        h_ref[...] = _bf(h)

    h = h_ref[...].reshape(bb * tt, D_MODEL)
    u = lax.dot_general(h, w_ref[...], (((1,), (1,)), ((), ())), preferred_element_type=F32)
    u_ref[...] = u.reshape(u_ref.shape)


def _inproj(x, g, mod, w, l, bb, tt):
    b, t, _ = x.shape
    tn = U_COLS // 4
    if bb * tt > 512:
        tt = 512 // bb
    return pl.pallas_call(
        _inproj_kernel,
        grid=(b // bb, t // tt, U_COLS // tn),
        in_specs=[
            pl.BlockSpec((bb, tt, D_MODEL), lambda i, j, n: (i, j, 0)),
            pl.BlockSpec((1, 1, D_MODEL), lambda i, j, n: (0, 0, 0)),
            pl.BlockSpec((bb, 1, 1, D_MODEL), lambda i, j, n: (i, 1, 0, 0)),
            pl.BlockSpec((bb, 1, 1, D_MODEL), lambda i, j, n: (i, 0, 0, 0)),
            pl.BlockSpec((None, tn, D_MODEL), lambda i, j, n: (l, n, 0)),
        ],
        out_specs=[
            pl.BlockSpec((bb, tt, tn), lambda i, j, n: (i, j, n)),
            pl.BlockSpec((bb, tt, D_MODEL), lambda i, j, n: (i, j, 0)),
        ],
        out_shape=[
            jax.ShapeDtypeStruct((b, t, U_COLS), F32),
            jax.ShapeDtypeStruct((b, t, D_MODEL), BF16),
        ],
        compiler_params=_cparams(("parallel", "parallel", "arbitrary")),
        name="inproj",
    )(x, g.reshape(1, 1, D_MODEL), mod, mod, w)


def _merge_kernel(h_ref, ya_ref, yb_ref, yc_ref, yd_ref, wg_ref, wb_ref, o_ref):
    bb, tt, _ = h_ref.shape
    m = bb * tt
    h = h_ref[...].reshape(m, D_MODEL)
    acc = None
    for i, y_ref in enumerate((ya_ref, yb_ref, yc_ref, yd_ref)):
        gate = _sigmoid(jnp.dot(h, wg_ref[i], preferred_element_type=F32))
        br = jnp.dot(y_ref[...].reshape(m, BW), wb_ref[i], preferred_element_type=F32)
        acc = gate * br if acc is None else acc + gate * br
    o_ref[...] = _bf(acc).reshape(o_ref.shape)


def _merge(h, ys, wg, wb, l, bb, tt):
    b, t, _ = h.shape
    tn = 256
    yspec = pl.BlockSpec((bb, tt, BW), lambda i, j, n: (i, j, 0))
    return pl.pallas_call(
        _merge_kernel,
        grid=(b // bb, t // tt, D_MODEL // tn),
        in_specs=[
            pl.BlockSpec((bb, tt, D_MODEL), lambda i, j, n: (i, j, 0)),
            yspec, yspec, yspec, yspec,
            pl.BlockSpec((None, 4, D_MODEL, tn), lambda i, j, n: (l, 0, 0, n)),
            pl.BlockSpec((None, 4, BW, tn), lambda i, j, n: (l, 0, 0, n)),
        ],
        out_specs=pl.BlockSpec((bb, tt, tn), lambda i, j, n: (i, j, n)),
        out_shape=jax.ShapeDtypeStruct((b, t, D_MODEL), BF16),
        compiler_params=_cparams(("parallel", "parallel", "arbitrary")),
        name="merge",
    )(h, *ys, wg, wb)


def _outproj_kernel(m_ref, w_ref, x_ref, g_ref, o_ref):
    bb, tt, _ = m_ref.shape
    y = jnp.dot(m_ref[...].reshape(bb * tt, D_MODEL), w_ref[...], preferred_element_type=F32)
    o_ref[...] = x_ref[...] + g_ref[:, 0] * y.reshape(o_ref.shape)


def _outproj(merged, w, l, x, mod, bb, tt):
    b, t, _ = x.shape
    tn = 1024
    return pl.pallas_call(
        _outproj_kernel,
        grid=(b // bb, t // tt, D_MODEL // tn),
        in_specs=[
            pl.BlockSpec((bb, tt, D_MODEL), lambda i, j, n: (i, j, 0)),
            pl.BlockSpec((None, D_MODEL, tn), lambda i, j, n: (l, 0, n)),
            pl.BlockSpec((bb, tt, tn), lambda i, j, n: (i, j, n)),
            pl.BlockSpec((bb, 1, 1, tn), lambda i, j, n: (i, 2, 0, n)),
        ],
        out_specs=pl.BlockSpec((bb, tt, tn), lambda i, j, n: (i, j, n)),
        out_shape=jax.ShapeDtypeStruct((b, t, D_MODEL), F32),
        compiler_params=_cparams(("parallel", "parallel", "arbitrary")),
        name="outproj",
    )(merged, w, x, mod)


def _ffn_kernel(final, x_ref, g_ref, sc_ref, sh_ref, gate_ref, wu_ref, wd_ref, fg_ref, o_ref, h_sc):
    bb, tt, _ = x_ref.shape
    m = bb * tt
    f = pl.program_id(2)

    @pl.when(f == 0)
    def _():
        h = _norm_mod(x_ref[...], g_ref[...], sc_ref[:, 0], sh_ref[:, 0])
        h_sc[...] = _bf(h).reshape(m, D_MODEL)
        o_ref[...] = jnp.zeros(o_ref.shape, F32)

    a = jnp.dot(h_sc[...], wu_ref[...], preferred_element_type=F32)
    a = jnp.square(jnp.maximum(a, 0.0))
    o_ref[...] += jnp.dot(_bf(a), wd_ref[...], preferred_element_type=F32).reshape(o_ref.shape)

    @pl.when(f == pl.num_programs(2) - 1)
    def _():
        y = x_ref[...] + gate_ref[:, 0] * o_ref[...]
        if final:
            y = (y * lax.rsqrt(jnp.mean(y * y, axis=-1, keepdims=True) + NORM_EPS)) * fg_ref[...]
        o_ref[...] = y


def _ffn(x, g, mod, wu, wd, l, final_g, final, bb, tt):
    b, t, _ = x.shape
    tf = 1024
    if bb * tt > 512:
        tt = 512 // bb
    vec = pl.BlockSpec((1, 1, D_MODEL), lambda i, j, f: (0, 0, 0))
    return pl.pallas_call(
        functools.partial(_ffn_kernel, final),
        grid=(b // bb, t // tt, D_FF // tf),
        in_specs=[
            pl.BlockSpec((bb, tt, D_MODEL), lambda i, j, f: (i, j, 0)),
            vec,
            pl.BlockSpec((bb, 1, 1, D_MODEL), lambda i, j, f: (i, 4, 0, 0)),
            pl.BlockSpec((bb, 1, 1, D_MODEL), lambda i, j, f: (i, 3, 0, 0)),
            pl.BlockSpec((bb, 1, 1, D_MODEL), lambda i, j, f: (i, 5, 0, 0)),
            pl.BlockSpec((None, D_MODEL, tf), lambda i, j, f: (l, 0, f)),
            pl.BlockSpec((None, tf, D_MODEL), lambda i, j, f: (l, f, 0)),
            vec,
        ],
        out_specs=pl.BlockSpec((bb, tt, D_MODEL), lambda i, j, f: (i, j, 0)),
        out_shape=jax.ShapeDtypeStruct((b, t, D_MODEL), F32),
        scratch_shapes=[pltpu.VMEM((bb * tt, D_MODEL), BF16)],
        compiler_params=_cparams(("parallel", "parallel", "arbitrary")),
        name="ffn",
    )(x, g.reshape(1, 1, D_MODEL), mod, mod, mod, wu, wd, final_g.reshape(1, 1, D_MODEL))


NPAIR = N_HEADS // 2


def _rwkv_kernel(rkv_ref, lora_ref, sh_rkv_ref, sh_lora_ref, s0_ref, mu_rkv_ref, mu_lora_ref,
                 w0_ref, w2_ref, a0_ref, a2_ref, g2_ref, kkp_ref, ka_ref, rk_ref, lnw_ref, lnb_ref,
                 y_ref, sfin_ref,
                 prev_rkv, prev_lora, s_sc, g_sc, gend_sc, bonus_sc, yo_sc,
                 rt_sc, pt_sc, qt_sc, kt_sc, qe_sc, ke_sc, v_sc, lhs_sc, z_sc, arqk_sc):
    tt = rkv_ref.shape[1]
    nc = tt // CHUNK
    i = pl.program_id(1)

    @pl.when(i == 0)
    def _():
        prev_rkv[...] = sh_rkv_ref[0]
        prev_lora[...] = sh_lora_ref[0]
        for hp in range(NPAIR):
            s_sc[hp] = jnp.concatenate([s0_ref[0, 2 * hp], s0_ref[0, 2 * hp + 1]], axis=1)

    rid8 = lax.broadcasted_iota(jnp.int32, (8, 1), 0)

    def tshift(x, prev_row, mu):
        rolled = pltpu.roll(x, 1, 0)
        prev = jnp.concatenate([jnp.where(rid8 == 0, prev_row, rolled[0:8]), rolled[8:]], axis=0)
        return x + (prev - x) * mu

    u = rkv_ref[0]
    ul = lora_ref[0]
    xs = tshift(u, prev_rkv[...], mu_rkv_ref[...])
    xl = tshift(ul, prev_lora[...], mu_lora_ref[...])
    prev_rkv[...] = u[tt - 1:tt, :]
    prev_lora[...] = ul[tt - 1:tt, :]

    r = xs[:, 0:BW]
    k = xs[:, BW:2 * BW]
    v = xs[:, 2 * BW:3 * BW]
    wd = xl[:, 0:64]
    ad = xl[:, 64:128]
    gd = xl[:, 128:256]
    lw = -math.exp(-0.5) * _sigmoid(w0_ref[...] + _mm(jnp.tanh(wd), w2_ref[...]))
    a = _sigmoid(a0_ref[...] + _mm(ad, a2_ref[...]))
    g_sc[...] = _mm(_sigmoid(gd), g2_ref[...])
    kkr = k * kkp_ref[...]
    kh = k * (1.0 + (a - 1.0) * ka_ref[...])

    lane = lax.broadcasted_iota(jnp.int32, (CHUNK, 128), 1)
    row = lax.broadcasted_iota(jnp.int32, (CHUNK, 128), 0)
    col_in_head = lane & (HEAD_DIM - 1)
    tri_strict2 = row > col_in_head
    tri_incl2 = row >= col_in_head
    eye2 = (row == col_in_head).astype(F32)
    first_head = lane < HEAD_DIM
    r128 = lax.broadcasted_iota(jnp.int32, (128, 128), 0)
    c128 = lax.broadcasted_iota(jnp.int32, (128, 128), 1)
    same_head = (r128 < HEAD_DIM) == (c128 < HEAD_DIM)
    ones_bd = same_head.astype(BF16)
    tri_incl_bf = _tri(CHUNK).astype(BF16)

    def head_sum(x, terms):
        acc = None
        for part in _split(x, terms):
            d = jnp.dot(part, ones_bd, preferred_element_type=F32)
            acc = d if acc is None else acc + d
        return acc

    cl = jnp.concatenate(
        [_mm_sel(tri_incl_bf, lw[c * CHUNK:(c + 1) * CHUNK, :], terms=2) for c in range(nc)], axis=0)
    cl_last = jnp.concatenate(
        [jnp.broadcast_to(cl[(c + 1) * CHUNK - 1:(c + 1) * CHUNK, :], (CHUNK, BW)) for c in range(nc)], axis=0)
    e_in = jnp.exp(cl)
    e_ex = jnp.exp(cl - lw)
    e_inv = jnp.exp(-cl)
    e_end = jnp.exp(cl_last - cl)
    gend_sc[...] = jnp.exp(jnp.concatenate(
        [jnp.broadcast_to(cl[(c + 1) * CHUNK - 1:(c + 1) * CHUNK, :], (8, BW)) for c in range(nc)], axis=0))
    for hp in range(NPAIR):
        sl = slice(hp * 128, (hp + 1) * 128)
        kk = kkr[:, sl]
        kk = kk * lax.rsqrt(jnp.maximum(head_sum(kk * kk, 2), 1e-24))
        q = kk * a[:, sl]
        rt_sc[hp] = r[:, sl] * e_in[:, sl]
        pt_sc[hp] = -kk * e_ex[:, sl]
        qt_sc[hp] = q * e_inv[:, sl]
        kt_sc[hp] = kh[:, sl] * e_inv[:, sl]
        qe_sc[hp] = q * e_end[:, sl]
        ke_sc[hp] = kh[:, sl] * e_end[:, sl]
        v_sc[hp] = v[:, sl]
        bonus_sc[:, sl] = head_sum(r[:, sl] * kh[:, sl] * rk_ref[:, sl], 1) * v[:, sl]

    def bd(x):
        return jnp.where(same_head, jnp.concatenate([x, x], axis=0), jnp.zeros((), x.dtype))

    def parts(x):
        hi = _bf(x)
        return hi, _bf(x - hi.astype(F32))

    def bd2(p):
        return bd(p[0]), bd(p[1])

    nn, nt, tn = ((1,), (0,)), ((1,), (1,)), ((0,), (0,))

    def dot3_shared(a_list, b, dims):
        d = lambda x, y: lax.dot_general(x, y, (dims, ((), ())), preferred_element_type=F32)
        ax = 1 if dims == tn else 0
        m = a_list[0][0].shape[ax]
        big = d(jnp.concatenate([t for a in a_list for t in a], axis=ax), b[0])
        small = d(jnp.concatenate([a[0] for a in a_list], axis=ax), b[1])
        return [big[2 * i * m:(2 * i + 1) * m] + big[(2 * i + 1) * m:(2 * i + 2) * m] + small[i * m:(i + 1) * m]
                for i in range(len(a_list))]

    def dot3(a, b, dims):
        return dot3_shared([a], b, dims)[0]

    pairs = range(NPAIR)

    cpi = 2 if nc % 2 == 0 else 1

    def intra_body(n, carry):
        units = [(n * cpi + j, hp) for j in range(cpi) for hp in pairs]
        rows = [pl.ds(pl.multiple_of(c * CHUNK, CHUNK), CHUNK) for c, _ in units]
        un = range(len(units))
        pt = [pt_sc[units[k][1], rows[k], :] for k in un]
        rt = [rt_sc[units[k][1], rows[k], :] for k in un]
        lhs = [parts(jnp.concatenate([pt[k], rt[k]], axis=0)) for k in un]
        gq = [dot3(lhs[k], bd2(parts(qt_sc[units[k][1], rows[k], :])), nt) for k in un]
        gk = [dot3(lhs[k], bd2(parts(kt_sc[units[k][1], rows[k], :])), nt) for k in un]
        l_pq = [jnp.where(tri_strict2, gq[k][0:CHUNK], 0.0) for k in un]
        lv = [dot3(parts(jnp.where(tri_strict2, gk[k][0:CHUNK], 0.0)),
                   bd2(parts(v_sc[units[k][1], rows[k], :])), nn) for k in un]
        tinv = [eye2 + l_pq[k] for k in un]
        xpp = [parts(l_pq[k]) for k in un]
        xpp = [parts(dot3(xpp[k], bd2(xpp[k]), nn)) for k in un]
        for it in range(5):
            tp = [parts(tinv[k]) for k in un]
            if it < 4:
                res = [dot3_shared([xpp[k], tp[k]], bd2(xpp[k]), nn) for k in un]
                xpp = [parts(res[k][0]) for k in un]
                tinv = [tinv[k] + res[k][1] for k in un]
            else:
                tinv = [tinv[k] + dot3(tp[k], bd2(xpp[k]), nn) for k in un]
        tp = [parts(tinv[k]) for k in un]
        wmat = [dot3(tp[k], bd2((lhs[k][0][0:CHUNK], lhs[k][1][0:CHUNK])), nn) for k in un]
        for k, (c, hp) in enumerate(units):
            z_sc[c, hp] = dot3(tp[k], bd2(parts(lv[k])), nn)
            for t, part in enumerate(parts(jnp.concatenate([wmat[k], rt[k]], axis=0))):
                lhs_sc[t, c, hp] = part
            arqk_sc[c, hp] = jnp.concatenate(
                [_bf(jnp.where(tri_incl2, gq[k][CHUNK:2 * CHUNK], 0.0)),
                 _bf(jnp.where(tri_incl2, gk[k][CHUNK:2 * CHUNK], 0.0))], axis=1)
        return carry

    def state_body(c, carry):
        rows = pl.ds(pl.multiple_of(c * CHUNK, CHUNK), CHUNK)
        grow = pl.ds(pl.multiple_of(c * 8, 8), 8)
        vv = [parts(v_sc[hp, rows, :]) for hp in pairs]
        s0 = [s_sc[hp] for hp in pairs]
        ps = [dot3((lhs_sc[0, c, hp], lhs_sc[1, c, hp]), bd2(parts(s0[hp])), nt)
              for hp in pairs]
        uu = [parts(ps[hp][0:CHUNK] + z_sc[c, hp]) for hp in pairs]
        for hp in pairs:
            sl = slice(hp * 128, (hp + 1) * 128)
            yo_sc[rows, sl] = ps[hp][CHUNK:2 * CHUNK] + jnp.dot(
                arqk_sc[c, hp], jnp.concatenate([bd(uu[hp][0]), bd(vv[hp][0])], axis=0),
                preferred_element_type=F32)
        for hp in pairs:
            sl = slice(hp * 128, (hp + 1) * 128)
            uv = tuple(jnp.concatenate([uu[hp][t], vv[hp][t]], axis=0) for t in range(2))
            qk = parts(jnp.concatenate([qe_sc[hp, rows, :], ke_sc[hp, rows, :]], axis=0))
            full = dot3(uv, qk, tn)
            g_end = gend_sc[grow, sl][0:1, :]
            s_sc[hp] = s0[hp] * g_end + jnp.where(
                first_head, full[0:HEAD_DIM], full[HEAD_DIM:2 * HEAD_DIM])
        return carry

    lax.fori_loop(0, nc // cpi, intra_body, 0)
    lax.fori_loop(0, nc, state_body, 0)

    for hp in range(NPAIR):
        sl = slice(hp * 128, (hp + 1) * 128)
        yh = yo_sc[:, sl]
        mean = head_sum(yh, 1) * (1.0 / HEAD_DIM)
        d = yh - mean
        var = head_sum(d * d, 1) * (1.0 / HEAD_DIM)
        yn = d * lax.rsqrt(var + RW_GN_EPS) * lnw_ref[:, sl] + lnb_ref[:, sl]
        y_ref[0, :, sl] = _bf((yn + bonus_sc[:, sl]) * g_sc[:, sl])

    @pl.when(i == pl.num_programs(1) - 1)
    def _():
        for hp in range(NPAIR):
            sp = s_sc[hp]
            sfin_ref[0, 2 * hp] = sp[:, 0:HEAD_DIM]
            sfin_ref[0, 2 * hp + 1] = sp[:, HEAD_DIM:2 * HEAD_DIM]


def _rwkv(u, shift_hist, s0, p, tt):
    b, t, _ = u.shape
    row = lambda a: a.reshape(1, -1)
    sh = shift_hist.reshape(b, 1, RW_IN)
    sh_rkv = sh[:, :, :3 * BW]
    sh_lora = sh[:, :, 3 * BW:]
    mu = p["rw_mu"]
    full = lambda shape: pl.BlockSpec(shape, lambda i, j: (0,) * len(shape))
    scr = lambda: pltpu.VMEM((NPAIR, tt, 128), F32)
    wide = lambda rows: pltpu.VMEM((rows, BW), F32)
    return pl.pallas_call(
        _rwkv_kernel,
        grid=(b, t // tt),
        in_specs=[
            pl.BlockSpec((1, tt, 3 * BW), lambda i, j: (i, j, C_RKV // (3 * BW))),
            pl.BlockSpec((1, tt, RW_LORA), lambda i, j: (i, j, C_LORA // RW_LORA)),
            pl.BlockSpec((1, 1, 3 * BW), lambda i, j: (i, 0, 0)),
            pl.BlockSpec((1, 1, RW_LORA), lambda i, j: (i, 0, 0)),
            pl.BlockSpec((1, N_HEADS, HEAD_DIM, HEAD_DIM), lambda i, j: (i, 0, 0, 0)),
            full((1, 3 * BW)), full((1, RW_LORA)),
            full((1, BW)), full((64, BW)), full((1, BW)), full((64, BW)), full((128, BW)),
            full((1, BW)), full((1, BW)),
            full((1, BW)), full((1, BW)), full((1, BW)),
        ],
        out_specs=[
            pl.BlockSpec((1, tt, BW), lambda i, j: (i, j, 0)),
            pl.BlockSpec((1, N_HEADS, HEAD_DIM, HEAD_DIM), lambda i, j: (i, 0, 0, 0)),
        ],
        out_shape=[
            jax.ShapeDtypeStruct((b, t, BW), BF16),
            jax.ShapeDtypeStruct((b, N_HEADS, HEAD_DIM, HEAD_DIM), F32),
        ],
        scratch_shapes=[
            pltpu.VMEM((1, 3 * BW), F32), pltpu.VMEM((1, RW_LORA), F32),
            pltpu.VMEM((NPAIR, HEAD_DIM, 128), F32),
            wide(tt), wide(8 * (tt // CHUNK)), wide(tt), wide(tt),
            scr(), scr(), scr(), scr(), scr(), scr(), scr(),
            pltpu.VMEM((2, tt // CHUNK, NPAIR, 2 * CHUNK, 128), BF16),
            pltpu.VMEM((tt // CHUNK, NPAIR, CHUNK, 128), F32),
            pltpu.VMEM((tt // CHUNK, NPAIR, CHUNK, 256), BF16),
        ],
        compiler_params=_cparams(("parallel", "arbitrary")),
        name="rwkv7",
    )(u, u, sh_rkv, sh_lora, s0, row(mu[:3 * BW]), row(mu[3 * BW:]),
      row(p["rw_w0"]), p["rw_w2"], row(p["rw_a0"]), p["rw_a2"], p["rw_g2"],
      row(p["rw_kk"]), row(p["rw_ka"]), row(p["rw_rk"]), row(p["rw_ln_w"]), row(p["rw_ln_b"]))


def _pool_kernel(pos0, u_ref, prev_ref, hist_ref, w_ref, scale_ref, y_ref):
    tt = u_ref.shape[1]
    i = pl.program_id(1)
    x = u_ref[0]
    prev = jnp.where(i == 0, hist_ref[0], prev_ref[0])
    s = jnp.concatenate([prev, x], axis=0)
    pos = pos0 + i * tt + lax.broadcasted_iota(jnp.int32, (tt, 1), 0)
    outs = []
    for gi, win in enumerate(POOL_WINDOWS):
        s = s + pltpu.roll(s, win // 2, 0)
        sl = slice(gi * POOL_GW, (gi + 1) * POOL_GW)
        cnt = jnp.minimum(pos + 1, win).astype(F32)
        pg = s[16:, sl] / cnt - x[:, sl]
        outs.append(_mm(pg, w_ref[gi]))
    y_ref[0] = _bf(jnp.concatenate(outs, axis=1) * scale_ref[...])


def _pool(u, hist16, pos0, w_pool, scale, tt):
    b, t, _ = u.shape
    nprev = tt // 16
    return pl.pallas_call(
        functools.partial(_pool_kernel, pos0),
        grid=(b, t // tt),
        in_specs=[
            pl.BlockSpec((1, tt, BW), lambda i, j: (i, j, C_POOL // BW)),
            pl.BlockSpec((1, 16, BW), lambda i, j: (i, jnp.maximum(j * nprev - 1, 0), C_POOL // BW)),
            pl.BlockSpec((1, 16, BW), lambda i, j: (i, 0, 0)),
            pl.BlockSpec((4, POOL_GW, POOL_GW), lambda i, j: (0, 0, 0)),
            pl.BlockSpec((1, BW), lambda i, j: (0, 0)),
        ],
        out_specs=pl.BlockSpec((1, tt, BW), lambda i, j: (i, j, 0)),
        out_shape=jax.ShapeDtypeStruct((b, t, BW), BF16),
        compiler_params=_cparams(("parallel", "parallel")),
        name="pool",
    )(u, u, hist16, w_pool, scale.reshape(1, BW))


def _swa_kernel(mask_history, q_ref, kp2_ref, kp1_ref, kc_ref, vp2_ref, vp1_ref, vc_ref, sink_ref, y_ref):
    tq = q_ref.shape[1]
    nc = tq // CHUNK
    i = pl.program_id(1)
    q = q_ref[0]
    k_all = jnp.concatenate([kp2_ref[0], kp1_ref[0], kc_ref[0]], axis=0)
    v_all = jnp.concatenate([vp2_ref[0], vp1_ref[0], vc_ref[0]], axis=0)
    nk = 3 * CHUNK
    qi = lax.broadcasted_iota(jnp.int32, (CHUNK, nk), 0)
    si = lax.broadcasted_iota(jnp.int32, (CHUNK, nk), 1)
    dist1 = jnp.abs(qi + 2 * CHUNK - si).astype(F32)
    dist = jnp.concatenate([dist1] * SWA_GROUP, axis=0)
    scol = lax.broadcasted_iota(jnp.int32, (SWA_GROUP * CHUNK, nk), 1)
    units = [(c, g) for c in range(nc) for g in range(SWA_KV)]
    bias = []
    sinks = []
    for g in range(SWA_KV):
        heads = [g * SWA_GROUP + hh for hh in range(SWA_GROUP)]
        slope = jnp.concatenate(
            [jnp.full((CHUNK, 1), 2.0 ** (-8.0 * (h + 1) / N_HEADS), F32) for h in heads], axis=0)
        bias.append(slope * dist)
        sinks.append(jnp.concatenate([jnp.full((CHUNK, 1), sink_ref[h], F32) for h in heads], axis=0))

    def scores(c, g):
        ks = k_all[c * CHUNK:c * CHUNK + nk, g * HEAD_DIM:(g + 1) * HEAD_DIM]
        qs = jnp.concatenate(
            [q[c * CHUNK:(c + 1) * CHUNK, (g * SWA_GROUP + hh) * HEAD_DIM:(g * SWA_GROUP + hh + 1) * HEAD_DIM]
             for hh in range(SWA_GROUP)], axis=0)
        return _mm_nt(qs, ks)

    def probs(s, c, g):
        s = s * (HEAD_DIM ** -0.5) - bias[g]
        if mask_history:
            s = jnp.where(scol >= (2 - (i * nc + c)) * CHUNK, s, -1e30)
        m = jnp.maximum(jnp.max(s, axis=-1, keepdims=True), sinks[g])
        p = _bf(jnp.exp(s - m))
        rsum = jnp.dot(p, ones_k, preferred_element_type=F32)
        m_wide = jnp.broadcast_to(m, rsum.shape)
        inv = 1.0 / (rsum + jnp.exp(jnp.broadcast_to(sinks[g], rsum.shape) - m_wide))
        return p, inv[:, 0:HEAD_DIM]

    ones_k = jnp.ones((nk, 128), BF16)
    s_all = [scores(c, g) for c, g in units]
    p_all = [probs(s, c, g) for s, (c, g) in zip(s_all, units)]
    o_all = [_mm(p, v_all[c * CHUNK:c * CHUNK + nk, g * HEAD_DIM:(g + 1) * HEAD_DIM]) * inv
             for (p, inv), (c, g) in zip(p_all, units)]
    for o, (c, g) in zip(o_all, units):
        for hh in range(SWA_GROUP):
            h = g * SWA_GROUP + hh
            y_ref[0, c * CHUNK:(c + 1) * CHUNK, h * HEAD_DIM:(h + 1) * HEAD_DIM] = _bf(
                o[hh * CHUNK:(hh + 1) * CHUNK])


def _swa(u, hist_k, hist_v, sinks, tq):
    b, t, _ = u.shape
    per = tq // CHUNK
    kvw = SWA_KV * HEAD_DIM
    mask_history = hist_k is None
    cur_k = pl.BlockSpec((1, tq, kvw), lambda i, j: (i, j, C_KS // kvw))
    cur_v = pl.BlockSpec((1, tq, kvw), lambda i, j: (i, j, C_VS // kvw))
    if mask_history:
        prev = lambda d, col: pl.BlockSpec(
            (1, CHUNK, kvw), lambda i, j: (i, jnp.maximum(j * per - d, 0), col))
        specs = [prev(2, C_KS // kvw), prev(1, C_KS // kvw), cur_k,
                 prev(2, C_VS // kvw), prev(1, C_VS // kvw), cur_v]
        args = (u, u, u, u, u, u)
    else:
        assert t == tq
        hk = hist_k.reshape(b, WINDOW, kvw)
        hv = hist_v.reshape(b, WINDOW, kvw)
        hist = lambda blk: pl.BlockSpec((1, CHUNK, kvw), lambda i, j: (i, blk, 0))
        specs = [hist(0), hist(1), cur_k, hist(0), hist(1), cur_v]
        args = (hk, hk, u, hv, hv, u)
    return pl.pallas_call(
        functools.partial(_swa_kernel, mask_history),
        grid=(b, t // tq),
        in_specs=[pl.BlockSpec((1, tq, BW), lambda i, j: (i, j, C_Q // BW))] + specs
        + [pl.BlockSpec(memory_space=pltpu.SMEM)],
        out_specs=pl.BlockSpec((1, tq, BW), lambda i, j: (i, j, 0)),
        out_shape=jax.ShapeDtypeStruct((b, t, BW), BF16),
        compiler_params=_cparams(("parallel", "parallel")),
        name="swa",
    )(u, *args, sinks)


def _ssd_kernel(z_ref, x_ref, b_ref, c_ref, dt_ref, hx_ref, hb_ref, hc_ref, s0_ref,
                cwx_ref, cwb_ref, cwc_ref, cbx_ref, cbb_ref, cbc_ref,
                dtb_ref, alog_ref, dsk_ref, nw_ref,
                y_ref, sfin_ref,
                px_sc, pb_sc, pc_sc, s_sc, xa_sc, ba_sc, ca_sc, acs_sc, gend_sc,
                acsx_sc, eout_sc, y_sc, xdt_sc, xde_sc):
    tt = x_ref.shape[1]
    nc = tt // CHUNK
    i = pl.program_id(1)

    @pl.when(i == 0)
    def _():
        px_sc[...] = hx_ref[0]
        pb_sc[...] = hb_ref[0]
        pc_sc[...] = hc_ref[0]
        s_sc[...] = s0_ref[0].reshape(N_HEADS * HEAD_DIM, SSM_STATE)

    rid8 = lax.broadcasted_iota(jnp.int32, (8, 1), 0)

    def conv_silu(x, prev8, w_ref, b_ref_):
        acc = None
        for wi in range(SSM_CONV):
            sh = SSM_CONV - 1 - wi
            if sh == 0:
                xs = x
            else:
                rolled = pltpu.roll(x, sh, 0)
                top = jnp.where(rid8 < sh, pltpu.roll(prev8, sh, 0), rolled[0:8])
                xs = jnp.concatenate([top, rolled[8:]], axis=0)
            term = xs * w_ref[wi:wi + 1, :]
            acc = (b_ref_[...] + term) if acc is None else acc + term
        return _silu(acc)

    xr = x_ref[0]
    br = b_ref[0]
    cr = c_ref[0]
    xa_sc[...] = conv_silu(xr, px_sc[...], cwx_ref, cbx_ref)
    ba_sc[...] = conv_silu(br, pb_sc[...], cwb_ref, cbb_ref)
    ca_sc[...] = conv_silu(cr, pc_sc[...], cwc_ref, cbc_ref)
    px_sc[...] = xr[tt - 8:tt, :]
    pb_sc[...] = br[tt - 8:tt, :]
    pc_sc[...] = cr[tt - 8:tt, :]
    hl = 128
    dt = _softplus(dt_ref[0][:, 0:hl] + dtb_ref[...])
    ad = -jnp.exp(alog_ref[...]) * dt

    tri_incl_bf = _tri(CHUNK).astype(BF16)
    sel16 = (lax.broadcasted_iota(jnp.int32, (N_HEADS, hl), 0)
             == lax.broadcasted_iota(jnp.int32, (N_HEADS, hl), 1)).astype(BF16)
    max_terms = 3
    srow = lax.broadcasted_iota(jnp.int32, (hl, BW), 0)
    spread = (((srow & (N_HEADS - 1)) == lax.broadcasted_iota(jnp.int32, (hl, BW), 1) // HEAD_DIM)
              & (srow < max_terms * N_HEADS)).astype(BF16)
    head_lane = lax.broadcasted_iota(jnp.int32, (1, hl), 1) < N_HEADS

    def per_channel(x, terms):
        rem = jnp.where(head_lane, x, 0.0)
        packed = None
        for t in range(terms):
            part = _bf(rem).astype(F32)
            if t < terms - 1:
                rem = rem - part
            placed = part if t == 0 else pltpu.roll(part, t * N_HEADS, 1)
            packed = placed if packed is None else packed + placed
        return jnp.dot(_bf(packed), spread, preferred_element_type=F32)

    acs = jnp.concatenate(
        [_mm_sel(tri_incl_bf, ad[c * CHUNK:(c + 1) * CHUNK, :]) for c in range(nc)], axis=0)
    a_last = jnp.concatenate(
        [jnp.broadcast_to(acs[(c + 1) * CHUNK - 1:(c + 1) * CHUNK, :], (CHUNK, hl)) for c in range(nc)], axis=0)
    acs_sc[...] = acs
    acsx_sc[...] = per_channel(acs, 3)
    eout_sc[...] = per_channel(jnp.exp(acs), 2)
    gend_sc[...] = jnp.exp(jnp.concatenate(
        [jnp.broadcast_to(acs[(c + 1) * CHUNK - 1:(c + 1) * CHUNK, :], (8, hl)) for c in range(nc)], axis=0))
    xdt = xa_sc[...] * per_channel(dt, 2)
    xdt_sc[...] = _bf(xdt)
    xde_sc[...] = _bf(xdt * per_channel(jnp.exp(a_last - acs), 2))

    lane = lax.broadcasted_iota(jnp.int32, (CHUNK, 128), 1)
    row = lax.broadcasted_iota(jnp.int32, (CHUNK, 128), 0)
    tri_incl2 = row >= (lane & (HEAD_DIM - 1))
    r128 = lax.broadcasted_iota(jnp.int32, (128, 128), 0)
    c128 = lax.broadcasted_iota(jnp.int32, (128, 128), 1)
    same_head = (r128 < HEAD_DIM) == (c128 < HEAD_DIM)
    hpg = N_HEADS // SSM_GROUPS
    rpg = hpg * HEAD_DIM

    def chunk_body(c, carry):
        rows = pl.ds(pl.multiple_of(c * CHUNK, CHUNK), CHUNK)
        grow = pl.ds(pl.multiple_of(c * 8, 8), 8)
        bc = _bf(ba_sc[rows, :])
        cc = _bf(ca_sc[rows, :])
        acs_t = _mm_sel_nt(sel16, acs_sc[rows, :])
        g_end = gend_sc[grow, :][0:1, :]
        gsl = [slice(gi * SSM_STATE, (gi + 1) * SSM_STATE) for gi in range(SSM_GROUPS)]
        scores = [_mm_nt(cc[:, gs], bc[:, gs]) for gs in gsl]
        scores2 = [jnp.concatenate([s, s], axis=1) for s in scores]
        y_off = [_mm_nt(cc[:, gsl[gi]], s_sc[gi * rpg:(gi + 1) * rpg, :]) for gi in range(SSM_GROUPS)]
        upd = [_mm_tn(xde_sc[rows, gi * rpg:(gi + 1) * rpg], bc[:, gsl[gi]]) for gi in range(SSM_GROUPS)]
        y_diag = []
        for hp in range(N_HEADS // 2):
            sl = slice(hp * 128, (hp + 1) * 128)
            seg = acsx_sc[rows, sl] - jnp.concatenate([acs_t[2 * hp:2 * hp + 1, :],
                                                       acs_t[2 * hp + 1:2 * hp + 2, :]], axis=1)
            m = scores2[(2 * hp) // hpg] * jnp.exp(jnp.where(tri_incl2, seg, -1e30))
            xd = xdt_sc[rows, sl]
            y_diag.append(_mm(m, jnp.where(same_head, jnp.concatenate([xd, xd], axis=0),
                                           jnp.zeros((), BF16))))
        for h in range(N_HEADS):
            gi, hl_ = divmod(h, hpg)
            hrows = slice(h * HEAD_DIM, (h + 1) * HEAD_DIM)
            s_sc[hrows, :] = (s_sc[hrows, :] * g_end[:, h:h + 1]
                              + upd[gi][hl_ * HEAD_DIM:(hl_ + 1) * HEAD_DIM, :])
        y_sc[rows, :] = (jnp.concatenate(y_diag, axis=1)
                         + jnp.concatenate(y_off, axis=1) * eout_sc[rows, :])
        return carry

    lax.fori_loop(0, nc, chunk_body, 0)

    y = (y_sc[...] + xa_sc[...] * dsk_ref[...]) * _silu(z_ref[0])
    gw = BW // SSM_GROUPS
    parts = []
    for gi in range(SSM_GROUPS):
        yg = y[:, gi * gw:(gi + 1) * gw]
        parts.append(yg * lax.rsqrt(jnp.mean(yg * yg, axis=-1, keepdims=True) + NORM_EPS))
    y_ref[0] = _bf(jnp.concatenate(parts, axis=1) * nw_ref[...])

    @pl.when(i == pl.num_programs(1) - 1)
    def _():
        sfin_ref[0] = s_sc[...].reshape(N_HEADS, HEAD_DIM, SSM_STATE)


def _pad_rows8(a):
    return jnp.pad(a, ((0, 0), (8 - a.shape[1], 0), (0, 0)))


def _pad_lanes(a, n):
    return jnp.pad(a.reshape(1, -1), ((0, 0), (0, n - a.shape[-1])))


def _ssd(u, conv_hist, s0, p, tt):
    b, t, _ = u.shape
    h8 = _pad_rows8(conv_hist)
    hx, hb, hc = h8[:, :, :BW], h8[:, :, BW:BW + SSM_BC], h8[:, :, BW + SSM_BC:]
    cw, cb = p["ssm_conv_w"], p["ssm_conv_b"].reshape(1, -1)
    full = lambda shape: pl.BlockSpec(shape, lambda i, j: (0,) * len(shape))
    col = lambda w, c0: pl.BlockSpec((1, tt, w), lambda i, j: (i, j, c0 // w))
    hist = lambda w: pl.BlockSpec((1, 8, w), lambda i, j: (i, 0, 0))
    return pl.pallas_call(
        _ssd_kernel,
        grid=(b, t // tt),
        in_specs=[
            col(BW, C_Z), col(BW, C_X), col(SSM_BC, C_B), col(SSM_BC, C_C), col(SSM_BC, C_DT),
            hist(BW), hist(SSM_BC), hist(SSM_BC),
            pl.BlockSpec((1, N_HEADS, HEAD_DIM, SSM_STATE), lambda i, j: (i, 0, 0, 0)),
            full((SSM_CONV, BW)), full((SSM_CONV, SSM_BC)), full((SSM_CONV, SSM_BC)),
            full((1, BW)), full((1, SSM_BC)), full((1, SSM_BC)),
            full((1, 128)), full((1, 128)), full((1, BW)), full((1, BW)),
        ],
        out_specs=[
            pl.BlockSpec((1, tt, BW), lambda i, j: (i, j, 0)),
            pl.BlockSpec((1, N_HEADS, HEAD_DIM, SSM_STATE), lambda i, j: (i, 0, 0, 0)),
        ],
        out_shape=[
            jax.ShapeDtypeStruct((b, t, BW), BF16),
            jax.ShapeDtypeStruct((b, N_HEADS, HEAD_DIM, SSM_STATE), F32),
        ],
        scratch_shapes=[
            pltpu.VMEM((8, BW), F32), pltpu.VMEM((8, SSM_BC), F32), pltpu.VMEM((8, SSM_BC), F32),
            pltpu.VMEM((N_HEADS * HEAD_DIM, SSM_STATE), F32),
            pltpu.VMEM((tt, BW), F32), pltpu.VMEM((tt, SSM_BC), F32), pltpu.VMEM((tt, SSM_BC), F32),
            pltpu.VMEM((tt, 128), F32), pltpu.VMEM((8 * (tt // CHUNK), 128), F32),
            pltpu.VMEM((tt, BW), F32), pltpu.VMEM((tt, BW), F32), pltpu.VMEM((tt, BW), F32),
            pltpu.VMEM((tt, BW), BF16), pltpu.VMEM((tt, BW), BF16),
        ],
        compiler_params=_cparams(("parallel", "arbitrary")),
        name="ssd",
    )(u, u, u, u, u, hx, hb, hc, s0,
      cw[:, :BW], cw[:, BW:BW + SSM_BC], cw[:, BW + SSM_BC:],
      cb[:, :BW], cb[:, BW:BW + SSM_BC], cb[:, BW + SSM_BC:],
      _pad_lanes(p["ssm_dt_bias"], 128), _pad_lanes(p["ssm_a_log"], 128),
      jnp.repeat(p["ssm_d"], HEAD_DIM).reshape(1, BW), p["ssm_norm"].reshape(1, BW))


_RELAYOUT_BLOCK = 256


def _w_in_block_order():
    blk = _RELAYOUT_BLOCK
    segs = [(0, 3 * BW), (_N_SSM, BW), (_N_POOL, BW), (_N_SWA, BW), (_N_SSM + BW, BW), (3 * BW, RW_LORA),
            (_N_SWA + BW, 2 * SWA_KV * HEAD_DIM), (_N_SSM + 2 * BW, 2 * SSM_BC + blk)]
    order = [start // blk + i for start, width in segs for i in range(width // blk)]
    assert len(order) == U_COLS // blk and all(start % blk == 0 for start, _ in segs)
    return order


def _relayout_kernel(order_ref, w_ref, o_ref):
    src = order_ref[pl.program_id(1)]
    last = IN_COLS // _RELAYOUT_BLOCK
    valid = jnp.where(src == last, IN_COLS - last * _RELAYOUT_BLOCK, _RELAYOUT_BLOCK)
    out_col = lax.broadcasted_iota(jnp.int32, (_RELAYOUT_BLOCK, 1), 0)
    o_ref[...] = _bf(jnp.where(out_col < valid, w_ref[...], 0.0))


def _prep_w_in(w_in):
    blk = _RELAYOUT_BLOCK
    order = jnp.asarray(_w_in_block_order(), jnp.int32)
    return pl.pallas_call(
        _relayout_kernel,
        grid_spec=pltpu.PrefetchScalarGridSpec(
            num_scalar_prefetch=1,
            grid=(DEPTH, U_COLS // blk),
            in_specs=[pl.BlockSpec((None, blk, D_MODEL), lambda l, j, order: (l, order[j], 0))],
            out_specs=pl.BlockSpec((None, blk, D_MODEL), lambda l, j, order: (l, j, 0)),
        ),
        out_shape=jax.ShapeDtypeStruct((DEPTH, U_COLS, D_MODEL), BF16),
        compiler_params=_cparams(("parallel", "parallel")),
        name="w_in_relayout",
    )(order, jnp.swapaxes(w_in, 1, 2))


def _run_group(x, mods, st, p, final_norm, wts, pos0, bb, tt, tt_branch):
    b, t, _ = x.shape
    kvw = SWA_KV * HEAD_DIM
    outs = {k: [] for k in ("rwkv", "shift", "pool", "k", "v", "ssm", "conv")}
    for l in range(DEPTH):
        pl_ = {k: v[l] for k, v in p.items()}
        mod = mods[l]
        u, h = _inproj(x, pl_["norm_mix"], mod, wts["w_in"], l, bb, tt)
        if st is None:
            shift_hist = jnp.zeros((b, 1, RW_IN), F32)
            s_rwkv = jnp.zeros((b, N_HEADS, HEAD_DIM, HEAD_DIM), F32)
            pool_hist = jnp.zeros((b, POOL_HIST, BW), F32)
            hk = hv = None
            s_ssm = jnp.zeros((b, N_HEADS, HEAD_DIM, SSM_STATE), F32)
            conv_hist = jnp.zeros((b, SSM_CONV - 1, SSM_CONV_DIM), F32)
        else:
            shift_hist, s_rwkv, pool_hist = st["shift"][l], st["rwkv"][l], st["pool"][l]
            hk, hv, s_ssm, conv_hist = st["k"][l], st["v"][l], st["ssm"][l], st["conv"][l]
        y_a, n_rwkv = _rwkv(u, shift_hist, s_rwkv, pl_, min(tt_branch, 256))
        hist16 = jnp.pad(pool_hist, ((0, 0), (1, 0), (0, 0)))
        y_b = _pool(u, hist16, pos0, wts["pool_w"][l], pl_["pool_scale"], tt_branch)
        y_c = _swa(u, hk, hv, pl_["swa_sinks"], min(tt_branch, 256))
        y_d, n_ssm = _ssd(u, conv_hist, s_ssm, pl_, tt_branch)
        merged = _merge(h, (y_a, y_b, y_c, y_d), wts["w_gate"], wts["w_branch"], l, bb, tt)
        x = _outproj(merged, wts["w_out"], l, x, mod, bb, tt)
        x = _ffn(x, pl_["norm_ffn"], mod, wts["w_up"], wts["w_down"], l, final_norm, l == DEPTH - 1,
                 bb, tt)

        outs["rwkv"].append(n_rwkv)
        outs["shift"].append(jnp.concatenate(
            [u[:, t - 1:, C_RKV:C_RKV + 3 * BW], u[:, t - 1:, C_LORA:C_LORA + RW_LORA]], axis=-1))
        outs["pool"].append(u[:, t - POOL_HIST:, C_POOL:C_POOL + BW])
        k_new = u[:, :, C_KS:C_KS + kvw]
        v_new = u[:, :, C_VS:C_VS + kvw]
        if hk is not None:
            k_new = jnp.concatenate([hk.reshape(b, WINDOW, kvw), k_new], axis=1)
            v_new = jnp.concatenate([hv.reshape(b, WINDOW, kvw), v_new], axis=1)
        outs["k"].append(k_new[:, -WINDOW:].reshape(b, WINDOW, SWA_KV, HEAD_DIM))
        outs["v"].append(v_new[:, -WINDOW:].reshape(b, WINDOW, SWA_KV, HEAD_DIM))
        outs["ssm"].append(n_ssm)
        outs["conv"].append(jnp.concatenate(
            [u[:, t - 3:, C_X:C_X + BW], u[:, t - 3:, C_B:C_B + 2 * SSM_BC]], axis=-1))
    order = ("rwkv", "shift", "pool", "k", "v", "ssm", "conv")
    return x, tuple(jnp.stack(outs[k]) for k in order)


def kernel(x_prompt, x_sample, state_rwkv, state_rwkv_shift, state_pool, cache_swa_k, cache_swa_v,
           state_ssm, state_ssm_conv, c_prompt, c_sample, ada_w, ada_b, norm_mix, norm_ffn, w_in,
           rw_mu, rw_w0, rw_w2, rw_a0, rw_a2, rw_g2, rw_kk, rw_ka, rw_rk, rw_ln_w, rw_ln_b, pool_w,
           pool_scale, swa_sinks, ssm_conv_w, ssm_conv_b, ssm_dt_bias, ssm_a_log, ssm_d, ssm_norm,
           w_gate, w_branch, w_out, w_up, w_down, final_norm):
    bp, tp, _ = x_prompt.shape
    bs, ts, _ = x_sample.shape
    p = dict(norm_mix=norm_mix, norm_ffn=norm_ffn, rw_mu=rw_mu, rw_w0=rw_w0, rw_w2=rw_w2, rw_a0=rw_a0,
             rw_a2=rw_a2, rw_g2=rw_g2, rw_kk=rw_kk, rw_ka=rw_ka, rw_rk=rw_rk, rw_ln_w=rw_ln_w,
             rw_ln_b=rw_ln_b, pool_scale=pool_scale, swa_sinks=swa_sinks, ssm_conv_w=ssm_conv_w,
             ssm_conv_b=ssm_conv_b, ssm_dt_bias=ssm_dt_bias, ssm_a_log=ssm_a_log, ssm_d=ssm_d,
             ssm_norm=ssm_norm)
    wts = dict(w_in=_prep_w_in(w_in), pool_w=pool_w.astype(BF16), w_gate=w_gate.astype(BF16),
               w_branch=w_branch.astype(BF16), w_out=w_out.astype(BF16), w_up=w_up.astype(BF16),
               w_down=w_down.astype(BF16))

    nb = bp + bs
    nb_pad = -(-nb // 8) * 8
    c_all = jnp.pad(jnp.concatenate([c_prompt, c_sample], axis=0), ((0, nb_pad - nb), (0, 0)))
    mod_all = _adaln(c_all, ada_w, ada_b)
    mods_p = mod_all[:, :bp].reshape(DEPTH, bp, 6, 1, D_MODEL)
    mods_s = mod_all[:, bp:nb].reshape(DEPTH, bs, 6, 1, D_MODEL)

    y_prompt, st_p = _run_group(x_prompt, mods_p, None, p, final_norm, wts, 0,
                                1, min(tp, 1024), min(tp, 512))
    st_s = dict(rwkv=state_rwkv, shift=state_rwkv_shift, pool=state_pool, k=cache_swa_k,
                v=cache_swa_v, ssm=state_ssm, conv=state_ssm_conv)
    y_sample, st_o = _run_group(x_sample, mods_s, st_s, p, final_norm, wts, PAST_LEN, bs, ts, ts)
    return (y_prompt, y_sample) + st_p + st_o
```

```python
import functools
import math

import jax
import jax.numpy as jnp
from jax import lax
from jax.experimental import pallas as pl
from jax.experimental.pallas import tpu as pltpu

F32 = jnp.float32
BF16 = jnp.bfloat16

D_MODEL = 2048
DEPTH = 4
PAST_LEN = 1024
CHUNK = 64
HEAD_DIM = 64
BW = D_MODEL // 2
D_FF = 4 * D_MODEL
NORM_EPS = 1e-6
N_HEADS = BW // HEAD_DIM
RW_LORA = 256
RW_IN = 3 * BW + RW_LORA
RW_GN_EPS = HEAD_DIM * 1e-5
POOL_WINDOWS = (2, 4, 8, 16)
POOL_GW = BW // 4
POOL_HIST = 15
SWA_KV = 4
SWA_GROUP = N_HEADS // SWA_KV
WINDOW = 128
SSM_GROUPS = 2
SSM_STATE = 128
SSM_CONV = 4
SSM_BC = SSM_GROUPS * SSM_STATE
SSM_CONV_DIM = BW + 2 * SSM_BC

C_RKV = 0
C_Z = 3072
C_POOL = 4096
C_Q = 5120
C_X = 6144
C_LORA = 7168
C_KS = 7424
C_VS = 7680
C_B = 7936
C_C = 8192
C_DT = 8448
U_COLS = 8704

_N_POOL = RW_IN
_N_SWA = _N_POOL + BW
_N_SSM = _N_SWA + (N_HEADS + 2 * SWA_KV) * HEAD_DIM
IN_COLS = _N_SSM + BW + SSM_CONV_DIM + N_HEADS

VMEM_LIMIT = 56 * 1024 * 1024

ROWS_MERGE = 1024
ROWS_WIDE = 512
ROWS_MIXER = 512
ROWS_RWKV = 256
ROWS_SWA = 256
TN_ADALN = 1024
TN_INPROJ = U_COLS // 4
TN_MERGE = 256
TF_FFN = 1024


def _cparams(sem):
    return pltpu.CompilerParams(dimension_semantics=sem, vmem_limit_bytes=VMEM_LIMIT)


def _bf(x):
    return x.astype(BF16)


def _mm(a, b):
    return jnp.dot(_bf(a), _bf(b), preferred_element_type=F32)


def _mm_nt(a, b):
    return lax.dot_general(_bf(a), _bf(b), (((1,), (1,)), ((), ())), preferred_element_type=F32)


def _mm_tn(a, b):
    return lax.dot_general(_bf(a), _bf(b), (((0,), (0,)), ((), ())), preferred_element_type=F32)


def _split(x, terms):
    parts = []
    for _ in range(terms - 1):
        hi = _bf(x)
        parts.append(hi)
        x = x - hi.astype(F32)
    parts.append(_bf(x))
    return parts


def _mm_sel(c, x, terms=3):
    acc = None
    for part in _split(x, terms):
        d = jnp.dot(c, part, preferred_element_type=F32)
        acc = d if acc is None else acc + d
    return acc


def _mm_sel_nt(c, x, terms=3):
    acc = None
    for part in _split(x, terms):
        d = lax.dot_general(c, part, (((1,), (1,)), ((), ())), preferred_element_type=F32)
        acc = d if acc is None else acc + d
    return acc


def _softplus(x):
    return jnp.maximum(x, 0.0) + jnp.log1p(jnp.exp(-jnp.abs(x)))


def _sigmoid(x):
    return 0.5 * jnp.tanh(0.5 * x) + 0.5


def _silu(x):
    return x * _sigmoid(x)


def _tri(n, strict=False):
    r = lax.broadcasted_iota(jnp.int32, (n, n), 0)
    c = lax.broadcasted_iota(jnp.int32, (n, n), 1)
    return (r > c) if strict else (r >= c)


def _adaln_kernel(c_ref, w_ref, b_ref, o_ref):
    s = _silu(c_ref[...])
    o_ref[0] = _mm(s, w_ref[0]) + b_ref[0]


def _adaln(c_all, ada_w, ada_b):
    nb = c_all.shape[0]
    tn = TN_ADALN
    n_out = ada_w.shape[2]
    return pl.pallas_call(
        _adaln_kernel,
        grid=(DEPTH, n_out // tn),
        in_specs=[
            pl.BlockSpec((nb, D_MODEL), lambda l, n: (0, 0)),
            pl.BlockSpec((1, D_MODEL, tn), lambda l, n: (l, 0, n)),
            pl.BlockSpec((1, 1, tn), lambda l, n: (l, 0, n)),
        ],
        out_specs=pl.BlockSpec((1, nb, tn), lambda l, n: (l, 0, n)),
        out_shape=jax.ShapeDtypeStruct((DEPTH, nb, n_out), F32),
        compiler_params=_cparams(("parallel", "parallel")),
        name="adaln",
    )(c_all, ada_w, ada_b.reshape(DEPTH, 1, n_out))


def _norm_mod(x, g, sc, sh):
    y = x * lax.rsqrt(jnp.mean(x * x, axis=-1, keepdims=True) + NORM_EPS)
    return (y * g) * (1.0 + sc) + sh


def _inproj_kernel(x_ref, g_ref, sc_ref, sh_ref, w_ref, u_ref, h_ref):
    bb, tt, _ = x_ref.shape

    @pl.when(pl.program_id(2) == 0)
    def _():
        h = _norm_mod(x_ref[...], g_ref[...], sc_ref[:, 0], sh_ref[:, 0])
        h_ref[...] = _bf(h)

    h = h_ref[...].reshape(bb * tt, D_MODEL)
    u = lax.dot_general(h, w_ref[...], (((1,), (1,)), ((), ())), preferred_element_type=F32)
    u_ref[...] = u.reshape(u_ref.shape)


def _inproj(x, g, mod, w, l, bb, tt):
    b, t, _ = x.shape
    tn = TN_INPROJ
    return pl.pallas_call(
        _inproj_kernel,
        grid=(b // bb, t // tt, U_COLS // tn),
        in_specs=[
            pl.BlockSpec((bb, tt, D_MODEL), lambda i, j, n: (i, j, 0)),
            pl.BlockSpec((1, 1, D_MODEL), lambda i, j, n: (0, 0, 0)),
            pl.BlockSpec((bb, 1, 1, D_MODEL), lambda i, j, n: (i, 1, 0, 0)),
            pl.BlockSpec((bb, 1, 1, D_MODEL), lambda i, j, n: (i, 0, 0, 0)),
            pl.BlockSpec((None, tn, D_MODEL), lambda i, j, n: (l, n, 0)),
        ],
        out_specs=[
            pl.BlockSpec((bb, tt, tn), lambda i, j, n: (i, j, n)),
            pl.BlockSpec((bb, tt, D_MODEL), lambda i, j, n: (i, j, 0)),
        ],
        out_shape=[
            jax.ShapeDtypeStruct((b, t, U_COLS), F32),
            jax.ShapeDtypeStruct((b, t, D_MODEL), BF16),
        ],
        compiler_params=_cparams(("parallel", "parallel", "arbitrary")),
        name="inproj",
    )(x, g.reshape(1, 1, D_MODEL), mod, mod, w)


def _merge_kernel(h_ref, ya_ref, yb_ref, yc_ref, yd_ref, wg_ref, wb_ref, o_ref):
    bb, tt, _ = h_ref.shape
    m = bb * tt
    h = h_ref[...].reshape(m, D_MODEL)
    acc = None
    for i, y_ref in enumerate((ya_ref, yb_ref, yc_ref, yd_ref)):
        gate = _sigmoid(jnp.dot(h, wg_ref[i], preferred_element_type=F32))
        br = jnp.dot(y_ref[...].reshape(m, BW), wb_ref[i], preferred_element_type=F32)
        acc = gate * br if acc is None else acc + gate * br
    o_ref[...] = _bf(acc).reshape(o_ref.shape)


def _merge(h, ys, wg, wb, l, bb, tt):
    b, t, _ = h.shape
    tn = TN_MERGE
    yspec = pl.BlockSpec((bb, tt, BW), lambda i, j, n: (i, j, 0))
    return pl.pallas_call(
        _merge_kernel,
        grid=(b // bb, t // tt, D_MODEL // tn),
        in_specs=[
            pl.BlockSpec((bb, tt, D_MODEL), lambda i, j, n: (i, j, 0)),
            yspec, yspec, yspec, yspec,
            pl.BlockSpec((None, 4, D_MODEL, tn), lambda i, j, n: (l, 0, 0, n)),
            pl.BlockSpec((None, 4, BW, tn), lambda i, j, n: (l, 0, 0, n)),
        ],
        out_specs=pl.BlockSpec((bb, tt, tn), lambda i, j, n: (i, j, n)),
        out_shape=jax.ShapeDtypeStruct((b, t, D_MODEL), BF16),
        compiler_params=_cparams(("parallel", "parallel", "arbitrary")),
        name="merge",
    )(h, *ys, wg, wb)


def _outproj_kernel(m_ref, w_ref, x_ref, g1_ref, gn_ref, sc_ref, sh_ref, o_ref, h_ref):
    bb, tt, _ = m_ref.shape
    y = jnp.dot(m_ref[...].reshape(bb * tt, D_MODEL), w_ref[...], preferred_element_type=F32)
    x1 = x_ref[...] + g1_ref[:, 0] * y.reshape(o_ref.shape)
    o_ref[...] = x1
    h_ref[...] = _bf(_norm_mod(x1, gn_ref[...], sc_ref[:, 0], sh_ref[:, 0]))


def _outproj(merged, w, l, x, mod, g_ffn, bb, tt):
    b, t, _ = x.shape
    row = pl.BlockSpec((bb, tt, D_MODEL), lambda i, j: (i, j, 0))
    modk = lambda k: pl.BlockSpec((bb, 1, 1, D_MODEL), lambda i, j: (i, k, 0, 0))
    return pl.pallas_call(
        _outproj_kernel,
        grid=(b // bb, t // tt),
        in_specs=[
            row,
            pl.BlockSpec((None, D_MODEL, D_MODEL), lambda i, j: (l, 0, 0)),
            row,
            modk(2),
            pl.BlockSpec((1, 1, D_MODEL), lambda i, j: (0, 0, 0)),
            modk(4), modk(3),
        ],
        out_specs=[row, row],
        out_shape=[jax.ShapeDtypeStruct((b, t, D_MODEL), F32), jax.ShapeDtypeStruct((b, t, D_MODEL), BF16)],
        compiler_params=_cparams(("parallel", "parallel")),
        name="outproj",
    )(merged, w, x, mod, g_ffn.reshape(1, 1, D_MODEL), mod, mod)


def _ffn_kernel(final, x_ref, h_ref, gate_ref, wu_ref, wd_ref, fg_ref, o_ref):
    bb, tt, _ = x_ref.shape
    f = pl.program_id(2)

    @pl.when(f == 0)
    def _():
        o_ref[...] = jnp.zeros(o_ref.shape, F32)

    a = jnp.dot(h_ref[...].reshape(bb * tt, D_MODEL), wu_ref[...], preferred_element_type=F32)
    a = jnp.square(jnp.maximum(a, 0.0))
    o_ref[...] += jnp.dot(_bf(a), wd_ref[...], preferred_element_type=F32).reshape(o_ref.shape)

    @pl.when(f == pl.num_programs(2) - 1)
    def _():
        y = x_ref[...] + gate_ref[:, 0] * o_ref[...]
        if final:
            y = (y * lax.rsqrt(jnp.mean(y * y, axis=-1, keepdims=True) + NORM_EPS)) * fg_ref[...]
        o_ref[...] = y


def _ffn(x, h2, mod, wu, wd, l, final_g, final, bb, tt):
    b, t, _ = x.shape
    tf = TF_FFN
    row = pl.BlockSpec((bb, tt, D_MODEL), lambda i, j, f: (i, j, 0))
    return pl.pallas_call(
        functools.partial(_ffn_kernel, final),
        grid=(b // bb, t // tt, D_FF // tf),
        in_specs=[
            row, row,
            pl.BlockSpec((bb, 1, 1, D_MODEL), lambda i, j, f: (i, 5, 0, 0)),
            pl.BlockSpec((None, D_MODEL, tf), lambda i, j, f: (l, 0, f)),
            pl.BlockSpec((None, tf, D_MODEL), lambda i, j, f: (l, f, 0)),
            pl.BlockSpec((1, 1, D_MODEL), lambda i, j, f: (0, 0, 0)),
        ],
        out_specs=row,
        out_shape=jax.ShapeDtypeStruct((b, t, D_MODEL), F32),
        compiler_params=_cparams(("parallel", "parallel", "arbitrary")),
        name="ffn",
    )(x, h2, mod, wu, wd, final_g.reshape(1, 1, D_MODEL))


NPAIR = N_HEADS // 2


def _rwkv_kernel(rkv_ref, lora_ref, sh_rkv_ref, sh_lora_ref, s0_ref, mu_rkv_ref, mu_lora_ref,
                 w0_ref, w2_ref, a0_ref, a2_ref, g2_ref, kkp_ref, ka_ref, rk_ref, lnw_ref, lnb_ref,
                 y_ref, sfin_ref,
                 prev_rkv, prev_lora, s_sc, g_sc, gend_sc, bonus_sc, yo_sc,
                 rt_sc, pt_sc, qt_sc, kt_sc, qe_sc, ke_sc, v_sc, lhs_sc, z_sc, arqk_sc):
    tt = rkv_ref.shape[1]
    nc = tt // CHUNK
    i = pl.program_id(1)

    @pl.when(i == 0)
    def _():
        prev_rkv[...] = sh_rkv_ref[0]
        prev_lora[...] = sh_lora_ref[0]
        for hp in range(NPAIR):
            s_sc[hp] = jnp.concatenate([s0_ref[0, 2 * hp], s0_ref[0, 2 * hp + 1]], axis=1)

    rid8 = lax.broadcasted_iota(jnp.int32, (8, 1), 0)

    def tshift(x, prev_row, mu):
        rolled = pltpu.roll(x, 1, 0)
        prev = jnp.concatenate([jnp.where(rid8 == 0, prev_row, rolled[0:8]), rolled[8:]], axis=0)
        return x + (prev - x) * mu

    u = rkv_ref[0]
    ul = lora_ref[0]
    xs = tshift(u, prev_rkv[...], mu_rkv_ref[...])
    xl = tshift(ul, prev_lora[...], mu_lora_ref[...])
    prev_rkv[...] = u[tt - 1:tt, :]
    prev_lora[...] = ul[tt - 1:tt, :]

    r = xs[:, 0:BW]
    k = xs[:, BW:2 * BW]
    v = xs[:, 2 * BW:3 * BW]
    wd = xl[:, 0:64]
    ad = xl[:, 64:128]
    gd = xl[:, 128:256]
    lw = -math.exp(-0.5) * _sigmoid(w0_ref[...] + _mm(jnp.tanh(wd), w2_ref[...]))
    a = _sigmoid(a0_ref[...] + _mm(ad, a2_ref[...]))
    g_sc[...] = _mm(_sigmoid(gd), g2_ref[...])
    kkr = k * kkp_ref[...]
    kh = k * (1.0 + (a - 1.0) * ka_ref[...])

    lane = lax.broadcasted_iota(jnp.int32, (CHUNK, 128), 1)
    row = lax.broadcasted_iota(jnp.int32, (CHUNK, 128), 0)
    col_in_head = lane & (HEAD_DIM - 1)
    tri_strict2 = row > col_in_head
    tri_incl2 = row >= col_in_head
    eye2 = (row == col_in_head).astype(F32)
    first_head = lane < HEAD_DIM
    r128 = lax.broadcasted_iota(jnp.int32, (128, 128), 0)
    c128 = lax.broadcasted_iota(jnp.int32, (128, 128), 1)
    same_head = (r128 < HEAD_DIM) == (c128 < HEAD_DIM)
    ones_bd = same_head.astype(BF16)
    tri_incl_bf = _tri(CHUNK).astype(BF16)

    def head_sum(x, terms):
        acc = None
        for part in _split(x, terms):
            d = jnp.dot(part, ones_bd, preferred_element_type=F32)
            acc = d if acc is None else acc + d
        return acc

    cl = jnp.concatenate(
        [_mm_sel(tri_incl_bf, lw[c * CHUNK:(c + 1) * CHUNK, :], terms=2) for c in range(nc)], axis=0)
    cl_last = jnp.concatenate(
        [jnp.broadcast_to(cl[(c + 1) * CHUNK - 1:(c + 1) * CHUNK, :], (CHUNK, BW)) for c in range(nc)], axis=0)
    e_in = jnp.exp(cl)
    e_ex = jnp.exp(cl - lw)
    e_inv = jnp.exp(-cl)
    e_end = jnp.exp(cl_last - cl)
    gend_sc[...] = jnp.exp(jnp.concatenate(
        [jnp.broadcast_to(cl[(c + 1) * CHUNK - 1:(c + 1) * CHUNK, :], (8, BW)) for c in range(nc)], axis=0))
    for hp in range(NPAIR):
        sl = slice(hp * 128, (hp + 1) * 128)
        kk = kkr[:, sl]
        kk = kk * lax.rsqrt(jnp.maximum(head_sum(kk * kk, 2), 1e-24))
        q = kk * a[:, sl]
        rt_sc[hp] = r[:, sl] * e_in[:, sl]
        pt_sc[hp] = -kk * e_ex[:, sl]
        qt_sc[hp] = q * e_inv[:, sl]
        kt_sc[hp] = kh[:, sl] * e_inv[:, sl]
        qe_sc[hp] = q * e_end[:, sl]
        ke_sc[hp] = kh[:, sl] * e_end[:, sl]
        v_sc[hp] = v[:, sl]
        bonus_sc[:, sl] = head_sum(r[:, sl] * kh[:, sl] * rk_ref[:, sl], 1) * v[:, sl]

    def bd(x):
        return jnp.where(same_head, jnp.concatenate([x, x], axis=0), jnp.zeros((), x.dtype))

    def parts(x):
        hi = _bf(x)
        return hi, _bf(x - hi.astype(F32))

    def bd2(p):
        return bd(p[0]), bd(p[1])

    nn, nt, tn = ((1,), (0,)), ((1,), (1,)), ((0,), (0,))

    def dot3_shared(a_list, b, dims):
        d = lambda x, y: lax.dot_general(x, y, (dims, ((), ())), preferred_element_type=F32)
        ax = 1 if dims == tn else 0
        m = a_list[0][0].shape[ax]
        big = d(jnp.concatenate([t for a in a_list for t in a], axis=ax), b[0])
        small = d(jnp.concatenate([a[0] for a in a_list], axis=ax), b[1])
        return [big[2 * i * m:(2 * i + 1) * m] + big[(2 * i + 1) * m:(2 * i + 2) * m] + small[i * m:(i + 1) * m]
                for i in range(len(a_list))]

    def dot3(a, b, dims):
        return dot3_shared([a], b, dims)[0]

    pairs = range(NPAIR)

    cpi = 4 if nc % 4 == 0 else 1

    def intra_body(n, carry):
        units = [(n * cpi + j, hp) for j in range(cpi) for hp in pairs]
        rows = [pl.ds(pl.multiple_of(c * CHUNK, CHUNK), CHUNK) for c, _ in units]
        un = range(len(units))
        pt = [pt_sc[units[k][1], rows[k], :] for k in un]
        rt = [rt_sc[units[k][1], rows[k], :] for k in un]
        lhs = [parts(jnp.concatenate([pt[k], rt[k]], axis=0)) for k in un]
        gq = [_mm_nt(lhs[k][0], bd(_bf(qt_sc[units[k][1], rows[k], :]))) for k in un]
        gk = [_mm_nt(lhs[k][0], bd(_bf(kt_sc[units[k][1], rows[k], :]))) for k in un]
        l_pq = [jnp.where(tri_strict2, gq[k][0:CHUNK], 0.0) for k in un]
        lv = [dot3(parts(jnp.where(tri_strict2, gk[k][0:CHUNK], 0.0)),
                   bd2(parts(v_sc[units[k][1], rows[k], :])), nn) for k in un]
        tinv = [eye2 + l_pq[k] for k in un]
        xpp = [parts(l_pq[k]) for k in un]
        xpp = [parts(dot3(xpp[k], bd2(xpp[k]), nn)) for k in un]
        for it in range(5):
            tp = [parts(tinv[k]) for k in un]
            if it < 4:
                res = [dot3_shared([xpp[k], tp[k]], bd2(xpp[k]), nn) for k in un]
                xpp = [parts(res[k][0]) for k in un]
                tinv = [tinv[k] + res[k][1] for k in un]
            else:
                tinv = [tinv[k] + dot3(tp[k], bd2(xpp[k]), nn) for k in un]
        tp = [parts(tinv[k]) for k in un]
        wmat = [dot3(tp[k], bd2((lhs[k][0][0:CHUNK], lhs[k][1][0:CHUNK])), nn) for k in un]
        for k, (c, hp) in enumerate(units):
            z_sc[c, hp] = dot3(tp[k], bd2(parts(lv[k])), nn)
            for t, part in enumerate(parts(jnp.concatenate([wmat[k], rt[k]], axis=0))):
                lhs_sc[t, c, hp] = part
            arqk_sc[c, hp] = jnp.concatenate(
                [_bf(jnp.where(tri_incl2, gq[k][CHUNK:2 * CHUNK], 0.0)),
                 _bf(jnp.where(tri_incl2, gk[k][CHUNK:2 * CHUNK], 0.0))], axis=1)
        return carry

    def state_body(c, carry):
        rows = pl.ds(pl.multiple_of(c * CHUNK, CHUNK), CHUNK)
        grow = pl.ds(pl.multiple_of(c * 8, 8), 8)
        vv = [parts(v_sc[hp, rows, :]) for hp in pairs]
        s0 = [s_sc[hp] for hp in pairs]
        ps = [dot3((lhs_sc[0, c, hp], lhs_sc[1, c, hp]), bd2(parts(s0[hp])), nt)
              for hp in pairs]
        uu = [parts(ps[hp][0:CHUNK] + z_sc[c, hp]) for hp in pairs]
        for hp in pairs:
            sl = slice(hp * 128, (hp + 1) * 128)
            yo_sc[rows, sl] = ps[hp][CHUNK:2 * CHUNK] + jnp.dot(
                arqk_sc[c, hp], jnp.concatenate([bd(uu[hp][0]), bd(vv[hp][0])], axis=0),
                preferred_element_type=F32)
        for hp in pairs:
            sl = slice(hp * 128, (hp + 1) * 128)
            uv = tuple(jnp.concatenate([uu[hp][t], vv[hp][t]], axis=0) for t in range(2))
            qk = parts(jnp.concatenate([qe_sc[hp, rows, :], ke_sc[hp, rows, :]], axis=0))
            full = dot3(uv, qk, tn)
            g_end = gend_sc[grow, sl][0:1, :]
            s_sc[hp] = s0[hp] * g_end + jnp.where(
                first_head, full[0:HEAD_DIM], full[HEAD_DIM:2 * HEAD_DIM])
        return carry

    lax.fori_loop(0, nc // cpi, intra_body, 0)
    lax.fori_loop(0, nc, state_body, 0)

    for hp in range(NPAIR):
        sl = slice(hp * 128, (hp + 1) * 128)
        yh = yo_sc[:, sl]
        mean = head_sum(yh, 1) * (1.0 / HEAD_DIM)
        d = yh - mean
        var = head_sum(d * d, 1) * (1.0 / HEAD_DIM)
        yn = d * lax.rsqrt(var + RW_GN_EPS) * lnw_ref[:, sl] + lnb_ref[:, sl]
        y_ref[0, :, sl] = _bf((yn + bonus_sc[:, sl]) * g_sc[:, sl])

    @pl.when(i == pl.num_programs(1) - 1)
    def _():
        for hp in range(NPAIR):
            sp = s_sc[hp]
            sfin_ref[0, 2 * hp] = sp[:, 0:HEAD_DIM]
            sfin_ref[0, 2 * hp + 1] = sp[:, HEAD_DIM:2 * HEAD_DIM]


def _rwkv(u, shift_hist, s0, p, tt):
    b, t, _ = u.shape
    row = lambda a: a.reshape(1, -1)
    sh = shift_hist.reshape(b, 1, RW_IN)
    sh_rkv = sh[:, :, :3 * BW]
    sh_lora = sh[:, :, 3 * BW:]
    mu = p["rw_mu"]
    full = lambda shape: pl.BlockSpec(shape, lambda i, j: (0,) * len(shape))
    scr = lambda: pltpu.VMEM((NPAIR, tt, 128), F32)
    wide = lambda rows: pltpu.VMEM((rows, BW), F32)
    return pl.pallas_call(
        _rwkv_kernel,
        grid=(b, t // tt),
        in_specs=[
            pl.BlockSpec((1, tt, 3 * BW), lambda i, j: (i, j, C_RKV // (3 * BW))),
            pl.BlockSpec((1, tt, RW_LORA), lambda i, j: (i, j, C_LORA // RW_LORA)),
            pl.BlockSpec((1, 1, 3 * BW), lambda i, j: (i, 0, 0)),
            pl.BlockSpec((1, 1, RW_LORA), lambda i, j: (i, 0, 0)),
            pl.BlockSpec((1, N_HEADS, HEAD_DIM, HEAD_DIM), lambda i, j: (i, 0, 0, 0)),
            full((1, 3 * BW)), full((1, RW_LORA)),
            full((1, BW)), full((64, BW)), full((1, BW)), full((64, BW)), full((128, BW)),
            full((1, BW)), full((1, BW)),
            full((1, BW)), full((1, BW)), full((1, BW)),
        ],
        out_specs=[
            pl.BlockSpec((1, tt, BW), lambda i, j: (i, j, 0)),
            pl.BlockSpec((1, N_HEADS, HEAD_DIM, HEAD_DIM), lambda i, j: (i, 0, 0, 0)),
        ],
        out_shape=[
            jax.ShapeDtypeStruct((b, t, BW), BF16),
            jax.ShapeDtypeStruct((b, N_HEADS, HEAD_DIM, HEAD_DIM), F32),
        ],
        scratch_shapes=[
            pltpu.VMEM((1, 3 * BW), F32), pltpu.VMEM((1, RW_LORA), F32),
            pltpu.VMEM((NPAIR, HEAD_DIM, 128), F32),
            wide(tt), wide(8 * (tt // CHUNK)), wide(tt), wide(tt),
            scr(), scr(), scr(), scr(), scr(), scr(), scr(),
            pltpu.VMEM((2, tt // CHUNK, NPAIR, 2 * CHUNK, 128), BF16),
            pltpu.VMEM((tt // CHUNK, NPAIR, CHUNK, 128), F32),
            pltpu.VMEM((tt // CHUNK, NPAIR, CHUNK, 256), BF16),
        ],
        compiler_params=_cparams(("parallel", "arbitrary")),
        name="rwkv7",
    )(u, u, sh_rkv, sh_lora, s0, row(mu[:3 * BW]), row(mu[3 * BW:]),
      row(p["rw_w0"]), p["rw_w2"], row(p["rw_a0"]), p["rw_a2"], p["rw_g2"],
      row(p["rw_kk"]), row(p["rw_ka"]), row(p["rw_rk"]), row(p["rw_ln_w"]), row(p["rw_ln_b"]))


def _pool_kernel(pos0, u_ref, prev_ref, hist_ref, w_ref, scale_ref, y_ref):
    tt = u_ref.shape[1]
    i = pl.program_id(1)
    x = u_ref[0]
    prev = jnp.where(i == 0, hist_ref[0], prev_ref[0])
    s = jnp.concatenate([prev, x], axis=0)
    pos = pos0 + i * tt + lax.broadcasted_iota(jnp.int32, (tt, 1), 0)
    outs = []
    for gi, win in enumerate(POOL_WINDOWS):
        s = s + pltpu.roll(s, win // 2, 0)
        sl = slice(gi * POOL_GW, (gi + 1) * POOL_GW)
        cnt = jnp.minimum(pos + 1, win).astype(F32)
        pg = s[16:, sl] / cnt - x[:, sl]
        outs.append(_mm(pg, w_ref[gi]))
    y_ref[0] = _bf(jnp.concatenate(outs, axis=1) * scale_ref[...])


def _pool(u, hist16, pos0, w_pool, scale, tt):
    b, t, _ = u.shape
    nprev = tt // 16
    return pl.pallas_call(
        functools.partial(_pool_kernel, pos0),
        grid=(b, t // tt),
        in_specs=[
            pl.BlockSpec((1, tt, BW), lambda i, j: (i, j, C_POOL // BW)),
            pl.BlockSpec((1, 16, BW), lambda i, j: (i, jnp.maximum(j * nprev - 1, 0), C_POOL // BW)),
            pl.BlockSpec((1, 16, BW), lambda i, j: (i, 0, 0)),
            pl.BlockSpec((4, POOL_GW, POOL_GW), lambda i, j: (0, 0, 0)),
            pl.BlockSpec((1, BW), lambda i, j: (0, 0)),
        ],
        out_specs=pl.BlockSpec((1, tt, BW), lambda i, j: (i, j, 0)),
        out_shape=jax.ShapeDtypeStruct((b, t, BW), BF16),
        compiler_params=_cparams(("parallel", "parallel")),
        name="pool",
    )(u, u, hist16, w_pool, scale.reshape(1, BW))


def _swa_kernel(mask_history, q_ref, kp2_ref, kp1_ref, kc_ref, vp2_ref, vp1_ref, vc_ref, sink_ref, y_ref):
    tq = q_ref.shape[1]
    nc = tq // CHUNK
    i = pl.program_id(1)
    q = q_ref[0]
    k_all = jnp.concatenate([kp2_ref[0], kp1_ref[0], kc_ref[0]], axis=0)
    v_all = jnp.concatenate([vp2_ref[0], vp1_ref[0], vc_ref[0]], axis=0)
    nk = 3 * CHUNK
    qi = lax.broadcasted_iota(jnp.int32, (CHUNK, nk), 0)
    si = lax.broadcasted_iota(jnp.int32, (CHUNK, nk), 1)
    dist1 = jnp.abs(qi + 2 * CHUNK - si).astype(F32)
    dist = jnp.concatenate([dist1] * SWA_GROUP, axis=0)
    scol = lax.broadcasted_iota(jnp.int32, (SWA_GROUP * CHUNK, nk), 1)
    units = [(c, g) for c in range(nc) for g in range(SWA_KV)]
    bias = []
    sinks = []
    for g in range(SWA_KV):
        heads = [g * SWA_GROUP + hh for hh in range(SWA_GROUP)]
        slope = jnp.concatenate(
            [jnp.full((CHUNK, 1), 2.0 ** (-8.0 * (h + 1) / N_HEADS), F32) for h in heads], axis=0)
        bias.append(slope * dist)
        sinks.append(jnp.concatenate([jnp.full((CHUNK, 1), sink_ref[h], F32) for h in heads], axis=0))

    def scores(c, g):
        ks = k_all[c * CHUNK:c * CHUNK + nk, g * HEAD_DIM:(g + 1) * HEAD_DIM]
        qs = jnp.concatenate(
            [q[c * CHUNK:(c + 1) * CHUNK, (g * SWA_GROUP + hh) * HEAD_DIM:(g * SWA_GROUP + hh + 1) * HEAD_DIM]
             for hh in range(SWA_GROUP)], axis=0)
        return _mm_nt(qs, ks)

    def probs(s, c, g):
        s = s * (HEAD_DIM ** -0.5) - bias[g]
        if mask_history:
            s = jnp.where(scol >= (2 - (i * nc + c)) * CHUNK, s, -1e30)
        m = jnp.maximum(jnp.max(s, axis=-1, keepdims=True), sinks[g])
        p = _bf(jnp.exp(s - m))
        rsum = jnp.dot(p, ones_k, preferred_element_type=F32)
        m_wide = jnp.broadcast_to(m, rsum.shape)
        inv = 1.0 / (rsum + jnp.exp(jnp.broadcast_to(sinks[g], rsum.shape) - m_wide))
        return p, inv[:, 0:HEAD_DIM]

    ones_k = jnp.ones((nk, 128), BF16)
    s_all = [scores(c, g) for c, g in units]
    p_all = [probs(s, c, g) for s, (c, g) in zip(s_all, units)]
    o_all = [_mm(p, v_all[c * CHUNK:c * CHUNK + nk, g * HEAD_DIM:(g + 1) * HEAD_DIM]) * inv
             for (p, inv), (c, g) in zip(p_all, units)]
    for o, (c, g) in zip(o_all, units):
        for hh in range(SWA_GROUP):
            h = g * SWA_GROUP + hh
            y_ref[0, c * CHUNK:(c + 1) * CHUNK, h * HEAD_DIM:(h + 1) * HEAD_DIM] = _bf(
                o[hh * CHUNK:(hh + 1) * CHUNK])


def _swa(u, hist_k, hist_v, sinks, tq):
    b, t, _ = u.shape
    per = tq // CHUNK
    kvw = SWA_KV * HEAD_DIM
    mask_history = hist_k is None
    cur_k = pl.BlockSpec((1, tq, kvw), lambda i, j: (i, j, C_KS // kvw))
    cur_v = pl.BlockSpec((1, tq, kvw), lambda i, j: (i, j, C_VS // kvw))
    if mask_history:
        prev = lambda d, col: pl.BlockSpec(
            (1, CHUNK, kvw), lambda i, j: (i, jnp.maximum(j * per - d, 0), col))
        specs = [prev(2, C_KS // kvw), prev(1, C_KS // kvw), cur_k,
                 prev(2, C_VS // kvw), prev(1, C_VS // kvw), cur_v]
        args = (u, u, u, u, u, u)
    else:
        assert t == tq
        hk = hist_k.reshape(b, WINDOW, kvw)
        hv = hist_v.reshape(b, WINDOW, kvw)
        hist = lambda blk: pl.BlockSpec((1, CHUNK, kvw), lambda i, j: (i, blk, 0))
        specs = [hist(0), hist(1), cur_k, hist(0), hist(1), cur_v]
        args = (hk, hk, u, hv, hv, u)
    return pl.pallas_call(
        functools.partial(_swa_kernel, mask_history),
        grid=(b, t // tq),
        in_specs=[pl.BlockSpec((1, tq, BW), lambda i, j: (i, j, C_Q // BW))] + specs
        + [pl.BlockSpec(memory_space=pltpu.SMEM)],
        out_specs=pl.BlockSpec((1, tq, BW), lambda i, j: (i, j, 0)),
        out_shape=jax.ShapeDtypeStruct((b, t, BW), BF16),
        compiler_params=_cparams(("parallel", "parallel")),
        name="swa",
    )(u, *args, sinks)


def _ssd_kernel(z_ref, x_ref, b_ref, c_ref, dt_ref, hx_ref, hb_ref, hc_ref, s0_ref,
                cwx_ref, cwb_ref, cwc_ref, cbx_ref, cbb_ref, cbc_ref,
                dtb_ref, alog_ref, dsk_ref, nw_ref,
                y_ref, sfin_ref,
                px_sc, pb_sc, pc_sc, s_sc, xa_sc, ba_sc, ca_sc, acs_sc, gend_sc,
                acsx_sc, eout_sc, y_sc, xdt_sc, xde_sc):
    tt = x_ref.shape[1]
    nc = tt // CHUNK
    i = pl.program_id(1)

    @pl.when(i == 0)
    def _():
        px_sc[...] = hx_ref[0]
        pb_sc[...] = hb_ref[0]
        pc_sc[...] = hc_ref[0]
        s_sc[...] = s0_ref[0].reshape(N_HEADS * HEAD_DIM, SSM_STATE)

    rid8 = lax.broadcasted_iota(jnp.int32, (8, 1), 0)

    def conv_silu(x, prev8, w_ref, b_ref_):
        acc = None
        for wi in range(SSM_CONV):
            sh = SSM_CONV - 1 - wi
            if sh == 0:
                xs = x
            else:
                rolled = pltpu.roll(x, sh, 0)
                top = jnp.where(rid8 < sh, pltpu.roll(prev8, sh, 0), rolled[0:8])
                xs = jnp.concatenate([top, rolled[8:]], axis=0)
            term = xs * w_ref[wi:wi + 1, :]
            acc = (b_ref_[...] + term) if acc is None else acc + term
        return _silu(acc)

    xr = x_ref[0]
    br = b_ref[0]
    cr = c_ref[0]
    xa_sc[...] = conv_silu(xr, px_sc[...], cwx_ref, cbx_ref)
    ba_sc[...] = conv_silu(br, pb_sc[...], cwb_ref, cbb_ref)
    ca_sc[...] = conv_silu(cr, pc_sc[...], cwc_ref, cbc_ref)
    px_sc[...] = xr[tt - 8:tt, :]
    pb_sc[...] = br[tt - 8:tt, :]
    pc_sc[...] = cr[tt - 8:tt, :]
    hl = 128
    dt = _softplus(dt_ref[0][:, 0:hl] + dtb_ref[...])
    ad = -jnp.exp(alog_ref[...]) * dt

    tri_incl_bf = _tri(CHUNK).astype(BF16)
    sel16 = (lax.broadcasted_iota(jnp.int32, (N_HEADS, hl), 0)
             == lax.broadcasted_iota(jnp.int32, (N_HEADS, hl), 1)).astype(BF16)
    max_terms = 3
    srow = lax.broadcasted_iota(jnp.int32, (hl, BW), 0)
    spread = (((srow & (N_HEADS - 1)) == lax.broadcasted_iota(jnp.int32, (hl, BW), 1) // HEAD_DIM)
              & (srow < max_terms * N_HEADS)).astype(BF16)
    head_lane = lax.broadcasted_iota(jnp.int32, (1, hl), 1) < N_HEADS

    def per_channel(x, terms):
        rem = jnp.where(head_lane, x, 0.0)
        packed = None
        for t in range(terms):
            part = _bf(rem).astype(F32)
            if t < terms - 1:
                rem = rem - part
            placed = part if t == 0 else pltpu.roll(part, t * N_HEADS, 1)
            packed = placed if packed is None else packed + placed
        return jnp.dot(_bf(packed), spread, preferred_element_type=F32)

    acs = jnp.concatenate(
        [_mm_sel(tri_incl_bf, ad[c * CHUNK:(c + 1) * CHUNK, :]) for c in range(nc)], axis=0)
    a_last = jnp.concatenate(
        [jnp.broadcast_to(acs[(c + 1) * CHUNK - 1:(c + 1) * CHUNK, :], (CHUNK, hl)) for c in range(nc)], axis=0)
    acs_sc[...] = acs
    acsx_sc[...] = per_channel(acs, 3)
    eout_sc[...] = per_channel(jnp.exp(acs), 2)
    gend_sc[...] = jnp.exp(jnp.concatenate(
        [jnp.broadcast_to(acs[(c + 1) * CHUNK - 1:(c + 1) * CHUNK, :], (8, hl)) for c in range(nc)], axis=0))
    xdt = xa_sc[...] * per_channel(dt, 2)
    xdt_sc[...] = _bf(xdt)
    xde_sc[...] = _bf(xdt * per_channel(jnp.exp(a_last - acs), 2))

    lane = lax.broadcasted_iota(jnp.int32, (CHUNK, 128), 1)
    row = lax.broadcasted_iota(jnp.int32, (CHUNK, 128), 0)
    tri_incl2 = row >= (lane & (HEAD_DIM - 1))
    r128 = lax.broadcasted_iota(jnp.int32, (128, 128), 0)
    c128 = lax.broadcasted_iota(jnp.int32, (128, 128), 1)
    same_head = (r128 < HEAD_DIM) == (c128 < HEAD_DIM)
    hpg = N_HEADS // SSM_GROUPS
    rpg = hpg * HEAD_DIM

    def chunk_body(c, carry):
        rows = pl.ds(pl.multiple_of(c * CHUNK, CHUNK), CHUNK)
        grow = pl.ds(pl.multiple_of(c * 8, 8), 8)
        bc = _bf(ba_sc[rows, :])
        cc = _bf(ca_sc[rows, :])
        acs_t = _mm_sel_nt(sel16, acs_sc[rows, :])
        g_end = gend_sc[grow, :][0:1, :]
        gsl = [slice(gi * SSM_STATE, (gi + 1) * SSM_STATE) for gi in range(SSM_GROUPS)]
        scores = [_mm_nt(cc[:, gs], bc[:, gs]) for gs in gsl]
        scores2 = [jnp.concatenate([s, s], axis=1) for s in scores]
        y_off = [_mm_nt(cc[:, gsl[gi]], s_sc[gi * rpg:(gi + 1) * rpg, :]) for gi in range(SSM_GROUPS)]
        upd = [_mm_tn(xde_sc[rows, gi * rpg:(gi + 1) * rpg], bc[:, gsl[gi]]) for gi in range(SSM_GROUPS)]
        y_diag = []
        for hp in range(N_HEADS // 2):
            sl = slice(hp * 128, (hp + 1) * 128)
            seg = acsx_sc[rows, sl] - jnp.concatenate([acs_t[2 * hp:2 * hp + 1, :],
                                                       acs_t[2 * hp + 1:2 * hp + 2, :]], axis=1)
            m = scores2[(2 * hp) // hpg] * jnp.exp(jnp.where(tri_incl2, seg, -1e30))
            xd = xdt_sc[rows, sl]
            y_diag.append(_mm(m, jnp.where(same_head, jnp.concatenate([xd, xd], axis=0),
                                           jnp.zeros((), BF16))))
        for h in range(N_HEADS):
            gi, hl_ = divmod(h, hpg)
            hrows = slice(h * HEAD_DIM, (h + 1) * HEAD_DIM)
            s_sc[hrows, :] = (s_sc[hrows, :] * g_end[:, h:h + 1]
                              + upd[gi][hl_ * HEAD_DIM:(hl_ + 1) * HEAD_DIM, :])
        y_sc[rows, :] = (jnp.concatenate(y_diag, axis=1)
                         + jnp.concatenate(y_off, axis=1) * eout_sc[rows, :])
        return carry

    lax.fori_loop(0, nc, chunk_body, 0)

    y = (y_sc[...] + xa_sc[...] * dsk_ref[...]) * _silu(z_ref[0])
    gw = BW // SSM_GROUPS
    parts = []
    for gi in range(SSM_GROUPS):
        yg = y[:, gi * gw:(gi + 1) * gw]
        parts.append(yg * lax.rsqrt(jnp.mean(yg * yg, axis=-1, keepdims=True) + NORM_EPS))
    y_ref[0] = _bf(jnp.concatenate(parts, axis=1) * nw_ref[...])

    @pl.when(i == pl.num_programs(1) - 1)
    def _():
        sfin_ref[0] = s_sc[...].reshape(N_HEADS, HEAD_DIM, SSM_STATE)


def _pad_rows8(a):
    return jnp.pad(a, ((0, 0), (8 - a.shape[1], 0), (0, 0)))


def _pad_lanes(a, n):
    return jnp.pad(a.reshape(1, -1), ((0, 0), (0, n - a.shape[-1])))


def _ssd(u, conv_hist, s0, p, tt):
    b, t, _ = u.shape
    h8 = _pad_rows8(conv_hist)
    hx, hb, hc = h8[:, :, :BW], h8[:, :, BW:BW + SSM_BC], h8[:, :, BW + SSM_BC:]
    cw, cb = p["ssm_conv_w"], p["ssm_conv_b"].reshape(1, -1)
    full = lambda shape: pl.BlockSpec(shape, lambda i, j: (0,) * len(shape))
    col = lambda w, c0: pl.BlockSpec((1, tt, w), lambda i, j: (i, j, c0 // w))
    hist = lambda w: pl.BlockSpec((1, 8, w), lambda i, j: (i, 0, 0))
    return pl.pallas_call(
        _ssd_kernel,
        grid=(b, t // tt),
        in_specs=[
            col(BW, C_Z), col(BW, C_X), col(SSM_BC, C_B), col(SSM_BC, C_C), col(SSM_BC, C_DT),
            hist(BW), hist(SSM_BC), hist(SSM_BC),
            pl.BlockSpec((1, N_HEADS, HEAD_DIM, SSM_STATE), lambda i, j: (i, 0, 0, 0)),
            full((SSM_CONV, BW)), full((SSM_CONV, SSM_BC)), full((SSM_CONV, SSM_BC)),
            full((1, BW)), full((1, SSM_BC)), full((1, SSM_BC)),
            full((1, 128)), full((1, 128)), full((1, BW)), full((1, BW)),
        ],
        out_specs=[
            pl.BlockSpec((1, tt, BW), lambda i, j: (i, j, 0)),
            pl.BlockSpec((1, N_HEADS, HEAD_DIM, SSM_STATE), lambda i, j: (i, 0, 0, 0)),
        ],
        out_shape=[
            jax.ShapeDtypeStruct((b, t, BW), BF16),
            jax.ShapeDtypeStruct((b, N_HEADS, HEAD_DIM, SSM_STATE), F32),
        ],
        scratch_shapes=[
            pltpu.VMEM((8, BW), F32), pltpu.VMEM((8, SSM_BC), F32), pltpu.VMEM((8, SSM_BC), F32),
            pltpu.VMEM((N_HEADS * HEAD_DIM, SSM_STATE), F32),
            pltpu.VMEM((tt, BW), F32), pltpu.VMEM((tt, SSM_BC), F32), pltpu.VMEM((tt, SSM_BC), F32),
            pltpu.VMEM((tt, 128), F32), pltpu.VMEM((8 * (tt // CHUNK), 128), F32),
            pltpu.VMEM((tt, BW), F32), pltpu.VMEM((tt, BW), F32), pltpu.VMEM((tt, BW), F32),
            pltpu.VMEM((tt, BW), BF16), pltpu.VMEM((tt, BW), BF16),
        ],
        compiler_params=_cparams(("parallel", "arbitrary")),
        name="ssd",
    )(u, u, u, u, u, hx, hb, hc, s0,
      cw[:, :BW], cw[:, BW:BW + SSM_BC], cw[:, BW + SSM_BC:],
      cb[:, :BW], cb[:, BW:BW + SSM_BC], cb[:, BW + SSM_BC:],
      _pad_lanes(p["ssm_dt_bias"], 128), _pad_lanes(p["ssm_a_log"], 128),
      jnp.repeat(p["ssm_d"], HEAD_DIM).reshape(1, BW), p["ssm_norm"].reshape(1, BW))


_RELAYOUT_BLOCK = 256


def _w_in_block_order():
    blk = _RELAYOUT_BLOCK
    segs = [(0, 3 * BW), (_N_SSM, BW), (_N_POOL, BW), (_N_SWA, BW), (_N_SSM + BW, BW), (3 * BW, RW_LORA),
            (_N_SWA + BW, 2 * SWA_KV * HEAD_DIM), (_N_SSM + 2 * BW, 2 * SSM_BC + blk)]
    order = [start // blk + i for start, width in segs for i in range(width // blk)]
    assert len(order) == U_COLS // blk and all(start % blk == 0 for start, _ in segs)
    return order


def _relayout_kernel(order_ref, w_ref, o_ref):
    src = order_ref[pl.program_id(1)]
    last = IN_COLS // _RELAYOUT_BLOCK
    valid = jnp.where(src == last, IN_COLS - last * _RELAYOUT_BLOCK, _RELAYOUT_BLOCK)
    out_col = lax.broadcasted_iota(jnp.int32, (_RELAYOUT_BLOCK, 1), 0)
    o_ref[...] = _bf(jnp.where(out_col < valid, w_ref[...], 0.0))


def _prep_w_in(w_in):
    blk = _RELAYOUT_BLOCK
    order = jnp.asarray(_w_in_block_order(), jnp.int32)
    return pl.pallas_call(
        _relayout_kernel,
        grid_spec=pltpu.PrefetchScalarGridSpec(
            num_scalar_prefetch=1,
            grid=(DEPTH, U_COLS // blk),
            in_specs=[pl.BlockSpec((None, blk, D_MODEL), lambda l, j, order: (l, order[j], 0))],
            out_specs=pl.BlockSpec((None, blk, D_MODEL), lambda l, j, order: (l, j, 0)),
        ),
        out_shape=jax.ShapeDtypeStruct((DEPTH, U_COLS, D_MODEL), BF16),
        compiler_params=_cparams(("parallel", "parallel")),
        name="w_in_relayout",
    )(order, jnp.swapaxes(w_in, 1, 2))


def _run_group(x, mods, st, p, final_norm, wts, pos0, bb):
    b, t, _ = x.shape
    kvw = SWA_KV * HEAD_DIM
    tt_wide, tt_merge = min(t, ROWS_WIDE // bb), min(t, ROWS_MERGE // bb)
    tt_mixer, tt_rwkv, tt_swa = min(t, ROWS_MIXER), min(t, ROWS_RWKV), min(t, ROWS_SWA)
    outs = {k: [] for k in ("rwkv", "shift", "pool", "k", "v", "ssm", "conv")}
    for l in range(DEPTH):
        pl_ = {k: v[l] for k, v in p.items()}
        mod = mods[l]
        u, h = _inproj(x, pl_["norm_mix"], mod, wts["w_in"], l, bb, tt_wide)
        if st is None:
            shift_hist = jnp.zeros((b, 1, RW_IN), F32)
            s_rwkv = jnp.zeros((b, N_HEADS, HEAD_DIM, HEAD_DIM), F32)
            pool_hist = jnp.zeros((b, POOL_HIST, BW), F32)
            hk = hv = None
            s_ssm = jnp.zeros((b, N_HEADS, HEAD_DIM, SSM_STATE), F32)
            conv_hist = jnp.zeros((b, SSM_CONV - 1, SSM_CONV_DIM), F32)
        else:
            shift_hist, s_rwkv, pool_hist = st["shift"][l], st["rwkv"][l], st["pool"][l]
            hk, hv, s_ssm, conv_hist = st["k"][l], st["v"][l], st["ssm"][l], st["conv"][l]
        y_a, n_rwkv = _rwkv(u, shift_hist, s_rwkv, pl_, tt_rwkv)
        hist16 = jnp.pad(pool_hist, ((0, 0), (1, 0), (0, 0)))
        y_b = _pool(u, hist16, pos0, wts["pool_w"][l], pl_["pool_scale"], tt_mixer)
        y_c = _swa(u, hk, hv, pl_["swa_sinks"], tt_swa)
        y_d, n_ssm = _ssd(u, conv_hist, s_ssm, pl_, tt_mixer)
        merged = _merge(h, (y_a, y_b, y_c, y_d), wts["w_gate"], wts["w_branch"], l, bb, tt_merge)
        x, h2 = _outproj(merged, wts["w_out"], l, x, mod, pl_["norm_ffn"], bb, tt_wide)
        x = _ffn(x, h2, mod, wts["w_up"], wts["w_down"], l, final_norm, l == DEPTH - 1, bb, tt_wide)

        outs["rwkv"].append(n_rwkv)
        outs["shift"].append(jnp.concatenate(
            [u[:, t - 1:, C_RKV:C_RKV + 3 * BW], u[:, t - 1:, C_LORA:C_LORA + RW_LORA]], axis=-1))
        outs["pool"].append(u[:, t - POOL_HIST:, C_POOL:C_POOL + BW])
        k_new = u[:, :, C_KS:C_KS + kvw]
        v_new = u[:, :, C_VS:C_VS + kvw]
        if hk is not None:
            k_new = jnp.concatenate([hk.reshape(b, WINDOW, kvw), k_new], axis=1)
            v_new = jnp.concatenate([hv.reshape(b, WINDOW, kvw), v_new], axis=1)
        outs["k"].append(k_new[:, -WINDOW:].reshape(b, WINDOW, SWA_KV, HEAD_DIM))
        outs["v"].append(v_new[:, -WINDOW:].reshape(b, WINDOW, SWA_KV, HEAD_DIM))
        outs["ssm"].append(n_ssm)
        outs["conv"].append(jnp.concatenate(
            [u[:, t - 3:, C_X:C_X + BW], u[:, t - 3:, C_B:C_B + 2 * SSM_BC]], axis=-1))
    order = ("rwkv", "shift", "pool", "k", "v", "ssm", "conv")
    return x, tuple(jnp.stack(outs[k]) for k in order)


def kernel(x_prompt, x_sample, state_rwkv, state_rwkv_shift, state_pool, cache_swa_k, cache_swa_v,
           state_ssm, state_ssm_conv, c_prompt, c_sample, ada_w, ada_b, norm_mix, norm_ffn, w_in,
           rw_mu, rw_w0, rw_w2, rw_a0, rw_a2, rw_g2, rw_kk, rw_ka, rw_rk, rw_ln_w, rw_ln_b, pool_w,
           pool_scale, swa_sinks, ssm_conv_w, ssm_conv_b, ssm_dt_bias, ssm_a_log, ssm_d, ssm_norm,
           w_gate, w_branch, w_out, w_up, w_down, final_norm):
    bp, bs = x_prompt.shape[0], x_sample.shape[0]
    p = dict(norm_mix=norm_mix, norm_ffn=norm_ffn, rw_mu=rw_mu, rw_w0=rw_w0, rw_w2=rw_w2, rw_a0=rw_a0,
             rw_a2=rw_a2, rw_g2=rw_g2, rw_kk=rw_kk, rw_ka=rw_ka, rw_rk=rw_rk, rw_ln_w=rw_ln_w,
             rw_ln_b=rw_ln_b, pool_scale=pool_scale, swa_sinks=swa_sinks, ssm_conv_w=ssm_conv_w,
             ssm_conv_b=ssm_conv_b, ssm_dt_bias=ssm_dt_bias, ssm_a_log=ssm_a_log, ssm_d=ssm_d,
             ssm_norm=ssm_norm)
    wts = dict(w_in=_prep_w_in(w_in), pool_w=pool_w.astype(BF16), w_gate=w_gate.astype(BF16),
               w_branch=w_branch.astype(BF16), w_out=w_out.astype(BF16), w_up=w_up.astype(BF16),
               w_down=w_down.astype(BF16))

    nb = bp + bs
    nb_pad = -(-nb // 8) * 8
    c_all = jnp.pad(jnp.concatenate([c_prompt, c_sample], axis=0), ((0, nb_pad - nb), (0, 0)))
    mod_all = _adaln(c_all, ada_w, ada_b)
    mods_p = mod_all[:, :bp].reshape(DEPTH, bp, 6, 1, D_MODEL)
    mods_s = mod_all[:, bp:nb].reshape(DEPTH, bs, 6, 1, D_MODEL)

    y_prompt, st_p = _run_group(x_prompt, mods_p, None, p, final_norm, wts, 0, 1)
    st_s = dict(rwkv=state_rwkv, shift=state_rwkv_shift, pool=state_pool, k=cache_swa_k,
                v=cache_swa_v, ssm=state_ssm, conv=state_ssm_conv)
    y_sample, st_o = _run_group(x_sample, mods_s, st_s, p, final_norm, wts, PAST_LEN, bs)
    return (y_prompt, y_sample) + st_p + st_o
```

```python
import functools
import math

import jax
import jax.numpy as jnp
from jax import lax
from jax.experimental import pallas as pl
from jax.experimental.pallas import tpu as pltpu

F32 = jnp.float32
BF16 = jnp.bfloat16

D_MODEL = 2048
DEPTH = 4
PAST_LEN = 1024
CHUNK = 64
HEAD_DIM = 64
BW = D_MODEL // 2
D_FF = 4 * D_MODEL
NORM_EPS = 1e-6
N_HEADS = BW // HEAD_DIM
RW_LORA = 256
RW_IN = 3 * BW + RW_LORA
RW_GN_EPS = HEAD_DIM * 1e-5
POOL_WINDOWS = (2, 4, 8, 16)
POOL_GW = BW // 4
POOL_HIST = 15
SWA_KV = 4
SWA_GROUP = N_HEADS // SWA_KV
WINDOW = 128
SSM_GROUPS = 2
SSM_STATE = 128
SSM_CONV = 4
SSM_BC = SSM_GROUPS * SSM_STATE
SSM_CONV_DIM = BW + 2 * SSM_BC

C_RKV = 0
C_Z = 3072
C_POOL = 4096
C_Q = 5120
C_X = 6144
C_LORA = 7168
C_KS = 7424
C_VS = 7680
C_B = 7936
C_C = 8192
C_DT = 8448
U_COLS = 8704

_N_POOL = RW_IN
_N_SWA = _N_POOL + BW
_N_SSM = _N_SWA + (N_HEADS + 2 * SWA_KV) * HEAD_DIM
IN_COLS = _N_SSM + BW + SSM_CONV_DIM + N_HEADS

VMEM_LIMIT = 56 * 1024 * 1024

ROWS_MERGE = 1024
ROWS_WIDE = 512
ROWS_MIXER = 512
ROWS_RWKV = 256
ROWS_SWA = 256
TN_ADALN = 1024
TN_INPROJ = U_COLS // 4
TN_MERGE = 256
TF_FFN = 1024


def _cparams(sem):
    return pltpu.CompilerParams(dimension_semantics=sem, vmem_limit_bytes=VMEM_LIMIT)


def _bf(x):
    return x.astype(BF16)


def _mm(a, b):
    return jnp.dot(_bf(a), _bf(b), preferred_element_type=F32)


def _mm_nt(a, b):
    return lax.dot_general(_bf(a), _bf(b), (((1,), (1,)), ((), ())), preferred_element_type=F32)


def _mm_tn(a, b):
    return lax.dot_general(_bf(a), _bf(b), (((0,), (0,)), ((), ())), preferred_element_type=F32)


def _split(x, terms):
    parts = []
    for _ in range(terms - 1):
        hi = _bf(x)
        parts.append(hi)
        x = x - hi.astype(F32)
    parts.append(_bf(x))
    return parts


def _mm_sel(c, x, terms=3):
    acc = None
    for part in _split(x, terms):
        d = jnp.dot(c, part, preferred_element_type=F32)
        acc = d if acc is None else acc + d
    return acc


def _mm_sel_nt(c, x, terms=3):
    acc = None
    for part in _split(x, terms):
        d = lax.dot_general(c, part, (((1,), (1,)), ((), ())), preferred_element_type=F32)
        acc = d if acc is None else acc + d
    return acc


def _softplus(x):
    return jnp.maximum(x, 0.0) + jnp.log1p(jnp.exp(-jnp.abs(x)))


def _sigmoid(x):
    return 0.5 * jnp.tanh(0.5 * x) + 0.5


def _silu(x):
    return x * _sigmoid(x)


def _tri(n, strict=False):
    r = lax.broadcasted_iota(jnp.int32, (n, n), 0)
    c = lax.broadcasted_iota(jnp.int32, (n, n), 1)
    return (r > c) if strict else (r >= c)


def _adaln_kernel(c_ref, w_ref, b_ref, o_ref):
    s = _silu(c_ref[...])
    o_ref[0] = _mm(s, w_ref[0]) + b_ref[0]


def _adaln(c_all, ada_w, ada_b):
    nb = c_all.shape[0]
    tn = TN_ADALN
    n_out = ada_w.shape[2]
    return pl.pallas_call(
        _adaln_kernel,
        grid=(DEPTH, n_out // tn),
        in_specs=[
            pl.BlockSpec((nb, D_MODEL), lambda l, n: (0, 0)),
            pl.BlockSpec((1, D_MODEL, tn), lambda l, n: (l, 0, n)),
            pl.BlockSpec((1, 1, tn), lambda l, n: (l, 0, n)),
        ],
        out_specs=pl.BlockSpec((1, nb, tn), lambda l, n: (l, 0, n)),
        out_shape=jax.ShapeDtypeStruct((DEPTH, nb, n_out), F32),
        compiler_params=_cparams(("parallel", "parallel")),
        name="adaln",
    )(c_all, ada_w, ada_b.reshape(DEPTH, 1, n_out))


def _norm_mod(x, g, sc, sh):
    y = x * lax.rsqrt(jnp.mean(x * x, axis=-1, keepdims=True) + NORM_EPS)
    return (y * g) * (1.0 + sc) + sh


def _inproj_kernel(x_ref, g_ref, sc_ref, sh_ref, w_ref, u_ref, h_ref):
    bb, tt, _ = x_ref.shape

    @pl.when(pl.program_id(2) == 0)
    def _():
        h = _norm_mod(x_ref[...], g_ref[...], sc_ref[:, 0], sh_ref[:, 0])
        h_ref[...] = _bf(h)

    h = h_ref[...].reshape(bb * tt, D_MODEL)
    u = lax.dot_general(h, w_ref[...], (((1,), (1,)), ((), ())), preferred_element_type=F32)
    u_ref[...] = _bf(u).reshape(u_ref.shape)


def _inproj(x, g, mod, w, l, bb, tt):
    b, t, _ = x.shape
    tn = TN_INPROJ
    return pl.pallas_call(
        _inproj_kernel,
        grid=(b // bb, t // tt, U_COLS // tn),
        in_specs=[
            pl.BlockSpec((bb, tt, D_MODEL), lambda i, j, n: (i, j, 0)),
            pl.BlockSpec((1, 1, D_MODEL), lambda i, j, n: (0, 0, 0)),
            pl.BlockSpec((bb, 1, 1, D_MODEL), lambda i, j, n: (i, 1, 0, 0)),
            pl.BlockSpec((bb, 1, 1, D_MODEL), lambda i, j, n: (i, 0, 0, 0)),
            pl.BlockSpec((None, tn, D_MODEL), lambda i, j, n: (l, n, 0)),
        ],
        out_specs=[
            pl.BlockSpec((bb, tt, tn), lambda i, j, n: (i, j, n)),
            pl.BlockSpec((bb, tt, D_MODEL), lambda i, j, n: (i, j, 0)),
        ],
        out_shape=[
            jax.ShapeDtypeStruct((b, t, U_COLS), BF16),
            jax.ShapeDtypeStruct((b, t, D_MODEL), BF16),
        ],
        compiler_params=_cparams(("parallel", "parallel", "arbitrary")),
        name="inproj",
    )(x, g.reshape(1, 1, D_MODEL), mod, mod, w)


def _merge_kernel(h_ref, ya_ref, yb_ref, yc_ref, yd_ref, wg_ref, wb_ref, o_ref):
    bb, tt, _ = h_ref.shape
    m = bb * tt
    h = h_ref[...].reshape(m, D_MODEL)
    acc = None
    for i, y_ref in enumerate((ya_ref, yb_ref, yc_ref, yd_ref)):
        gate = _sigmoid(jnp.dot(h, wg_ref[i], preferred_element_type=F32))
        br = jnp.dot(y_ref[...].reshape(m, BW), wb_ref[i], preferred_element_type=F32)
        acc = gate * br if acc is None else acc + gate * br
    o_ref[...] = _bf(acc).reshape(o_ref.shape)


def _merge(h, ys, wg, wb, l, bb, tt):
    b, t, _ = h.shape
    tn = TN_MERGE
    yspec = pl.BlockSpec((bb, tt, BW), lambda i, j, n: (i, j, 0))
    return pl.pallas_call(
        _merge_kernel,
        grid=(b // bb, t // tt, D_MODEL // tn),
        in_specs=[
            pl.BlockSpec((bb, tt, D_MODEL), lambda i, j, n: (i, j, 0)),
            yspec, yspec, yspec, yspec,
            pl.BlockSpec((None, 4, D_MODEL, tn), lambda i, j, n: (l, 0, 0, n)),
            pl.BlockSpec((None, 4, BW, tn), lambda i, j, n: (l, 0, 0, n)),
        ],
        out_specs=pl.BlockSpec((bb, tt, tn), lambda i, j, n: (i, j, n)),
        out_shape=jax.ShapeDtypeStruct((b, t, D_MODEL), BF16),
        compiler_params=_cparams(("parallel", "parallel", "arbitrary")),
        name="merge",
    )(h, *ys, wg, wb)


def _outproj_kernel(m_ref, w_ref, x_ref, g1_ref, gn_ref, sc_ref, sh_ref, o_ref, h_ref):
    bb, tt, _ = m_ref.shape
    y = jnp.dot(m_ref[...].reshape(bb * tt, D_MODEL), w_ref[...], preferred_element_type=F32)
    x1 = x_ref[...] + g1_ref[:, 0] * y.reshape(o_ref.shape)
    o_ref[...] = x1
    h_ref[...] = _bf(_norm_mod(x1, gn_ref[...], sc_ref[:, 0], sh_ref[:, 0]))


def _outproj(merged, w, l, x, mod, g_ffn, bb, tt):
    b, t, _ = x.shape
    row = pl.BlockSpec((bb, tt, D_MODEL), lambda i, j: (i, j, 0))
    modk = lambda k: pl.BlockSpec((bb, 1, 1, D_MODEL), lambda i, j: (i, k, 0, 0))
    return pl.pallas_call(
        _outproj_kernel,
        grid=(b // bb, t // tt),
        in_specs=[
            row,
            pl.BlockSpec((None, D_MODEL, D_MODEL), lambda i, j: (l, 0, 0)),
            row,
            modk(2),
            pl.BlockSpec((1, 1, D_MODEL), lambda i, j: (0, 0, 0)),
            modk(4), modk(3),
        ],
        out_specs=[row, row],
        out_shape=[jax.ShapeDtypeStruct((b, t, D_MODEL), F32), jax.ShapeDtypeStruct((b, t, D_MODEL), BF16)],
        compiler_params=_cparams(("parallel", "parallel")),
        name="outproj",
    )(merged, w, x, mod, g_ffn.reshape(1, 1, D_MODEL), mod, mod)


def _ffn_kernel(final, x_ref, h_ref, gate_ref, wu_ref, wd_ref, fg_ref, o_ref):
    bb, tt, _ = x_ref.shape
    f = pl.program_id(2)

    @pl.when(f == 0)
    def _():
        o_ref[...] = jnp.zeros(o_ref.shape, F32)

    a = jnp.dot(h_ref[...].reshape(bb * tt, D_MODEL), wu_ref[...], preferred_element_type=F32)
    a = jnp.square(jnp.maximum(a, 0.0))
    o_ref[...] += jnp.dot(_bf(a), wd_ref[...], preferred_element_type=F32).reshape(o_ref.shape)

    @pl.when(f == pl.num_programs(2) - 1)
    def _():
        y = x_ref[...] + gate_ref[:, 0] * o_ref[...]
        if final:
            y = (y * lax.rsqrt(jnp.mean(y * y, axis=-1, keepdims=True) + NORM_EPS)) * fg_ref[...]
        o_ref[...] = y


def _ffn(x, h2, mod, wu, wd, l, final_g, final, bb, tt):
    b, t, _ = x.shape
    tf = TF_FFN
    row = pl.BlockSpec((bb, tt, D_MODEL), lambda i, j, f: (i, j, 0))
    return pl.pallas_call(
        functools.partial(_ffn_kernel, final),
        grid=(b // bb, t // tt, D_FF // tf),
        in_specs=[
            row, row,
            pl.BlockSpec((bb, 1, 1, D_MODEL), lambda i, j, f: (i, 5, 0, 0)),
            pl.BlockSpec((None, D_MODEL, tf), lambda i, j, f: (l, 0, f)),
            pl.BlockSpec((None, tf, D_MODEL), lambda i, j, f: (l, f, 0)),
            pl.BlockSpec((1, 1, D_MODEL), lambda i, j, f: (0, 0, 0)),
        ],
        out_specs=row,
        out_shape=jax.ShapeDtypeStruct((b, t, D_MODEL), F32),
        compiler_params=_cparams(("parallel", "parallel", "arbitrary")),
        name="ffn",
    )(x, h2, mod, wu, wd, final_g.reshape(1, 1, D_MODEL))


NPAIR = N_HEADS // 2


def _rwkv_kernel(rkv_ref, lora_ref, sh_rkv_ref, sh_lora_ref, s0_ref, mu_rkv_ref, mu_lora_ref,
                 w0_ref, w2_ref, a0_ref, a2_ref, g2_ref, kkp_ref, ka_ref, rk_ref, lnw_ref, lnb_ref,
                 y_ref, sfin_ref,
                 prev_rkv, prev_lora, s_sc, g_sc, gend_sc, bonus_sc, yo_sc,
                 rt_sc, pt_sc, qt_sc, kt_sc, qe_sc, ke_sc, v_sc, lhs_sc, z_sc, arqk_sc):
    tt = rkv_ref.shape[1]
    nc = tt // CHUNK
    i = pl.program_id(1)

    @pl.when(i == 0)
    def _():
        prev_rkv[...] = sh_rkv_ref[0]
        prev_lora[...] = sh_lora_ref[0]
        for hp in range(NPAIR):
            s_sc[hp] = jnp.concatenate([s0_ref[0, 2 * hp], s0_ref[0, 2 * hp + 1]], axis=1)

    rid8 = lax.broadcasted_iota(jnp.int32, (8, 1), 0)

    def tshift(x, prev_row, mu):
        rolled = pltpu.roll(x, 1, 0)
        prev = jnp.concatenate([jnp.where(rid8 == 0, prev_row, rolled[0:8]), rolled[8:]], axis=0)
        return x + (prev - x) * mu

    u = rkv_ref[0].astype(F32)
    ul = lora_ref[0].astype(F32)
    xs = tshift(u, prev_rkv[...], mu_rkv_ref[...])
    xl = tshift(ul, prev_lora[...], mu_lora_ref[...])
    prev_rkv[...] = u[tt - 1:tt, :]
    prev_lora[...] = ul[tt - 1:tt, :]

    r = xs[:, 0:BW]
    k = xs[:, BW:2 * BW]
    v = xs[:, 2 * BW:3 * BW]
    wd = xl[:, 0:64]
    ad = xl[:, 64:128]
    gd = xl[:, 128:256]
    lw = -math.exp(-0.5) * _sigmoid(w0_ref[...] + _mm(jnp.tanh(wd), w2_ref[...]))
    a = _sigmoid(a0_ref[...] + _mm(ad, a2_ref[...]))
    g_sc[...] = _mm(_sigmoid(gd), g2_ref[...])
    kkr = k * kkp_ref[...]
    kh = k * (1.0 + (a - 1.0) * ka_ref[...])

    lane = lax.broadcasted_iota(jnp.int32, (CHUNK, 128), 1)
    row = lax.broadcasted_iota(jnp.int32, (CHUNK, 128), 0)
    col_in_head = lane & (HEAD_DIM - 1)
    tri_strict2 = row > col_in_head
    tri_incl2 = row >= col_in_head
    eye2 = (row == col_in_head).astype(F32)
    first_head = lane < HEAD_DIM
    r128 = lax.broadcasted_iota(jnp.int32, (128, 128), 0)
    c128 = lax.broadcasted_iota(jnp.int32, (128, 128), 1)
    same_head = (r128 < HEAD_DIM) == (c128 < HEAD_DIM)
    ones_bd = same_head.astype(BF16)
    tri_incl_bf = _tri(CHUNK).astype(BF16)

    def head_sum(x, terms):
        acc = None
        for part in _split(x, terms):
            d = jnp.dot(part, ones_bd, preferred_element_type=F32)
            acc = d if acc is None else acc + d
        return acc

    cl = jnp.concatenate(
        [_mm_sel(tri_incl_bf, lw[c * CHUNK:(c + 1) * CHUNK, :], terms=2) for c in range(nc)], axis=0)
    cl_last = jnp.concatenate(
        [jnp.broadcast_to(cl[(c + 1) * CHUNK - 1:(c + 1) * CHUNK, :], (CHUNK, BW)) for c in range(nc)], axis=0)
    e_in = jnp.exp(cl)
    e_ex = jnp.exp(cl - lw)
    e_inv = jnp.exp(-cl)
    e_end = jnp.exp(cl_last - cl)
    gend_sc[...] = jnp.exp(jnp.concatenate(
        [jnp.broadcast_to(cl[(c + 1) * CHUNK - 1:(c + 1) * CHUNK, :], (8, BW)) for c in range(nc)], axis=0))
    for hp in range(NPAIR):
        sl = slice(hp * 128, (hp + 1) * 128)
        kk = kkr[:, sl]
        kk = kk * lax.rsqrt(jnp.maximum(head_sum(kk * kk, 2), 1e-24))
        q = kk * a[:, sl]
        rt_sc[hp] = r[:, sl] * e_in[:, sl]
        pt_sc[hp] = -kk * e_ex[:, sl]
        qt_sc[hp] = q * e_inv[:, sl]
        kt_sc[hp] = kh[:, sl] * e_inv[:, sl]
        qe_sc[hp] = q * e_end[:, sl]
        ke_sc[hp] = kh[:, sl] * e_end[:, sl]
        v_sc[hp] = v[:, sl]
        bonus_sc[:, sl] = head_sum(r[:, sl] * kh[:, sl] * rk_ref[:, sl], 1) * v[:, sl]

    def bd(x):
        return jnp.where(same_head, jnp.concatenate([x, x], axis=0), jnp.zeros((), x.dtype))

    def parts(x):
        hi = _bf(x)
        return hi, _bf(x - hi.astype(F32))

    def bd2(p):
        return bd(p[0]), bd(p[1])

    nn, nt, tn = ((1,), (0,)), ((1,), (1,)), ((0,), (0,))

    def dot3_shared(a_list, b, dims):
        d = lambda x, y: lax.dot_general(x, y, (dims, ((), ())), preferred_element_type=F32)
        ax = 1 if dims == tn else 0
        m = a_list[0][0].shape[ax]
        big = d(jnp.concatenate([t for a in a_list for t in a], axis=ax), b[0])
        small = d(jnp.concatenate([a[0] for a in a_list], axis=ax), b[1])
        return [big[2 * i * m:(2 * i + 1) * m] + big[(2 * i + 1) * m:(2 * i + 2) * m] + small[i * m:(i + 1) * m]
                for i in range(len(a_list))]

    def dot3(a, b, dims):
        return dot3_shared([a], b, dims)[0]

    pairs = range(NPAIR)

    cpi = 4 if nc % 4 == 0 else 1

    def intra_body(n, carry):
        units = [(n * cpi + j, hp) for j in range(cpi) for hp in pairs]
        rows = [pl.ds(pl.multiple_of(c * CHUNK, CHUNK), CHUNK) for c, _ in units]
        un = range(len(units))
        pt = [pt_sc[units[k][1], rows[k], :] for k in un]
        rt = [rt_sc[units[k][1], rows[k], :] for k in un]
        lhs = [parts(jnp.concatenate([pt[k], rt[k]], axis=0)) for k in un]
        gq = [_mm_nt(lhs[k][0], bd(_bf(qt_sc[units[k][1], rows[k], :]))) for k in un]
        gk = [_mm_nt(lhs[k][0], bd(_bf(kt_sc[units[k][1], rows[k], :]))) for k in un]
        l_pq = [jnp.where(tri_strict2, gq[k][0:CHUNK], 0.0) for k in un]
        lv = [dot3(parts(jnp.where(tri_strict2, gk[k][0:CHUNK], 0.0)),
                   bd2(parts(v_sc[units[k][1], rows[k], :])), nn) for k in un]
        tinv = [eye2 + l_pq[k] for k in un]
        xpp = [parts(l_pq[k]) for k in un]
        xpp = [parts(dot3(xpp[k], bd2(xpp[k]), nn)) for k in un]
        for it in range(5):
            tp = [parts(tinv[k]) for k in un]
            if it < 4:
                res = [dot3_shared([xpp[k], tp[k]], bd2(xpp[k]), nn) for k in un]
                xpp = [parts(res[k][0]) for k in un]
                tinv = [tinv[k] + res[k][1] for k in un]
            else:
                tinv = [tinv[k] + dot3(tp[k], bd2(xpp[k]), nn) for k in un]
        tp = [parts(tinv[k]) for k in un]
        wmat = [dot3(tp[k], bd2((lhs[k][0][0:CHUNK], lhs[k][1][0:CHUNK])), nn) for k in un]
        for k, (c, hp) in enumerate(units):
            z_sc[c, hp] = dot3(tp[k], bd2(parts(lv[k])), nn)
            for t, part in enumerate(parts(jnp.concatenate([wmat[k], rt[k]], axis=0))):
                lhs_sc[t, c, hp] = part
            arqk_sc[c, hp] = jnp.concatenate(
                [_bf(jnp.where(tri_incl2, gq[k][CHUNK:2 * CHUNK], 0.0)),
                 _bf(jnp.where(tri_incl2, gk[k][CHUNK:2 * CHUNK], 0.0))], axis=1)
        return carry

    def state_body(c, carry):
        rows = pl.ds(pl.multiple_of(c * CHUNK, CHUNK), CHUNK)
        grow = pl.ds(pl.multiple_of(c * 8, 8), 8)
        vv = [parts(v_sc[hp, rows, :]) for hp in pairs]
        s0 = [s_sc[hp] for hp in pairs]
        ps = [dot3((lhs_sc[0, c, hp], lhs_sc[1, c, hp]), bd2(parts(s0[hp])), nt)
              for hp in pairs]
        uu = [parts(ps[hp][0:CHUNK] + z_sc[c, hp]) for hp in pairs]
        for hp in pairs:
            sl = slice(hp * 128, (hp + 1) * 128)
            yo_sc[rows, sl] = ps[hp][CHUNK:2 * CHUNK] + jnp.dot(
                arqk_sc[c, hp], jnp.concatenate([bd(uu[hp][0]), bd(vv[hp][0])], axis=0),
                preferred_element_type=F32)
        for hp in pairs:
            sl = slice(hp * 128, (hp + 1) * 128)
            uv = tuple(jnp.concatenate([uu[hp][t], vv[hp][t]], axis=0) for t in range(2))
            qk = parts(jnp.concatenate([qe_sc[hp, rows, :], ke_sc[hp, rows, :]], axis=0))
            full = dot3(uv, qk, tn)
            g_end = gend_sc[grow, sl][0:1, :]
            s_sc[hp] = s0[hp] * g_end + jnp.where(
                first_head, full[0:HEAD_DIM], full[HEAD_DIM:2 * HEAD_DIM])
        return carry

    lax.fori_loop(0, nc // cpi, intra_body, 0)
    lax.fori_loop(0, nc, state_body, 0)

    for hp in range(NPAIR):
        sl = slice(hp * 128, (hp + 1) * 128)
        yh = yo_sc[:, sl]
        mean = head_sum(yh, 1) * (1.0 / HEAD_DIM)
        d = yh - mean
        var = head_sum(d * d, 1) * (1.0 / HEAD_DIM)
        yn = d * lax.rsqrt(var + RW_GN_EPS) * lnw_ref[:, sl] + lnb_ref[:, sl]
        y_ref[0, :, sl] = _bf((yn + bonus_sc[:, sl]) * g_sc[:, sl])

    @pl.when(i == pl.num_programs(1) - 1)
    def _():
        for hp in range(NPAIR):
            sp = s_sc[hp]
            sfin_ref[0, 2 * hp] = sp[:, 0:HEAD_DIM]
            sfin_ref[0, 2 * hp + 1] = sp[:, HEAD_DIM:2 * HEAD_DIM]


def _rwkv(u, shift_hist, s0, p, tt):
    b, t, _ = u.shape
    row = lambda a: a.reshape(1, -1)
    sh = shift_hist.reshape(b, 1, RW_IN)
    sh_rkv = sh[:, :, :3 * BW]
    sh_lora = sh[:, :, 3 * BW:]
    mu = p["rw_mu"]
    full = lambda shape: pl.BlockSpec(shape, lambda i, j: (0,) * len(shape))
    scr = lambda: pltpu.VMEM((NPAIR, tt, 128), F32)
    wide = lambda rows: pltpu.VMEM((rows, BW), F32)
    return pl.pallas_call(
        _rwkv_kernel,
        grid=(b, t // tt),
        in_specs=[
            pl.BlockSpec((1, tt, 3 * BW), lambda i, j: (i, j, C_RKV // (3 * BW))),
            pl.BlockSpec((1, tt, RW_LORA), lambda i, j: (i, j, C_LORA // RW_LORA)),
            pl.BlockSpec((1, 1, 3 * BW), lambda i, j: (i, 0, 0)),
            pl.BlockSpec((1, 1, RW_LORA), lambda i, j: (i, 0, 0)),
            pl.BlockSpec((1, N_HEADS, HEAD_DIM, HEAD_DIM), lambda i, j: (i, 0, 0, 0)),
            full((1, 3 * BW)), full((1, RW_LORA)),
            full((1, BW)), full((64, BW)), full((1, BW)), full((64, BW)), full((128, BW)),
            full((1, BW)), full((1, BW)),
            full((1, BW)), full((1, BW)), full((1, BW)),
        ],
        out_specs=[
            pl.BlockSpec((1, tt, BW), lambda i, j: (i, j, 0)),
            pl.BlockSpec((1, N_HEADS, HEAD_DIM, HEAD_DIM), lambda i, j: (i, 0, 0, 0)),
        ],
        out_shape=[
            jax.ShapeDtypeStruct((b, t, BW), BF16),
            jax.ShapeDtypeStruct((b, N_HEADS, HEAD_DIM, HEAD_DIM), F32),
        ],
        scratch_shapes=[
            pltpu.VMEM((1, 3 * BW), F32), pltpu.VMEM((1, RW_LORA), F32),
            pltpu.VMEM((NPAIR, HEAD_DIM, 128), F32),
            wide(tt), wide(8 * (tt // CHUNK)), wide(tt), wide(tt),
            scr(), scr(), scr(), scr(), scr(), scr(), scr(),
            pltpu.VMEM((2, tt // CHUNK, NPAIR, 2 * CHUNK, 128), BF16),
            pltpu.VMEM((tt // CHUNK, NPAIR, CHUNK, 128), F32),
            pltpu.VMEM((tt // CHUNK, NPAIR, CHUNK, 256), BF16),
        ],
        compiler_params=_cparams(("parallel", "arbitrary")),
        name="rwkv7",
    )(u, u, sh_rkv, sh_lora, s0, row(mu[:3 * BW]), row(mu[3 * BW:]),
      row(p["rw_w0"]), p["rw_w2"], row(p["rw_a0"]), p["rw_a2"], p["rw_g2"],
      row(p["rw_kk"]), row(p["rw_ka"]), row(p["rw_rk"]), row(p["rw_ln_w"]), row(p["rw_ln_b"]))


def _pool_kernel(pos0, u_ref, prev_ref, hist_ref, w_ref, scale_ref, y_ref):
    tt = u_ref.shape[1]
    i = pl.program_id(1)
    x = u_ref[0].astype(F32)
    prev = jnp.where(i == 0, hist_ref[0], prev_ref[0].astype(F32))
    s = jnp.concatenate([prev, x], axis=0)
    pos = pos0 + i * tt + lax.broadcasted_iota(jnp.int32, (tt, 1), 0)
    outs = []
    for gi, win in enumerate(POOL_WINDOWS):
        s = s + pltpu.roll(s, win // 2, 0)
        sl = slice(gi * POOL_GW, (gi + 1) * POOL_GW)
        cnt = jnp.minimum(pos + 1, win).astype(F32)
        pg = s[16:, sl] / cnt - x[:, sl]
        outs.append(_mm(pg, w_ref[gi]))
    y_ref[0] = _bf(jnp.concatenate(outs, axis=1) * scale_ref[...])


def _pool(u, hist16, pos0, w_pool, scale, tt):
    b, t, _ = u.shape
    nprev = tt // 16
    return pl.pallas_call(
        functools.partial(_pool_kernel, pos0),
        grid=(b, t // tt),
        in_specs=[
            pl.BlockSpec((1, tt, BW), lambda i, j: (i, j, C_POOL // BW)),
            pl.BlockSpec((1, 16, BW), lambda i, j: (i, jnp.maximum(j * nprev - 1, 0), C_POOL // BW)),
            pl.BlockSpec((1, 16, BW), lambda i, j: (i, 0, 0)),
            pl.BlockSpec((4, POOL_GW, POOL_GW), lambda i, j: (0, 0, 0)),
            pl.BlockSpec((1, BW), lambda i, j: (0, 0)),
        ],
        out_specs=pl.BlockSpec((1, tt, BW), lambda i, j: (i, j, 0)),
        out_shape=jax.ShapeDtypeStruct((b, t, BW), BF16),
        compiler_params=_cparams(("parallel", "parallel")),
        name="pool",
    )(u, u, hist16, w_pool, scale.reshape(1, BW))


def _swa_kernel(mask_history, q_ref, kp2_ref, kp1_ref, kc_ref, vp2_ref, vp1_ref, vc_ref, sink_ref, y_ref):
    tq = q_ref.shape[1]
    nc = tq // CHUNK
    i = pl.program_id(1)
    f32 = lambda ref: ref[0].astype(F32)
    q = f32(q_ref)
    k_all = jnp.concatenate([f32(kp2_ref), f32(kp1_ref), f32(kc_ref)], axis=0)
    v_all = jnp.concatenate([f32(vp2_ref), f32(vp1_ref), f32(vc_ref)], axis=0)
    nk = 3 * CHUNK
    qi = lax.broadcasted_iota(jnp.int32, (CHUNK, nk), 0)
    si = lax.broadcasted_iota(jnp.int32, (CHUNK, nk), 1)
    dist1 = jnp.abs(qi + 2 * CHUNK - si).astype(F32)
    dist = jnp.concatenate([dist1] * SWA_GROUP, axis=0)
    scol = lax.broadcasted_iota(jnp.int32, (SWA_GROUP * CHUNK, nk), 1)
    units = [(c, g) for c in range(nc) for g in range(SWA_KV)]
    bias = []
    sinks = []
    for g in range(SWA_KV):
        heads = [g * SWA_GROUP + hh for hh in range(SWA_GROUP)]
        slope = jnp.concatenate(
            [jnp.full((CHUNK, 1), 2.0 ** (-8.0 * (h + 1) / N_HEADS), F32) for h in heads], axis=0)
        bias.append(slope * dist)
        sinks.append(jnp.concatenate([jnp.full((CHUNK, 1), sink_ref[h], F32) for h in heads], axis=0))

    def scores(c, g):
        ks = k_all[c * CHUNK:c * CHUNK + nk, g * HEAD_DIM:(g + 1) * HEAD_DIM]
        qs = jnp.concatenate(
            [q[c * CHUNK:(c + 1) * CHUNK, (g * SWA_GROUP + hh) * HEAD_DIM:(g * SWA_GROUP + hh + 1) * HEAD_DIM]
             for hh in range(SWA_GROUP)], axis=0)
        return _mm_nt(qs, ks)

    def probs(s, c, g):
        s = s * (HEAD_DIM ** -0.5) - bias[g]
        if mask_history:
            s = jnp.where(scol >= (2 - (i * nc + c)) * CHUNK, s, -1e30)
        m = jnp.maximum(jnp.max(s, axis=-1, keepdims=True), sinks[g])
        p = _bf(jnp.exp(s - m))
        rsum = jnp.dot(p, ones_k, preferred_element_type=F32)
        m_wide = jnp.broadcast_to(m, rsum.shape)
        inv = 1.0 / (rsum + jnp.exp(jnp.broadcast_to(sinks[g], rsum.shape) - m_wide))
        return p, inv[:, 0:HEAD_DIM]

    ones_k = jnp.ones((nk, 128), BF16)
    s_all = [scores(c, g) for c, g in units]
    p_all = [probs(s, c, g) for s, (c, g) in zip(s_all, units)]
    o_all = [_mm(p, v_all[c * CHUNK:c * CHUNK + nk, g * HEAD_DIM:(g + 1) * HEAD_DIM]) * inv
             for (p, inv), (c, g) in zip(p_all, units)]
    for o, (c, g) in zip(o_all, units):
        for hh in range(SWA_GROUP):
            h = g * SWA_GROUP + hh
            y_ref[0, c * CHUNK:(c + 1) * CHUNK, h * HEAD_DIM:(h + 1) * HEAD_DIM] = _bf(
                o[hh * CHUNK:(hh + 1) * CHUNK])


def _swa(u, hist_k, hist_v, sinks, tq):
    b, t, _ = u.shape
    per = tq // CHUNK
    kvw = SWA_KV * HEAD_DIM
    mask_history = hist_k is None
    cur_k = pl.BlockSpec((1, tq, kvw), lambda i, j: (i, j, C_KS // kvw))
    cur_v = pl.BlockSpec((1, tq, kvw), lambda i, j: (i, j, C_VS // kvw))
    if mask_history:
        prev = lambda d, col: pl.BlockSpec(
            (1, CHUNK, kvw), lambda i, j: (i, jnp.maximum(j * per - d, 0), col))
        specs = [prev(2, C_KS // kvw), prev(1, C_KS // kvw), cur_k,
                 prev(2, C_VS // kvw), prev(1, C_VS // kvw), cur_v]
        args = (u, u, u, u, u, u)
    else:
        assert t == tq
        hk = hist_k.reshape(b, WINDOW, kvw)
        hv = hist_v.reshape(b, WINDOW, kvw)
        hist = lambda blk: pl.BlockSpec((1, CHUNK, kvw), lambda i, j: (i, blk, 0))
        specs = [hist(0), hist(1), cur_k, hist(0), hist(1), cur_v]
        args = (hk, hk, u, hv, hv, u)
    return pl.pallas_call(
        functools.partial(_swa_kernel, mask_history),
        grid=(b, t // tq),
        in_specs=[pl.BlockSpec((1, tq, BW), lambda i, j: (i, j, C_Q // BW))] + specs
        + [pl.BlockSpec(memory_space=pltpu.SMEM)],
        out_specs=pl.BlockSpec((1, tq, BW), lambda i, j: (i, j, 0)),
        out_shape=jax.ShapeDtypeStruct((b, t, BW), BF16),
        compiler_params=_cparams(("parallel", "parallel")),
        name="swa",
    )(u, *args, sinks)


def _ssd_kernel(z_ref, x_ref, b_ref, c_ref, dt_ref, hx_ref, hb_ref, hc_ref, s0_ref,
                cwx_ref, cwb_ref, cwc_ref, cbx_ref, cbb_ref, cbc_ref,
                dtb_ref, alog_ref, dsk_ref, nw_ref,
                y_ref, sfin_ref,
                px_sc, pb_sc, pc_sc, s_sc, xa_sc, ba_sc, ca_sc, acs_sc, gend_sc,
                acsx_sc, eout_sc, y_sc, xdt_sc, xde_sc):
    tt = x_ref.shape[1]
    nc = tt // CHUNK
    i = pl.program_id(1)

    @pl.when(i == 0)
    def _():
        px_sc[...] = hx_ref[0]
        pb_sc[...] = hb_ref[0]
        pc_sc[...] = hc_ref[0]
        s_sc[...] = s0_ref[0].reshape(N_HEADS * HEAD_DIM, SSM_STATE)

    rid8 = lax.broadcasted_iota(jnp.int32, (8, 1), 0)

    def conv_silu(x, prev8, w_ref, b_ref_):
        acc = None
        for wi in range(SSM_CONV):
            sh = SSM_CONV - 1 - wi
            if sh == 0:
                xs = x
            else:
                rolled = pltpu.roll(x, sh, 0)
                top = jnp.where(rid8 < sh, pltpu.roll(prev8, sh, 0), rolled[0:8])
                xs = jnp.concatenate([top, rolled[8:]], axis=0)
            term = xs * w_ref[wi:wi + 1, :]
            acc = (b_ref_[...] + term) if acc is None else acc + term
        return _silu(acc)

    xr = x_ref[0].astype(F32)
    br = b_ref[0].astype(F32)
    cr = c_ref[0].astype(F32)
    xa_sc[...] = conv_silu(xr, px_sc[...], cwx_ref, cbx_ref)
    ba_sc[...] = conv_silu(br, pb_sc[...], cwb_ref, cbb_ref)
    ca_sc[...] = conv_silu(cr, pc_sc[...], cwc_ref, cbc_ref)
    px_sc[...] = xr[tt - 8:tt, :]
    pb_sc[...] = br[tt - 8:tt, :]
    pc_sc[...] = cr[tt - 8:tt, :]
    hl = 128
    dt = _softplus(dt_ref[0][:, 0:hl].astype(F32) + dtb_ref[...])
    ad = -jnp.exp(alog_ref[...]) * dt

    tri_incl_bf = _tri(CHUNK).astype(BF16)
    sel16 = (lax.broadcasted_iota(jnp.int32, (N_HEADS, hl), 0)
             == lax.broadcasted_iota(jnp.int32, (N_HEADS, hl), 1)).astype(BF16)
    max_terms = 3
    srow = lax.broadcasted_iota(jnp.int32, (hl, BW), 0)
    spread = (((srow & (N_HEADS - 1)) == lax.broadcasted_iota(jnp.int32, (hl, BW), 1) // HEAD_DIM)
              & (srow < max_terms * N_HEADS)).astype(BF16)
    head_lane = lax.broadcasted_iota(jnp.int32, (1, hl), 1) < N_HEADS

    def per_channel(x, terms):
        rem = jnp.where(head_lane, x, 0.0)
        packed = None
        for t in range(terms):
            part = _bf(rem).astype(F32)
            if t < terms - 1:
                rem = rem - part
            placed = part if t == 0 else pltpu.roll(part, t * N_HEADS, 1)
            packed = placed if packed is None else packed + placed
        return jnp.dot(_bf(packed), spread, preferred_element_type=F32)

    acs = jnp.concatenate(
        [_mm_sel(tri_incl_bf, ad[c * CHUNK:(c + 1) * CHUNK, :]) for c in range(nc)], axis=0)
    a_last = jnp.concatenate(
        [jnp.broadcast_to(acs[(c + 1) * CHUNK - 1:(c + 1) * CHUNK, :], (CHUNK, hl)) for c in range(nc)], axis=0)
    acs_sc[...] = acs
    acsx_sc[...] = per_channel(acs, 3)
    eout_sc[...] = per_channel(jnp.exp(acs), 2)
    gend_sc[...] = jnp.exp(jnp.concatenate(
        [jnp.broadcast_to(acs[(c + 1) * CHUNK - 1:(c + 1) * CHUNK, :], (8, hl)) for c in range(nc)], axis=0))
    xdt = xa_sc[...] * per_channel(dt, 2)
    xdt_sc[...] = _bf(xdt)
    xde_sc[...] = _bf(xdt * per_channel(jnp.exp(a_last - acs), 2))

    lane = lax.broadcasted_iota(jnp.int32, (CHUNK, 128), 1)
    row = lax.broadcasted_iota(jnp.int32, (CHUNK, 128), 0)
    tri_incl2 = row >= (lane & (HEAD_DIM - 1))
    r128 = lax.broadcasted_iota(jnp.int32, (128, 128), 0)
    c128 = lax.broadcasted_iota(jnp.int32, (128, 128), 1)
    same_head = (r128 < HEAD_DIM) == (c128 < HEAD_DIM)
    hpg = N_HEADS // SSM_GROUPS
    rpg = hpg * HEAD_DIM

    def chunk_body(c, carry):
        rows = pl.ds(pl.multiple_of(c * CHUNK, CHUNK), CHUNK)
        grow = pl.ds(pl.multiple_of(c * 8, 8), 8)
        bc = _bf(ba_sc[rows, :])
        cc = _bf(ca_sc[rows, :])
        acs_t = _mm_sel_nt(sel16, acs_sc[rows, :])
        g_end = gend_sc[grow, :][0:1, :]
        gsl = [slice(gi * SSM_STATE, (gi + 1) * SSM_STATE) for gi in range(SSM_GROUPS)]
        scores = [_mm_nt(cc[:, gs], bc[:, gs]) for gs in gsl]
        scores2 = [jnp.concatenate([s, s], axis=1) for s in scores]
        y_off = [_mm_nt(cc[:, gsl[gi]], s_sc[gi * rpg:(gi + 1) * rpg, :]) for gi in range(SSM_GROUPS)]
        upd = [_mm_tn(xde_sc[rows, gi * rpg:(gi + 1) * rpg], bc[:, gsl[gi]]) for gi in range(SSM_GROUPS)]
        y_diag = []
        for hp in range(N_HEADS // 2):
            sl = slice(hp * 128, (hp + 1) * 128)
            seg = acsx_sc[rows, sl] - jnp.concatenate([acs_t[2 * hp:2 * hp + 1, :],
                                                       acs_t[2 * hp + 1:2 * hp + 2, :]], axis=1)
            m = scores2[(2 * hp) // hpg] * jnp.exp(jnp.where(tri_incl2, seg, -1e30))
            xd = xdt_sc[rows, sl]
            y_diag.append(_mm(m, jnp.where(same_head, jnp.concatenate([xd, xd], axis=0),
                                           jnp.zeros((), BF16))))
        for h in range(N_HEADS):
            gi, hl_ = divmod(h, hpg)
            hrows = slice(h * HEAD_DIM, (h + 1) * HEAD_DIM)
            s_sc[hrows, :] = (s_sc[hrows, :] * g_end[:, h:h + 1]
                              + upd[gi][hl_ * HEAD_DIM:(hl_ + 1) * HEAD_DIM, :])
        y_sc[rows, :] = (jnp.concatenate(y_diag, axis=1)
                         + jnp.concatenate(y_off, axis=1) * eout_sc[rows, :])
        return carry

    lax.fori_loop(0, nc, chunk_body, 0)

    y = (y_sc[...] + xa_sc[...] * dsk_ref[...]) * _silu(z_ref[0].astype(F32))
    gw = BW // SSM_GROUPS
    parts = []
    for gi in range(SSM_GROUPS):
        yg = y[:, gi * gw:(gi + 1) * gw]
        parts.append(yg * lax.rsqrt(jnp.mean(yg * yg, axis=-1, keepdims=True) + NORM_EPS))
    y_ref[0] = _bf(jnp.concatenate(parts, axis=1) * nw_ref[...])

    @pl.when(i == pl.num_programs(1) - 1)
    def _():
        sfin_ref[0] = s_sc[...].reshape(N_HEADS, HEAD_DIM, SSM_STATE)


def _pad_rows8(a):
    return jnp.pad(a, ((0, 0), (8 - a.shape[1], 0), (0, 0)))


def _pad_lanes(a, n):
    return jnp.pad(a.reshape(1, -1), ((0, 0), (0, n - a.shape[-1])))


def _ssd(u, conv_hist, s0, p, tt):
    b, t, _ = u.shape
    h8 = _pad_rows8(conv_hist)
    hx, hb, hc = h8[:, :, :BW], h8[:, :, BW:BW + SSM_BC], h8[:, :, BW + SSM_BC:]
    cw, cb = p["ssm_conv_w"], p["ssm_conv_b"].reshape(1, -1)
    full = lambda shape: pl.BlockSpec(shape, lambda i, j: (0,) * len(shape))
    col = lambda w, c0: pl.BlockSpec((1, tt, w), lambda i, j: (i, j, c0 // w))
    hist = lambda w: pl.BlockSpec((1, 8, w), lambda i, j: (i, 0, 0))
    return pl.pallas_call(
        _ssd_kernel,
        grid=(b, t // tt),
        in_specs=[
            col(BW, C_Z), col(BW, C_X), col(SSM_BC, C_B), col(SSM_BC, C_C), col(SSM_BC, C_DT),
            hist(BW), hist(SSM_BC), hist(SSM_BC),
            pl.BlockSpec((1, N_HEADS, HEAD_DIM, SSM_STATE), lambda i, j: (i, 0, 0, 0)),
            full((SSM_CONV, BW)), full((SSM_CONV, SSM_BC)), full((SSM_CONV, SSM_BC)),
            full((1, BW)), full((1, SSM_BC)), full((1, SSM_BC)),
            full((1, 128)), full((1, 128)), full((1, BW)), full((1, BW)),
        ],
        out_specs=[
            pl.BlockSpec((1, tt, BW), lambda i, j: (i, j, 0)),
            pl.BlockSpec((1, N_HEADS, HEAD_DIM, SSM_STATE), lambda i, j: (i, 0, 0, 0)),
        ],
        out_shape=[
            jax.ShapeDtypeStruct((b, t, BW), BF16),
            jax.ShapeDtypeStruct((b, N_HEADS, HEAD_DIM, SSM_STATE), F32),
        ],
        scratch_shapes=[
            pltpu.VMEM((8, BW), F32), pltpu.VMEM((8, SSM_BC), F32), pltpu.VMEM((8, SSM_BC), F32),
            pltpu.VMEM((N_HEADS * HEAD_DIM, SSM_STATE), F32),
            pltpu.VMEM((tt, BW), F32), pltpu.VMEM((tt, SSM_BC), F32), pltpu.VMEM((tt, SSM_BC), F32),
            pltpu.VMEM((tt, 128), F32), pltpu.VMEM((8 * (tt // CHUNK), 128), F32),
            pltpu.VMEM((tt, BW), F32), pltpu.VMEM((tt, BW), F32), pltpu.VMEM((tt, BW), F32),
            pltpu.VMEM((tt, BW), BF16), pltpu.VMEM((tt, BW), BF16),
        ],
        compiler_params=_cparams(("parallel", "arbitrary")),
        name="ssd",
    )(u, u, u, u, u, hx, hb, hc, s0,
      cw[:, :BW], cw[:, BW:BW + SSM_BC], cw[:, BW + SSM_BC:],
      cb[:, :BW], cb[:, BW:BW + SSM_BC], cb[:, BW + SSM_BC:],
      _pad_lanes(p["ssm_dt_bias"], 128), _pad_lanes(p["ssm_a_log"], 128),
      jnp.repeat(p["ssm_d"], HEAD_DIM).reshape(1, BW), p["ssm_norm"].reshape(1, BW))


_RELAYOUT_BLOCK = 256


def _w_in_block_order():
    blk = _RELAYOUT_BLOCK
    segs = [(0, 3 * BW), (_N_SSM, BW), (_N_POOL, BW), (_N_SWA, BW), (_N_SSM + BW, BW), (3 * BW, RW_LORA),
            (_N_SWA + BW, 2 * SWA_KV * HEAD_DIM), (_N_SSM + 2 * BW, 2 * SSM_BC + blk)]
    order = [start // blk + i for start, width in segs for i in range(width // blk)]
    assert len(order) == U_COLS // blk and all(start % blk == 0 for start, _ in segs)
    return order


def _relayout_kernel(order_ref, w_ref, o_ref):
    src = order_ref[pl.program_id(1)]
    last = IN_COLS // _RELAYOUT_BLOCK
    valid = jnp.where(src == last, IN_COLS - last * _RELAYOUT_BLOCK, _RELAYOUT_BLOCK)
    out_col = lax.broadcasted_iota(jnp.int32, (_RELAYOUT_BLOCK, 1), 0)
    o_ref[...] = _bf(jnp.where(out_col < valid, w_ref[...], 0.0))


def _prep_w_in(w_in):
    blk = _RELAYOUT_BLOCK
    order = jnp.asarray(_w_in_block_order(), jnp.int32)
    return pl.pallas_call(
        _relayout_kernel,
        grid_spec=pltpu.PrefetchScalarGridSpec(
            num_scalar_prefetch=1,
            grid=(DEPTH, U_COLS // blk),
            in_specs=[pl.BlockSpec((None, blk, D_MODEL), lambda l, j, order: (l, order[j], 0))],
            out_specs=pl.BlockSpec((None, blk, D_MODEL), lambda l, j, order: (l, j, 0)),
        ),
        out_shape=jax.ShapeDtypeStruct((DEPTH, U_COLS, D_MODEL), BF16),
        compiler_params=_cparams(("parallel", "parallel")),
        name="w_in_relayout",
    )(order, jnp.swapaxes(w_in, 1, 2))


def _run_group(x, mods, st, p, final_norm, wts, pos0, bb):
    b, t, _ = x.shape
    kvw = SWA_KV * HEAD_DIM
    tt_wide, tt_merge = min(t, ROWS_WIDE // bb), min(t, ROWS_MERGE // bb)
    tt_mixer, tt_rwkv, tt_swa = min(t, ROWS_MIXER), min(t, ROWS_RWKV), min(t, ROWS_SWA)
    outs = {k: [] for k in ("rwkv", "shift", "pool", "k", "v", "ssm", "conv")}
    for l in range(DEPTH):
        pl_ = {k: v[l] for k, v in p.items()}
        mod = mods[l]
        u, h = _inproj(x, pl_["norm_mix"], mod, wts["w_in"], l, bb, tt_wide)
        if st is None:
            shift_hist = jnp.zeros((b, 1, RW_IN), F32)
            s_rwkv = jnp.zeros((b, N_HEADS, HEAD_DIM, HEAD_DIM), F32)
            pool_hist = jnp.zeros((b, POOL_HIST, BW), F32)
            hk = hv = None
            s_ssm = jnp.zeros((b, N_HEADS, HEAD_DIM, SSM_STATE), F32)
            conv_hist = jnp.zeros((b, SSM_CONV - 1, SSM_CONV_DIM), F32)
        else:
            shift_hist, s_rwkv, pool_hist = st["shift"][l], st["rwkv"][l], st["pool"][l]
            hk, hv, s_ssm, conv_hist = st["k"][l], st["v"][l], st["ssm"][l], st["conv"][l]
        y_a, n_rwkv = _rwkv(u, shift_hist, s_rwkv, pl_, tt_rwkv)
        hist16 = jnp.pad(pool_hist, ((0, 0), (1, 0), (0, 0)))
        y_b = _pool(u, hist16, pos0, wts["pool_w"][l], pl_["pool_scale"], tt_mixer)
        y_c = _swa(u, hk, hv, pl_["swa_sinks"], tt_swa)
        y_d, n_ssm = _ssd(u, conv_hist, s_ssm, pl_, tt_mixer)
        merged = _merge(h, (y_a, y_b, y_c, y_d), wts["w_gate"], wts["w_branch"], l, bb, tt_merge)
        x, h2 = _outproj(merged, wts["w_out"], l, x, mod, pl_["norm_ffn"], bb, tt_wide)
        x = _ffn(x, h2, mod, wts["w_up"], wts["w_down"], l, final_norm, l == DEPTH - 1, bb, tt_wide)

        outs["rwkv"].append(n_rwkv)
        outs["shift"].append(jnp.concatenate(
            [u[:, t - 1:, C_RKV:C_RKV + 3 * BW], u[:, t - 1:, C_LORA:C_LORA + RW_LORA]], axis=-1).astype(F32))
        outs["pool"].append(u[:, t - POOL_HIST:, C_POOL:C_POOL + BW].astype(F32))
        k_new = u[:, t - min(t, WINDOW):, C_KS:C_KS + kvw].astype(F32)
        v_new = u[:, t - min(t, WINDOW):, C_VS:C_VS + kvw].astype(F32)
        if hk is not None:
            k_new = jnp.concatenate([hk.reshape(b, WINDOW, kvw), k_new], axis=1)
            v_new = jnp.concatenate([hv.reshape(b, WINDOW, kvw), v_new], axis=1)
        outs["k"].append(k_new[:, -WINDOW:].reshape(b, WINDOW, SWA_KV, HEAD_DIM))
        outs["v"].append(v_new[:, -WINDOW:].reshape(b, WINDOW, SWA_KV, HEAD_DIM))
        outs["ssm"].append(n_ssm)
        outs["conv"].append(jnp.concatenate(
            [u[:, t - 3:, C_X:C_X + BW], u[:, t - 3:, C_B:C_B + 2 * SSM_BC]], axis=-1).astype(F32))
    order = ("rwkv", "shift", "pool", "k", "v", "ssm", "conv")
    return x, tuple(jnp.stack(outs[k]) for k in order)


def kernel(x_prompt, x_sample, state_rwkv, state_rwkv_shift, state_pool, cache_swa_k, cache_swa_v,
           state_ssm, state_ssm_conv, c_prompt, c_sample, ada_w, ada_b, norm_mix, norm_ffn, w_in,
           rw_mu, rw_w0, rw_w2, rw_a0, rw_a2, rw_g2, rw_kk, rw_ka, rw_rk, rw_ln_w, rw_ln_b, pool_w,
           pool_scale, swa_sinks, ssm_conv_w, ssm_conv_b, ssm_dt_bias, ssm_a_log, ssm_d, ssm_norm,
           w_gate, w_branch, w_out, w_up, w_down, final_norm):
    bp, bs = x_prompt.shape[0], x_sample.shape[0]
    p = dict(norm_mix=norm_mix, norm_ffn=norm_ffn, rw_mu=rw_mu, rw_w0=rw_w0, rw_w2=rw_w2, rw_a0=rw_a0,
             rw_a2=rw_a2, rw_g2=rw_g2, rw_kk=rw_kk, rw_ka=rw_ka, rw_rk=rw_rk, rw_ln_w=rw_ln_w,
             rw_ln_b=rw_ln_b, pool_scale=pool_scale, swa_sinks=swa_sinks, ssm_conv_w=ssm_conv_w,
             ssm_conv_b=ssm_conv_b, ssm_dt_bias=ssm_dt_bias, ssm_a_log=ssm_a_log, ssm_d=ssm_d,
             ssm_norm=ssm_norm)
    wts = dict(w_in=_prep_w_in(w_in), pool_w=pool_w.astype(BF16), w_gate=w_gate.astype(BF16),
               w_branch=w_branch.astype(BF16), w_out=w_out.astype(BF16), w_up=w_up.astype(BF16),
               w_down=w_down.astype(BF16))

    nb = bp + bs
    nb_pad = -(-nb // 8) * 8
    c_all = jnp.pad(jnp.concatenate([c_prompt, c_sample], axis=0), ((0, nb_pad - nb), (0, 0)))
    mod_all = _adaln(c_all, ada_w, ada_b)
    mods_p = mod_all[:, :bp].reshape(DEPTH, bp, 6, 1, D_MODEL)
    mods_s = mod_all[:, bp:nb].reshape(DEPTH, bs, 6, 1, D_MODEL)

    y_prompt, st_p = _run_group(x_prompt, mods_p, None, p, final_norm, wts, 0, 1)
    st_s = dict(rwkv=state_rwkv, shift=state_rwkv_shift, pool=state_pool, k=cache_swa_k,
                v=cache_swa_v, ssm=state_ssm, conv=state_ssm_conv)
    y_sample, st_o = _run_group(x_sample, mods_s, st_s, p, final_norm, wts, PAST_LEN, bs)
    return (y_prompt, y_sample) + st_p + st_o
```
